```python
import jax, jax.numpy as jnp
from jax import lax
import numpy as np

D_MODEL = 2048
BATCH = 8
SEQ = 2048
DEPTH = 2

HEAD_DIM = 128
BLOCK = 128
ROPE_THETA = 10000.0
RMS_EPS = 1e-6
NEG_INF = -1e30

LRU_WIDTH = D_MODEL // 2
LRU_BLOCKS = 8
LRU_BLOCK_DIM = LRU_WIDTH // LRU_BLOCKS
LRU_CONV = 4
LRU_C = 8.0

DIL_CONFIGS = ((128, 1), (512, 4), (2048, 16))
DIL_GROUPS = 3
DIL_HEADS = 4
DIL_WIDTH = DIL_GROUPS * DIL_HEADS * HEAD_DIM
DIL_OUT = DIL_HEADS * HEAD_DIM

SB_HEADS = (D_MODEL // 2) // HEAD_DIM
SB_WIDTH = SB_HEADS * HEAD_DIM

RWKV_HEAD = 64
RWKV_WIDTH = D_MODEL // 2
RWKV_HEADS = RWKV_WIDTH // RWKV_HEAD
RWKV_W_LORA = 64
RWKV_A_LORA = 64
RWKV_G_LORA = 160
RWKV_V_LORA = 32
RWKV_GN_EPS = 64e-5
RWKV_IN = 3 * RWKV_WIDTH + RWKV_W_LORA + RWKV_A_LORA + RWKV_G_LORA

A_IN = 2 * LRU_WIDTH
B_IN = 3 * DIL_WIDTH
C_IN = 3 * SB_WIDTH
OFF_B = A_IN
OFF_C = OFF_B + B_IN
OFF_D = OFF_C + C_IN
N_IN = OFF_D + RWKV_IN
N_BRANCH = 4

D_FF = (11 * D_MODEL) // 4
FFN_CONV = 3
PLE_DIM = 256

kernel_name = 'hybrid_gated_parallel_mixers'


def rms_norm(x, g):
    xf = x.astype(jnp.float32)
    y = xf * lax.rsqrt(jnp.mean(xf * xf, axis=-1, keepdims=True) + RMS_EPS)
    return (y * g.astype(jnp.float32)).astype(x.dtype)


def causal_dwconv(x, w, b):
    width, channels = w.shape
    y = lax.conv_general_dilated(x, w[:, None, :], window_strides=(1,), padding=[(width - 1, 0)],
                                 dimension_numbers=('NWC', 'WIO', 'NWC'), feature_group_count=channels)
    return y + b


def token_shift(x):
    return jnp.pad(x, ((0, 0), (1, 0), (0, 0)))[:, :-1]


def rope(x, pos):
    half = x.shape[-1] // 2
    inv_freq = ROPE_THETA ** (-jnp.arange(half, dtype=jnp.float32) / half)
    ang = pos.astype(jnp.float32)[:, None] * inv_freq[None, :]
    cos = jnp.cos(ang)[None, :, None, :]
    sin = jnp.sin(ang)[None, :, None, :]
    xf = x.astype(jnp.float32)
    x1, x2 = xf[..., :half], xf[..., half:]
    return jnp.concatenate([x1 * cos - x2 * sin, x2 * cos + x1 * sin], axis=-1).astype(x.dtype)


def rglru_mixer(x_in, gate_in, conv_w, conv_b, w_r, b_r, w_i, b_i, lam):
    B, S, W = x_in.shape
    u = causal_dwconv(x_in, conv_w, conv_b)
    ub = u.reshape(B, S, LRU_BLOCKS, LRU_BLOCK_DIM)
    r = jax.nn.sigmoid((jnp.einsum('bshi,hij->bshj', ub, w_r).reshape(B, S, W) + b_r).astype(jnp.float32))
    ig = jax.nn.sigmoid((jnp.einsum('bshi,hij->bshj', ub, w_i).reshape(B, S, W) + b_i).astype(jnp.float32))
    log_a = -LRU_C * r * jax.nn.softplus(-lam.astype(jnp.float32))
    a = jnp.exp(log_a)
    inp = jnp.sqrt(1.0 - jnp.exp(2.0 * log_a)) * ig * u.astype(jnp.float32)

    def combine(left, right):
        a1, h1 = left
        a2, h2 = right
        return a1 * a2, a2 * h1 + h2

    _, h = lax.associative_scan(combine, (a, inp), axis=1)
    return (h * jax.nn.gelu(gate_in.astype(jnp.float32))).astype(x_in.dtype)


def dilated_window_attention(q, k, v, dilation, span):
    B, S, H, Dh = q.shape
    L = S // dilation
    nb = -(-L // BLOCK)
    pad = nb * BLOCK - L

    def to_blocks(t):
        t = t.reshape(B, L, dilation, H, Dh).transpose(0, 2, 3, 1, 4)
        t = jnp.pad(t, ((0, 0), (0, 0), (0, 0), (0, pad), (0, 0)))
        return t.reshape(B, dilation, H, nb, BLOCK, Dh)

    def with_prev(t):
        prev = jnp.pad(t, ((0, 0), (0, 0), (0, 0), (1, 0), (0, 0), (0, 0)))[:, :, :, :-1]
        return jnp.concatenate([prev, t], axis=4)

    qb = to_blocks(q)
    kw = with_prev(to_blocks(k))
    vw = with_prev(to_blocks(v)).astype(jnp.float32)
    s = jnp.einsum('bdhnqe,bdhnke->bdhnqk', qb, kw, preferred_element_type=jnp.float32) * (Dh ** -0.5)
    qi = jnp.arange(BLOCK)[:, None]
    kj = jnp.arange(2 * BLOCK)[None, :]
    dist = qi + BLOCK - kj
    key_l = (jnp.arange(nb) * BLOCK)[:, None, None] - BLOCK + kj[None]
    mask = (dist >= 0)[None] & (dist <= span)[None] & (key_l >= 0)
    s = jnp.where(mask, s, NEG_INF)
    m = jnp.max(s, axis=-1)
    e = jnp.exp(s - m[..., None])
    den = jnp.sum(e, axis=-1)
    o = jnp.einsum('bdhnqk,bdhnke->bdhnqe', e, vw) / den[..., None]
    lse = m + jnp.log(den)

    def from_blocks(t):
        t = t.reshape((B, dilation, H, nb * BLOCK) + t.shape[5:])[:, :, :, :L]
        t = jnp.moveaxis(t, 3, 1)
        return t.reshape((B, S, H) + t.shape[4:])

    return from_blocks(o), from_blocks(lse)


def dilated_mixer(seg, pos):
    B, S, _ = seg.shape
    qkv = seg.reshape(B, S, 3, DIL_GROUPS * DIL_HEADS, HEAD_DIM)
    q = rope(qkv[:, :, 0], pos).reshape(B, S, DIL_GROUPS, DIL_HEADS, HEAD_DIM)
    k = rope(qkv[:, :, 1], pos).reshape(B, S, DIL_GROUPS, DIL_HEADS, HEAD_DIM)
    v = qkv[:, :, 2].reshape(B, S, DIL_GROUPS, DIL_HEADS, HEAD_DIM)
    outs, lses = [], []
    for g, (window, dilation) in enumerate(DIL_CONFIGS):
        o_g, lse_g = dilated_window_attention(q[:, :, g], k[:, :, g], v[:, :, g], dilation, window // dilation)
        outs.append(o_g)
        lses.append(lse_g)
    wts = jax.nn.softmax(jnp.stack(lses, axis=0), axis=0)
    o = jnp.einsum('gbsh,gbshe->bshe', wts, jnp.stack(outs, axis=0))
    return o.reshape(B, S, DIL_OUT).astype(seg.dtype)


def stick_breaking_mixer(seg):
    B, S, _ = seg.shape
    qkv = seg.reshape(B, S, 3, SB_HEADS, HEAD_DIM)
    q, k, v = (jnp.moveaxis(qkv[:, :, j], 2, 1) for j in range(3))
    vf = v.astype(jnp.float32)
    nb = S // BLOCK
    qb = jnp.moveaxis(q.reshape(B, SB_HEADS, nb, BLOCK, HEAD_DIM), 2, 0)
    key_pos = jnp.arange(S)
    scale = HEAD_DIM ** -0.5

    def one_block(args):
        q_blk, n = args
        z = jnp.einsum('bhqe,bhke->bhqk', q_blk, k, preferred_element_type=jnp.float32) * scale
        q_pos = n * BLOCK + jnp.arange(BLOCK)
        before = key_pos[None, :] < q_pos[:, None]
        log_1m = jnp.where(before, jax.nn.log_sigmoid(-z), 0.0)
        suffix = lax.cumsum(log_1m, axis=3, reverse=True) - log_1m
        att = jnp.where(before, jnp.exp(jax.nn.log_sigmoid(z) + suffix), 0.0)
        return jnp.einsum('bhqk,bhke->bhqe', att, vf)

    o = lax.map(one_block, (qb, jnp.arange(nb)))
    o = jnp.moveaxis(o, 0, 2).reshape(B, SB_HEADS, S, HEAD_DIM)
    return jnp.moveaxis(o, 1, 2).reshape(B, S, SB_WIDTH).astype(seg.dtype)


def rwkv7_mixer(seg, mu, w0, w_up, a0, a_up, g_up, k_k, k_a, r_k, gn_w, gn_b, v_first, v_res):
    B, S, _ = seg.shape
    W = RWKV_WIDTH
    seg = seg + (token_shift(seg) - seg) * mu
    r = seg[..., :W]
    k = seg[..., W:2 * W]
    v = seg[..., 2 * W:3 * W]
    o1 = 3 * W
    o2 = o1 + RWKV_W_LORA
    o3 = o2 + RWKV_A_LORA
    w_low, a_low, g_low = seg[..., o1:o2], seg[..., o2:o3], seg[..., o3:]
    w = -jax.nn.softplus(-(w0 + jnp.tanh(w_low) @ w_up).astype(jnp.float32)) - 0.5
    decay = jnp.exp(-jnp.exp(w))
    a = jax.nn.sigmoid((a0 + a_low @ a_up).astype(jnp.float32))
    g = jax.nn.sigmoid(g_low) @ g_up
    if v_res is None:
        v_first = v
    else:
        v0, v_down, v_up = v_res
        v = v + (v_first - v) * jax.nn.sigmoid(v0 + (v @ v_down) @ v_up)

    def heads(t):
        return t.astype(jnp.float32).reshape(B, S, RWKV_HEADS, RWKV_HEAD)

    r, k, vh, decay, a = (heads(t) for t in (r, k, v, decay, a))
    kk = k * k_k.astype(jnp.float32).reshape(RWKV_HEADS, RWKV_HEAD)
    kk = kk / jnp.maximum(jnp.linalg.norm(kk, axis=-1, keepdims=True), 1e-12)
    k = k * (1.0 + (a - 1.0) * k_a.astype(jnp.float32).reshape(RWKV_HEADS, RWKV_HEAD))
    xs = tuple(jnp.moveaxis(t, 1, 0) for t in (r, decay, k, vh, kk, kk * a))

    def step(state, inp):
        r_t, w_t, k_t, v_t, kk_t, b_t = inp
        sa = jnp.einsum('bhij,bhj->bhi', state, kk_t)
        state = state * w_t[:, :, None, :] - sa[..., None] * b_t[:, :, None, :] + v_t[..., None] * k_t[:, :, None, :]
        return state, jnp.einsum('bhij,bhj->bhi', state, r_t)

    state0 = jnp.zeros((B, RWKV_HEADS, RWKV_HEAD, RWKV_HEAD), jnp.float32)
    _, y = lax.scan(step, state0, xs)
    y = jnp.moveaxis(y, 0, 1)
    mean = jnp.mean(y, axis=-1, keepdims=True)
    var = jnp.mean(jnp.square(y - mean), axis=-1, keepdims=True)
    y = ((y - mean) * lax.rsqrt(var + RWKV_GN_EPS)).reshape(B, S, W) * gn_w + gn_b
    bonus = jnp.sum(r * k * r_k.astype(jnp.float32), axis=-1, keepdims=True) * vh
    y = (y + bonus.reshape(B, S, W)) * g.astype(jnp.float32)
    return y.astype(seg.dtype), v_first


def conv_ffn(h, w_up, conv_w, conv_b, w_down):
    u = causal_dwconv(h @ w_up, conv_w, conv_b)
    gate, up = jnp.split(u, 2, axis=-1)
    return (jax.nn.gelu(gate) * up) @ w_down


def setup_inputs(seed: int = 0) -> dict:
    key = jax.random.key(seed)
    ks = iter(jax.random.split(key, 48))

    def nrm(shape, scale):
        return scale * jax.random.normal(next(ks), shape, jnp.float32)

    def gain():
        return 1.0 + nrm((DEPTH, D_MODEL), 0.05)

    x = nrm((BATCH, SEQ, D_MODEL), 1.0)
    p = nrm((DEPTH, BATCH, SEQ, PLE_DIM), 1.0)
    norm_mix_pre = gain()
    norm_mix_post = gain()
    norm_ffn_pre = gain()
    norm_ffn_post = gain()
    norm_ple_pre = gain()
    norm_ple_post = gain()
    w_in = nrm((DEPTH, D_MODEL, N_IN), D_MODEL ** -0.5)
    w_merge_gate = nrm((DEPTH, D_MODEL, N_BRANCH * D_MODEL), D_MODEL ** -0.5)
    lru_conv_w = nrm((DEPTH, LRU_CONV, LRU_WIDTH), LRU_CONV ** -0.5)
    lru_conv_b = nrm((DEPTH, LRU_WIDTH), 0.02)
    lru_w_r = nrm((DEPTH, LRU_BLOCKS, LRU_BLOCK_DIM, LRU_BLOCK_DIM), LRU_BLOCK_DIM ** -0.5)
    lru_b_r = nrm((DEPTH, LRU_WIDTH), 0.02)
    lru_w_i = nrm((DEPTH, LRU_BLOCKS, LRU_BLOCK_DIM, LRU_BLOCK_DIM), LRU_BLOCK_DIM ** -0.5)
    lru_b_i = nrm((DEPTH, LRU_WIDTH), 0.02)
    a_init = jax.random.uniform(next(ks), (DEPTH, LRU_WIDTH), jnp.float32, minval=0.9, maxval=0.999)
    lru_lambda = jnp.log(a_init) - jnp.log1p(-a_init)
    rwkv_mu = jax.random.uniform(next(ks), (DEPTH, RWKV_IN), jnp.float32)
    rwkv_w0 = nrm((DEPTH, RWKV_WIDTH), 0.5)
    rwkv_w_up = nrm((DEPTH, RWKV_W_LORA, RWKV_WIDTH), 0.1)
    rwkv_a0 = nrm((DEPTH, RWKV_WIDTH), 0.1)
    rwkv_a_up = nrm((DEPTH, RWKV_A_LORA, RWKV_WIDTH), 0.1)
    rwkv_g_up = nrm((DEPTH, RWKV_G_LORA, RWKV_WIDTH), RWKV_G_LORA ** -0.5)
    rwkv_k_k = 0.85 + nrm((DEPTH, RWKV_WIDTH), 0.05)
    rwkv_k_a = 1.0 + nrm((DEPTH, RWKV_WIDTH), 0.05)
    rwkv_r_k = nrm((DEPTH, RWKV_HEADS, RWKV_HEAD), 0.1)
    rwkv_gn_w = 1.0 + nrm((DEPTH, RWKV_WIDTH), 0.05)
    rwkv_gn_b = nrm((DEPTH, RWKV_WIDTH), 0.02)
    rwkv_v0 = nrm((DEPTH - 1, RWKV_WIDTH), 0.1)
    rwkv_v_down = nrm((DEPTH - 1, RWKV_WIDTH, RWKV_V_LORA), RWKV_WIDTH ** -0.5)
    rwkv_v_up = nrm((DEPTH - 1, RWKV_V_LORA, RWKV_WIDTH), RWKV_V_LORA ** -0.5)
    w_branch_a = nrm((DEPTH, LRU_WIDTH, D_MODEL), LRU_WIDTH ** -0.5)
    w_branch_b = nrm((DEPTH, DIL_OUT, D_MODEL), DIL_OUT ** -0.5)
    w_branch_c = nrm((DEPTH, SB_WIDTH, D_MODEL), SB_WIDTH ** -0.5)
    w_branch_d = nrm((DEPTH, RWKV_WIDTH, D_MODEL), RWKV_WIDTH ** -0.5)
    w_out = nrm((DEPTH, D_MODEL, D_MODEL), D_MODEL ** -0.5)
    w_ffn_up = nrm((DEPTH, D_MODEL, 2 * D_FF), D_MODEL ** -0.5)
    ffn_conv_w = nrm((DEPTH, FFN_CONV, 2 * D_FF), FFN_CONV ** -0.5)
    ffn_conv_b = nrm((DEPTH, 2 * D_FF), 0.02)
    w_ffn_down = nrm((DEPTH, D_FF, D_MODEL), D_FF ** -0.5)
    w_ple = nrm((DEPTH, PLE_DIM, D_MODEL), PLE_DIM ** -0.5)
    w_ple_gate = nrm((DEPTH, D_MODEL, D_MODEL), D_MODEL ** -0.5)
    return {
        'x': x, 'p': p,
        'norm_mix_pre': norm_mix_pre, 'norm_mix_post': norm_mix_post,
        'norm_ffn_pre': norm_ffn_pre, 'norm_ffn_post': norm_ffn_post,
        'norm_ple_pre': norm_ple_pre, 'norm_ple_post': norm_ple_post,
        'w_in': w_in, 'w_merge_gate': w_merge_gate,
        'lru_conv_w': lru_conv_w, 'lru_conv_b': lru_conv_b,
        'lru_w_r': lru_w_r, 'lru_b_r': lru_b_r, 'lru_w_i': lru_w_i, 'lru_b_i': lru_b_i,
        'lru_lambda': lru_lambda,
        'rwkv_mu': rwkv_mu, 'rwkv_w0': rwkv_w0, 'rwkv_w_up': rwkv_w_up,
        'rwkv_a0': rwkv_a0, 'rwkv_a_up': rwkv_a_up, 'rwkv_g_up': rwkv_g_up,
        'rwkv_k_k': rwkv_k_k, 'rwkv_k_a': rwkv_k_a, 'rwkv_r_k': rwkv_r_k,
        'rwkv_gn_w': rwkv_gn_w, 'rwkv_gn_b': rwkv_gn_b,
        'rwkv_v0': rwkv_v0, 'rwkv_v_down': rwkv_v_down, 'rwkv_v_up': rwkv_v_up,
        'w_branch_a': w_branch_a, 'w_branch_b': w_branch_b,
        'w_branch_c': w_branch_c, 'w_branch_d': w_branch_d,
        'w_out': w_out,
        'w_ffn_up': w_ffn_up, 'ffn_conv_w': ffn_conv_w, 'ffn_conv_b': ffn_conv_b, 'w_ffn_down': w_ffn_down,
        'w_ple': w_ple, 'w_ple_gate': w_ple_gate,
    }


def reference(x, p, norm_mix_pre, norm_mix_post, norm_ffn_pre, norm_ffn_post, norm_ple_pre, norm_ple_post,
              w_in, w_merge_gate, lru_conv_w, lru_conv_b, lru_w_r, lru_b_r, lru_w_i, lru_b_i, lru_lambda,
              rwkv_mu, rwkv_w0, rwkv_w_up, rwkv_a0, rwkv_a_up, rwkv_g_up, rwkv_k_k, rwkv_k_a, rwkv_r_k,
              rwkv_gn_w, rwkv_gn_b, rwkv_v0, rwkv_v_down, rwkv_v_up,
              w_branch_a, w_branch_b, w_branch_c, w_branch_d, w_out,
              w_ffn_up, ffn_conv_w, ffn_conv_b, w_ffn_down, w_ple, w_ple_gate):
    B, S, _ = x.shape
    pos = jnp.arange(S)
    v_first = None
    for i in range(DEPTH):
        h = rms_norm(x, norm_mix_pre[i])
        proj = h @ w_in[i]
        y_a = rglru_mixer(proj[..., :LRU_WIDTH], proj[..., LRU_WIDTH:OFF_B], lru_conv_w[i], lru_conv_b[i],
                          lru_w_r[i], lru_b_r[i], lru_w_i[i], lru_b_i[i], lru_lambda[i])
        y_b = dilated_mixer(proj[..., OFF_B:OFF_C], pos)
        y_c = stick_breaking_mixer(proj[..., OFF_C:OFF_D])
        v_res = (rwkv_v0[i - 1], rwkv_v_down[i - 1], rwkv_v_up[i - 1]) if i > 0 else None
        y_d, v_first = rwkv7_mixer(proj[..., OFF_D:], rwkv_mu[i], rwkv_w0[i], rwkv_w_up[i], rwkv_a0[i],
                                   rwkv_a_up[i], rwkv_g_up[i], rwkv_k_k[i], rwkv_k_a[i], rwkv_r_k[i],
                                   rwkv_gn_w[i], rwkv_gn_b[i], v_first, v_res)
        gates = jax.nn.sigmoid(h @ w_merge_gate[i]).reshape(B, S, N_BRANCH, D_MODEL)
        merged = (gates[:, :, 0] * (y_a @ w_branch_a[i]) + gates[:, :, 1] * (y_b @ w_branch_b[i])
                  + gates[:, :, 2] * (y_c @ w_branch_c[i]) + gates[:, :, 3] * (y_d @ w_branch_d[i]))
        x = x + rms_norm(merged @ w_out[i], norm_mix_post[i])
        h = rms_norm(x, norm_ffn_pre[i])
        x = x + rms_norm(conv_ffn(h, w_ffn_up[i], ffn_conv_w[i], ffn_conv_b[i], w_ffn_down[i]), norm_ffn_post[i])
        gate = jax.nn.sigmoid(rms_norm(x, norm_ple_pre[i]) @ w_ple_gate[i])
        x = x + rms_norm((p[i] @ w_ple[i]) * gate, norm_ple_post[i])
    return x
```

```python
import functools

import jax
import jax.numpy as jnp
from jax import lax
from jax.experimental import pallas as pl
from jax.experimental.pallas import tpu as pltpu

F32 = jnp.float32
BF16 = jnp.bfloat16

LANES = 128
SUBLANES = 8
VMEM_LIMIT_BYTES = 52 * 1024 * 1024

HEAD_DIM = 128
BLOCK = 128
ROPE_THETA = 10000.0
RMS_EPS = 1e-6
NEG_INF = -1e30

LRU_BLOCKS = 8
LRU_CONV = 4
LRU_C = 8.0
DIL_CONFIGS = ((128, 1), (512, 4), (2048, 16))
DIL_HEADS = 4
RWKV_HEAD = 64
RWKV_W_LORA = 64
RWKV_A_LORA = 64
RWKV_G_LORA = 160
RWKV_V_LORA = 32
RWKV_GN_EPS = 64e-5
RWKV_CHUNK = 64
FFN_CONV = 3


def _cparams(*sem):
    return pltpu.CompilerParams(dimension_semantics=sem, vmem_limit_bytes=VMEM_LIMIT_BYTES)


def _dot(a, b):
    return jnp.dot(a, b, preferred_element_type=F32)


def _dot_nt(a, b):
    return lax.dot_general(a, b, (((1,), (1,)), ((), ())), preferred_element_type=F32)


def _dot_tn(a, b):
    return lax.dot_general(a, b, (((0,), (0,)), ((), ())), preferred_element_type=F32)


def _split_bf16(x):
    hi = x.astype(BF16)
    lo = (x - hi.astype(F32)).astype(BF16)
    return hi, lo


def _dot_exact_rhs(x, m_bf16):
    hi, lo = _split_bf16(x)
    return _dot(hi, m_bf16) + _dot(lo, m_bf16)


def _rms(x, g):
    return x * lax.rsqrt(jnp.mean(x * x, axis=-1, keepdims=True) + RMS_EPS) * g


def _gelu(x):
    return jax.nn.gelu(x, approximate=True)


def _softplus(x):
    return jnp.maximum(x, 0.0) + jnp.log1p(jnp.exp(-jnp.abs(x)))


def _rmsnorm_kernel(x_ref, g_ref, o_ref):
    o_ref[...] = _rms(x_ref[...], g_ref[...]).astype(o_ref.dtype)


def rmsnorm_bf16(x, g, tm=512):
    t, d = x.shape
    return pl.pallas_call(
        _rmsnorm_kernel,
        out_shape=jax.ShapeDtypeStruct((t, d), BF16),
        grid=(t // tm,),
        in_specs=[pl.BlockSpec((tm, d), lambda i: (i, 0)), pl.BlockSpec((1, d), lambda i: (0, 0))],
        out_specs=pl.BlockSpec((tm, d), lambda i: (i, 0)),
        compiler_params=_cparams("parallel"),
        name="rmsnorm",
    )(x, g.reshape(1, d))


def _mm_kernel(a_ref, w_ref, o_ref, *, act):
    acc = _dot(a_ref[...], w_ref[...])
    if act == "sigmoid":
        acc = jax.nn.sigmoid(acc)
    o_ref[...] = acc.astype(o_ref.dtype)


def _mm_rope_kernel(a_ref, w_ref, cos_ref, sin_ref, o_ref, *, n_rope_blocks, tn):
    j = pl.program_id(1)
    acc = _dot(a_ref[...], w_ref[...])

    @pl.when(j < n_rope_blocks)
    def _():
        cos = cos_ref[...]
        sin = sin_ref[...]
        for c in range(tn // HEAD_DIM):
            seg = acc[:, c * HEAD_DIM:(c + 1) * HEAD_DIM]
            rot = pltpu.roll(seg, HEAD_DIM // 2, axis=1)
            o_ref[:, c * HEAD_DIM:(c + 1) * HEAD_DIM] = (seg * cos + rot * sin).astype(o_ref.dtype)

    @pl.when(j >= n_rope_blocks)
    def _():
        o_ref[...] = acc.astype(o_ref.dtype)


def matmul(a, w, out_dtype, act=None, tm=512, tn=512):
    m, k = a.shape
    n = w.shape[1]
    return pl.pallas_call(
        functools.partial(_mm_kernel, act=act),
        out_shape=jax.ShapeDtypeStruct((m, n), out_dtype),
        grid=(m // tm, n // tn),
        in_specs=[pl.BlockSpec((tm, k), lambda i, j: (i, 0)), pl.BlockSpec((k, tn), lambda i, j: (0, j))],
        out_specs=pl.BlockSpec((tm, tn), lambda i, j: (i, j)),
        compiler_params=_cparams("parallel", "arbitrary"),
        name="matmul_" + (act or "plain"),
    )(a, w)


def matmul_rope(a, w, cos, sin, seq, n_rope_cols, out_dtype, tm=512, tn=512):
    m, k = a.shape
    n = w.shape[1]
    sblocks = seq // tm
    return pl.pallas_call(
        functools.partial(_mm_rope_kernel, n_rope_blocks=n_rope_cols // tn, tn=tn),
        out_shape=jax.ShapeDtypeStruct((m, n), out_dtype),
        grid=(m // tm, n // tn),
        in_specs=[
            pl.BlockSpec((tm, k), lambda i, j: (i, 0)),
            pl.BlockSpec((k, tn), lambda i, j: (0, j)),
            pl.BlockSpec((tm, HEAD_DIM), lambda i, j: (i % sblocks, 0)),
            pl.BlockSpec((tm, HEAD_DIM), lambda i, j: (i % sblocks, 0)),
        ],
        out_specs=pl.BlockSpec((tm, tn), lambda i, j: (i, j)),
        compiler_params=_cparams("parallel", "arbitrary"),
        name="matmul_rope",
    )(a, w, cos, sin)


def _mm_norm_res_kernel(a_ref, w_ref, x_ref, gpost_ref, gnext_ref, xo_ref, ho_ref, acc_ref, *, nk):
    kk = pl.program_id(1)

    @pl.when(kk == 0)
    def _():
        acc_ref[...] = jnp.zeros_like(acc_ref)

    acc_ref[...] += _dot(a_ref[...], w_ref[...])

    @pl.when(kk == nk - 1)
    def _():
        xn = x_ref[...] + _rms(acc_ref[...], gpost_ref[...])
        xo_ref[...] = xn
        ho_ref[...] = _rms(xn, gnext_ref[...]).astype(ho_ref.dtype)


def matmul_norm_res(a, w, x, g_post, g_next, tm=256, tk=512):
    m, k = a.shape
    d = w.shape[1]
    nk = k // tk
    return pl.pallas_call(
        functools.partial(_mm_norm_res_kernel, nk=nk),
        out_shape=(jax.ShapeDtypeStruct((m, d), F32), jax.ShapeDtypeStruct((m, d), BF16)),
        grid=(m // tm, nk),
        in_specs=[
            pl.BlockSpec((tm, tk), lambda i, kk: (i, kk)),
            pl.BlockSpec((tk, d), lambda i, kk: (kk, 0)),
            pl.BlockSpec((tm, d), lambda i, kk: (i, 0)),
            pl.BlockSpec((1, d), lambda i, kk: (0, 0)),
            pl.BlockSpec((1, d), lambda i, kk: (0, 0)),
        ],
        out_specs=(pl.BlockSpec((tm, d), lambda i, kk: (i, 0)), pl.BlockSpec((tm, d), lambda i, kk: (i, 0))),
        scratch_shapes=[pltpu.VMEM((tm, d), F32)],
        compiler_params=_cparams("parallel", "arbitrary"),
        name="matmul_norm_res",
    )(a, w, x, g_post.reshape(1, d), g_next.reshape(1, d))


def _ple_kernel(p_ref, h_ref, wp_ref, wg_ref, x_ref, gpost_ref, gnext_ref, xo_ref, ho_ref):
    val = _dot(p_ref[...], wp_ref[...]) * jax.nn.sigmoid(_dot(h_ref[...], wg_ref[...]))
    xn = x_ref[...] + _rms(val, gpost_ref[...])
    xo_ref[...] = xn
    ho_ref[...] = _rms(xn, gnext_ref[...]).astype(ho_ref.dtype)


def ple_norm_res(p, h, w_ple, w_gate, x, g_post, g_next, tm=256):
    m, d = x.shape
    pd = p.shape[1]
    row = lambda i: (i, 0)
    fix = lambda i: (0, 0)
    return pl.pallas_call(
        _ple_kernel,
        out_shape=(jax.ShapeDtypeStruct((m, d), F32), jax.ShapeDtypeStruct((m, d), BF16)),
        grid=(m // tm,),
        in_specs=[
            pl.BlockSpec((tm, pd), row), pl.BlockSpec((tm, d), row),
            pl.BlockSpec((pd, d), fix), pl.BlockSpec((d, d), fix),
            pl.BlockSpec((tm, d), row), pl.BlockSpec((1, d), fix), pl.BlockSpec((1, d), fix),
        ],
        out_specs=(pl.BlockSpec((tm, d), row), pl.BlockSpec((tm, d), row)),
        compiler_params=_cparams("parallel"),
        name="ple_norm_res",
    )(p, h, w_ple, w_gate, x, g_post.reshape(1, d), g_next.reshape(1, d))


def _merge_kernel(ya_ref, yb_ref, yc_ref, yd_ref, ga_ref, gb_ref, gc_ref, gd_ref,
                  wa_ref, wb_ref, wc_ref, wd_ref, o_ref):
    acc = ga_ref[...].astype(F32) * _dot(ya_ref[...], wa_ref[...])
    acc += gb_ref[...].astype(F32) * _dot(yb_ref[...], wb_ref[...])
    acc += gc_ref[...].astype(F32) * _dot(yc_ref[...], wc_ref[...])
    acc += gd_ref[...].astype(F32) * _dot(yd_ref[...], wd_ref[...])
    o_ref[...] = acc.astype(o_ref.dtype)


def merge_branches(ys, gates, ws, tm=512, tn=512):
    m = ys[0].shape[0]
    d = ws[0].shape[1]
    nb = d // tn
    in_specs = [pl.BlockSpec((tm, y.shape[1]), lambda i, j: (i, 0)) for y in ys]
    in_specs += [pl.BlockSpec((tm, tn), functools.partial(lambda i, j, b: (i, b * nb + j), b=b)) for b in range(4)]
    in_specs += [pl.BlockSpec((w.shape[0], tn), lambda i, j: (0, j)) for w in ws]
    return pl.pallas_call(
        _merge_kernel,
        out_shape=jax.ShapeDtypeStruct((m, d), BF16),
        grid=(m // tm, nb),
        in_specs=in_specs,
        out_specs=pl.BlockSpec((tm, tn), lambda i, j: (i, j)),
        compiler_params=_cparams("parallel", "arbitrary"),
        name="merge_branches",
    )(*ys, gates, gates, gates, gates, *ws)


def _ffn_up_kernel(h_ref, wg_ref, wu_ref, cwg_ref, cwu_ref, cbg_ref, cbu_ref, o_ref, bufg_ref, bufu_ref,
                   *, tm, seq_blocks):
    i = pl.program_id(1)

    @pl.when(i % seq_blocks == 0)
    def _():
        bufg_ref[0:SUBLANES, :] = jnp.zeros((SUBLANES, bufg_ref.shape[1]), F32)
        bufu_ref[0:SUBLANES, :] = jnp.zeros((SUBLANES, bufu_ref.shape[1]), F32)

    h = h_ref[...]

    def conv(w_ref, cw_ref, cb_ref, buf_ref):
        buf_ref[SUBLANES:SUBLANES + tm, :] = _dot(h, w_ref[...])
        cw = cw_ref[...]
        out = cb_ref[...] + cw[2:3, :] * buf_ref[SUBLANES:SUBLANES + tm, :]
        out += cw[1:2, :] * buf_ref[SUBLANES - 1:SUBLANES - 1 + tm, :]
        out += cw[0:1, :] * buf_ref[SUBLANES - 2:SUBLANES - 2 + tm, :]
        buf_ref[0:SUBLANES, :] = buf_ref[tm:tm + SUBLANES, :]
        return out

    g = conv(wg_ref, cwg_ref, cbg_ref, bufg_ref)
    u = conv(wu_ref, cwu_ref, cbu_ref, bufu_ref)
    o_ref[...] = (_gelu(g) * u).astype(o_ref.dtype)


def ffn_up(h, w_up, conv_w, conv_b, seq, tm=512, tn=512):
    m, d = h.shape
    dff = w_up.shape[1] // 2
    nb = dff // tn
    cb = conv_b.reshape(1, 2 * dff)
    return pl.pallas_call(
        functools.partial(_ffn_up_kernel, tm=tm, seq_blocks=seq // tm),
        out_shape=jax.ShapeDtypeStruct((m, dff), BF16),
        grid=(nb, m // tm),
        in_specs=[
            pl.BlockSpec((tm, d), lambda j, i: (i, 0)),
            pl.BlockSpec((d, tn), lambda j, i: (0, j)),
            pl.BlockSpec((d, tn), lambda j, i: (0, j + nb)),
            pl.BlockSpec((FFN_CONV, tn), lambda j, i: (0, j)),
            pl.BlockSpec((FFN_CONV, tn), lambda j, i: (0, j + nb)),
            pl.BlockSpec((1, tn), lambda j, i: (0, j)),
            pl.BlockSpec((1, tn), lambda j, i: (0, j + nb)),
        ],
        out_specs=pl.BlockSpec((tm, tn), lambda j, i: (i, j)),
        scratch_shapes=[pltpu.VMEM((tm + SUBLANES, tn), F32), pltpu.VMEM((tm + SUBLANES, tn), F32)],
        compiler_params=_cparams("parallel", "arbitrary"),
        name="ffn_up_conv_glu",
    )(h, w_up, w_up, conv_w, conv_w, cb, cb)


def _lru_kernel(x_ref, gate_ref, cw_ref, cb_ref, wr_ref, br_ref, wi_ref, bi_ref, lam_ref, o_ref,
                xbuf_ref, a0_ref, h0_ref, a1_ref, h1_ref, *, seq, pad):
    xbuf_ref[0:SUBLANES, :] = jnp.zeros((SUBLANES, LANES), F32)
    xbuf_ref[SUBLANES:SUBLANES + seq, :] = x_ref[0]
    cw = cw_ref[...]
    u = cb_ref[...] + cw[3:4, :] * xbuf_ref[SUBLANES:SUBLANES + seq, :]
    for k in range(LRU_CONV - 1):
        off = SUBLANES - (LRU_CONV - 1) + k
        u += cw[k:k + 1, :] * xbuf_ref[off:off + seq, :]
    ub = u.astype(BF16)
    r = jax.nn.sigmoid(_dot(ub, wr_ref[0]) + br_ref[...])
    ig = jax.nn.sigmoid(_dot(ub, wi_ref[0]) + bi_ref[...])
    log_a = -LRU_C * r * _softplus(-lam_ref[...])
    a = jnp.exp(log_a)
    inp = jnp.sqrt(1.0 - jnp.exp(2.0 * log_a)) * ig * u

    ones = jnp.ones((pad, LANES), F32)
    zeros = jnp.zeros((pad, LANES), F32)
    a0_ref[0:pad, :] = ones
    a1_ref[0:pad, :] = ones
    h0_ref[0:pad, :] = zeros
    h1_ref[0:pad, :] = zeros
    a0_ref[pad:pad + seq, :] = a
    h0_ref[pad:pad + seq, :] = inp
    bufs = ((a0_ref, h0_ref), (a1_ref, h1_ref))
    d = 1
    level = 0
    while d < seq:
        (a_src, h_src), (a_dst, h_dst) = bufs[level % 2], bufs[(level + 1) % 2]
        a_cur = a_src[pad:pad + seq, :]
        h_dst[pad:pad + seq, :] = h_src[pad:pad + seq, :] + a_cur * h_src[pad - d:pad - d + seq, :]
        a_dst[pad:pad + seq, :] = a_cur * a_src[pad - d:pad - d + seq, :]
        d *= 2
        level += 1
    h = bufs[level % 2][1][pad:pad + seq, :]
    o_ref[0] = (h * _gelu(gate_ref[0])).astype(o_ref.dtype)


def rglru(xg, conv_w, conv_b, w_r, b_r, w_i, b_i, lam):
    b, s, w2 = xg.shape
    w = w2 // 2
    nblk = w // LANES
    pad = s
    vec = lambda v: v.reshape(1, w)
    vspec = pl.BlockSpec((1, LANES), lambda bi, c: (0, c))
    return pl.pallas_call(
        functools.partial(_lru_kernel, seq=s, pad=pad),
        out_shape=jax.ShapeDtypeStruct((b, s, w), BF16),
        grid=(b, nblk),
        in_specs=[
            pl.BlockSpec((1, s, LANES), lambda bi, c: (bi, 0, c)),
            pl.BlockSpec((1, s, LANES), lambda bi, c: (bi, 0, c + nblk)),
            pl.BlockSpec((LRU_CONV, LANES), lambda bi, c: (0, c)),
            vspec,
            pl.BlockSpec((1, LANES, LANES), lambda bi, c: (c, 0, 0)),
            vspec,
            pl.BlockSpec((1, LANES, LANES), lambda bi, c: (c, 0, 0)),
            vspec, vspec,
        ],
        out_specs=pl.BlockSpec((1, s, LANES), lambda bi, c: (bi, 0, c)),
        scratch_shapes=[pltpu.VMEM((s + SUBLANES, LANES), F32)] + [pltpu.VMEM((pad + s, LANES), F32)] * 4,
        compiler_params=_cparams("parallel", "parallel"),
        name="rglru",
    )(xg, xg, conv_w, vec(conv_b), w_r, vec(b_r), w_i, vec(b_i), vec(lam))


def _dil_kernel(q_ref, kp_ref, kc_ref, vp_ref, vc_ref, o_ref, lse_ref):
    n = pl.program_id(3)
    q = q_ref[0]
    scale = HEAD_DIM ** -0.5
    s_p = _dot_nt(q, kp_ref[0]) * scale
    s_c = _dot_nt(q, kc_ref[0]) * scale
    qi = lax.broadcasted_iota(jnp.int32, (BLOCK, BLOCK), 0)
    kj = lax.broadcasted_iota(jnp.int32, (BLOCK, BLOCK), 1)
    lim = jnp.where(n > 0, qi, BLOCK)
    s_p = jnp.where(kj >= lim, s_p, NEG_INF)
    s_c = jnp.where(kj <= qi, s_c, NEG_INF)
    m = jnp.maximum(jnp.max(s_p, axis=-1, keepdims=True), jnp.max(s_c, axis=-1, keepdims=True))
    e_p = jnp.exp(s_p - m)
    e_c = jnp.exp(s_c - m)
    den = jnp.sum(e_p, axis=-1, keepdims=True) + jnp.sum(e_c, axis=-1, keepdims=True)
    o = _dot(e_p.astype(BF16), vp_ref[0]) + _dot(e_c.astype(BF16), vc_ref[0])
    o_ref[0] = o / den
    lse_ref[0] = jnp.broadcast_to(m + jnp.log(den), (BLOCK, HEAD_DIM))


def dilated_group(qkv, group, dilation):
    b, s, width = qkv.shape
    ncol = width // HEAD_DIM
    nheads = ncol // 3
    ln = s // dilation
    nb = ln // BLOCK
    view = qkv.reshape(b, ln, dilation * width)
    hq = group * DIL_HEADS

    def spec(which, prev):
        def imap(bi, r, h, n):
            nn = jnp.maximum(n - 1, 0) if prev else n
            return (bi, nn, r * ncol + which * nheads + hq + h)
        return pl.BlockSpec((1, BLOCK, HEAD_DIM), imap)

    out_w = DIL_HEADS * HEAD_DIM
    ospec = pl.BlockSpec((1, BLOCK, HEAD_DIM), lambda bi, r, h, n: (bi, n, r * DIL_HEADS + h))
    o, lse = pl.pallas_call(
        _dil_kernel,
        out_shape=(jax.ShapeDtypeStruct((b, ln, dilation * out_w), F32),) * 2,
        grid=(b, dilation, DIL_HEADS, nb),
        in_specs=[spec(0, False), spec(1, True), spec(1, False), spec(2, True), spec(2, False)],
        out_specs=(ospec, ospec),
        compiler_params=_cparams("parallel", "parallel", "parallel", "arbitrary"),
        name="dilated_attention",
    )(view, view, view, view, view)
    return o.reshape(b, s, out_w), lse.reshape(b, s, out_w)


def _dil_combine_kernel(o0_ref, o1_ref, o2_ref, l0_ref, l1_ref, l2_ref, y_ref):
    l0, l1, l2 = l0_ref[...], l1_ref[...], l2_ref[...]
    m = jnp.maximum(jnp.maximum(l0, l1), l2)
    e0, e1, e2 = jnp.exp(l0 - m), jnp.exp(l1 - m), jnp.exp(l2 - m)
    y = (e0 * o0_ref[...] + e1 * o1_ref[...] + e2 * o2_ref[...]) / (e0 + e1 + e2)
    y_ref[...] = y.astype(y_ref.dtype)


def dilated_combine(os_, lses, tm=512):
    m, w = os_[0].shape
    spec = pl.BlockSpec((tm, w), lambda i: (i, 0))
    return pl.pallas_call(
        _dil_combine_kernel,
        out_shape=jax.ShapeDtypeStruct((m, w), BF16),
        grid=(m // tm,),
        in_specs=[spec] * 6,
        out_specs=spec,
        compiler_params=_cparams("parallel"),
        name="dilated_combine",
    )(*os_, *lses)


def _sb_kernel(q_ref, k_ref, v_ref, o_ref):
    n = pl.program_id(2)
    q = q_ref[...]
    scale = HEAD_DIM ** -0.5
    row = lax.broadcasted_iota(jnp.int32, (BLOCK, BLOCK), 0)
    col = lax.broadcasted_iota(jnp.int32, (BLOCK, BLOCK), 1)
    later = (row > col).astype(BF16)
    strict = col < row

    def block(j, carry, diag):
        acc, run = carry
        start = pl.multiple_of(j * BLOCK, BLOCK)
        kb = k_ref[pl.ds(start, BLOCK), :]
        vb = v_ref[pl.ds(start, BLOCK), :]
        z = _dot_nt(q, kb) * scale
        log_beta = -_softplus(-z)
        log_1m = log_beta - z
        if diag:
            log_1m = jnp.where(strict, log_1m, 0.0)
        suffix = _dot_exact_rhs(log_1m, later)
        att = jnp.exp(log_beta + suffix + run)
        if diag:
            att = jnp.where(strict, att, 0.0)
        acc = acc + _dot(att.astype(BF16), vb)
        run = run + jnp.sum(log_1m, axis=-1, keepdims=True)
        return acc, run

    carry = (jnp.zeros((BLOCK, HEAD_DIM), F32), jnp.zeros((BLOCK, 1), F32))
    carry = block(n, carry, True)
    acc, _ = lax.fori_loop(0, n, lambda t, c: block(n - 1 - t, c, False), carry)
    o_ref[...] = acc.astype(o_ref.dtype)


def stick_breaking(qkv, batch, seq):
    t, width = qkv.shape
    nh = width // (3 * HEAD_DIM)
    nq = seq // BLOCK
    return pl.pallas_call(
        _sb_kernel,
        out_shape=jax.ShapeDtypeStruct((t, nh * HEAD_DIM), BF16),
        grid=(batch, nh, nq),
        in_specs=[
            pl.BlockSpec((BLOCK, HEAD_DIM), lambda b, h, n: (b * nq + n, h)),
            pl.BlockSpec((seq, HEAD_DIM), lambda b, h, n: (b, nh + h)),
            pl.BlockSpec((seq, HEAD_DIM), lambda b, h, n: (b, 2 * nh + h)),
        ],
        out_specs=pl.BlockSpec((BLOCK, HEAD_DIM), lambda b, h, n: (b * nq + n, h)),
        compiler_params=_cparams("parallel", "parallel", "arbitrary"),
        name="stick_breaking",
    )(qkv, qkv, qkv)


def _head_sum(x, bd):
    cols = []
    for c in range(x.shape[1] // LANES):
        cols.append(_dot_exact_rhs(x[:, c * LANES:(c + 1) * LANES], bd))
    return jnp.concatenate(cols, axis=1)


def _rwkv_prep_kernel(*refs, tm, width, seq_blocks, has_vres):
    if has_vres:
        (seg_ref, prev_ref, mu_ref, w0_ref, wup_ref, a0_ref, aup_ref, gup_ref, kk_ref, ka_ref, rk_ref,
         vf_ref, v0_ref, vdn_ref, vup_ref,
         r_o, k_o, v_o, lw_o, kk_o, b_o, g_o, bon_o, buf_ref) = refs
    else:
        (seg_ref, prev_ref, mu_ref, w0_ref, wup_ref, a0_ref, aup_ref, gup_ref, kk_ref, ka_ref, rk_ref,
         r_o, k_o, v_o, lw_o, kk_o, b_o, g_o, bon_o, buf_ref) = refs
    i = pl.program_id(0)
    seg = seg_ref[...]
    buf_ref[0:SUBLANES, :] = prev_ref[...]

    @pl.when(i % seq_blocks == 0)
    def _():
        buf_ref[0:SUBLANES, :] = jnp.zeros((SUBLANES, buf_ref.shape[1]), F32)

    buf_ref[SUBLANES:SUBLANES + tm, :] = seg
    shifted = buf_ref[SUBLANES - 1:SUBLANES - 1 + tm, :]
    xs = seg + (shifted - seg) * mu_ref[...]
    w = width
    r = xs[:, 0:w]
    k = xs[:, w:2 * w]
    v = xs[:, 2 * w:3 * w]
    low = xs[:, 3 * w:3 * w + LANES]
    g_low = xs[:, 3 * w + LANES:3 * w + 3 * LANES]
    wpre = w0_ref[...] + _dot(jnp.tanh(low).astype(BF16), wup_ref[...])
    wlog = -_softplus(-wpre) - 0.5
    lw_o[...] = -jnp.exp(wlog)
    a = jax.nn.sigmoid(a0_ref[...] + _dot(low.astype(BF16), aup_ref[...]))
    g_o[...] = _dot(jax.nn.sigmoid(g_low).astype(BF16), gup_ref[...])
    if has_vres:
        mix = jax.nn.sigmoid(v0_ref[...] + _dot(_dot(v.astype(BF16), vdn_ref[...]).astype(BF16), vup_ref[...]))
        v = v + (vf_ref[...] - v) * mix
    row = lax.broadcasted_iota(jnp.int32, (LANES, LANES), 0) // RWKV_HEAD
    col = lax.broadcasted_iota(jnp.int32, (LANES, LANES), 1) // RWKV_HEAD
    bd = (row == col).astype(BF16)
    kk = k * kk_ref[...]
    norm = jnp.sqrt(_head_sum(kk * kk, bd))
    kk = kk / jnp.maximum(norm, 1e-12)
    k2 = k * (1.0 + (a - 1.0) * ka_ref[...])
    bonus = _head_sum(r * k2 * rk_ref[...], bd) * v
    r_o[...] = r
    k_o[...] = k2
    v_o[...] = v
    kk_o[...] = kk
    b_o[...] = kk * a
    bon_o[...] = bonus


def rwkv_prep(seg, seq, mu, w0, w_up, a0, a_up, g_up, k_k, k_a, r_k, v_first, v_res, tm=256):
    t, wpad = seg.shape
    w = w0.shape[0]
    has_vres = v_res is not None
    row = lambda i: (i, 0)
    fix = lambda i: (0, 0)
    vec = lambda x: x.reshape(1, -1)
    step = tm // SUBLANES
    in_specs = [
        pl.BlockSpec((tm, wpad), row),
        pl.BlockSpec((SUBLANES, wpad), lambda i: (jnp.maximum(i * step - 1, 0), 0)),
        pl.BlockSpec((1, wpad), fix), pl.BlockSpec((1, w), fix), pl.BlockSpec((LANES, w), fix),
        pl.BlockSpec((1, w), fix), pl.BlockSpec((LANES, w), fix), pl.BlockSpec((2 * LANES, w), fix),
        pl.BlockSpec((1, w), fix), pl.BlockSpec((1, w), fix), pl.BlockSpec((1, w), fix),
    ]
    args = [seg, seg, vec(mu), vec(w0), w_up, vec(a0), a_up, g_up, vec(k_k), vec(k_a), vec(r_k)]
    if has_vres:
        v0, v_down, v_up = v_res
        in_specs += [pl.BlockSpec((tm, w), row), pl.BlockSpec((1, w), fix),
                     pl.BlockSpec((w, LANES), fix), pl.BlockSpec((LANES, w), fix)]
        args += [v_first, vec(v0), v_down, v_up]
    out = jax.ShapeDtypeStruct((t, w), F32)
    return pl.pallas_call(
        functools.partial(_rwkv_prep_kernel, tm=tm, width=w, seq_blocks=seq // tm, has_vres=has_vres),
        out_shape=(out,) * 8,
        grid=(t // tm,),
        in_specs=in_specs,
        out_specs=(pl.BlockSpec((tm, w), row),) * 8,
        scratch_shapes=[pltpu.VMEM((tm + SUBLANES, wpad), F32)],
        compiler_params=_cparams("parallel"),
        name="rwkv_prep",
    )(*args)


def _rwkv_chunk_kernel(r_ref, k_ref, v_ref, lw_ref, kk_ref, b_ref, g_ref, bon_ref, gnw_ref, gnb_ref, o_ref,
                       state_ref):
    c = pl.program_id(2)
    n = RWKV_CHUNK

    @pl.when(c == 0)
    def _():
        state_ref[...] = jnp.zeros_like(state_ref)

    ti = lax.broadcasted_iota(jnp.int32, (n, n), 0)
    si = lax.broadcasted_iota(jnp.int32, (n, n), 1)
    incl = si <= ti
    strict = si < ti
    tri = incl.astype(BF16)
    eye = (si == ti).astype(F32)
    outs = []
    for hh in range(LANES // RWKV_HEAD):
        sl = slice(hh * RWKV_HEAD, (hh + 1) * RWKV_HEAD)
        r, k, v = r_ref[:, sl], k_ref[:, sl], v_ref[:, sl]
        lw, kk, b = lw_ref[:, sl], kk_ref[:, sl], b_ref[:, sl]
        gsum = _dot_exact_rhs_left(tri, lw)
        p = jnp.exp(gsum)
        p_inv = jnp.exp(-gsum)
        p_prev = jnp.exp(gsum - lw)
        p_end = jnp.exp(gsum[n - 1:n, :])
        rt = (r * p).astype(BF16)
        kt = k * p_inv
        bt = b * p_inv
        kap = (kk * p_prev).astype(BF16)
        ktb = kt.astype(BF16)
        btb = bt.astype(BF16)
        vb = v.astype(BF16)
        a_ab = jnp.where(strict, _dot_nt(kap, btb), 0.0)
        a_ak = jnp.where(strict, _dot_nt(kap, ktb), 0.0)
        a_rb = jnp.where(incl, _dot_nt(rt, btb), 0.0)
        a_rk = jnp.where(incl, _dot_nt(rt, ktb), 0.0)
        x = eye - a_ab
        q = _dot(a_ab.astype(BF16), a_ab.astype(BF16))
        steps = 1
        while True:
            x = x + _dot(x.astype(BF16), q.astype(BF16))
            steps *= 2
            if steps * 2 >= n:
                break
            q = _dot(q.astype(BF16), q.astype(BF16))
        s0 = state_ref[hh]
        s0b = s0.astype(BF16)
        rhs = _dot_nt(kap, s0b) + _dot(a_ak.astype(BF16), vb)
        u = _dot(x.astype(BF16), rhs.astype(BF16))
        ub = u.astype(BF16)
        y = _dot_nt(rt, s0b) - _dot(a_rb.astype(BF16), ub) + _dot(a_rk.astype(BF16), vb)
        state_ref[hh] = (s0 * p_end + _dot_tn(vb, (kt * p_end).astype(BF16))
                         - _dot_tn(ub, (bt * p_end).astype(BF16)))
        mean = jnp.mean(y, axis=-1, keepdims=True)
        var = jnp.mean(jnp.square(y - mean), axis=-1, keepdims=True)
        yn = (y - mean) * lax.rsqrt(var + RWKV_GN_EPS) * gnw_ref[:, sl] + gnb_ref[:, sl]
        outs.append((yn + bon_ref[:, sl]) * g_ref[:, sl])
    o_ref[...] = jnp.concatenate(outs, axis=1).astype(o_ref.dtype)


def _dot_exact_rhs_left(m_bf16, x):
    hi, lo = _split_bf16(x)
    return _dot(m_bf16, hi) + _dot(m_bf16, lo)


def rwkv_chunks(r, k, v, lw, kk, b, g, bon, gn_w, gn_b, batch, seq):
    t, w = r.shape
    nc = seq // RWKV_CHUNK
    spec = pl.BlockSpec((RWKV_CHUNK, LANES), lambda bi, hp, c: (bi * nc + c, hp))
    vspec = pl.BlockSpec((1, LANES), lambda bi, hp, c: (0, hp))
    return pl.pallas_call(
        _rwkv_chunk_kernel,
        out_shape=jax.ShapeDtypeStruct((t, w), BF16),
        grid=(batch, w // LANES, nc),
        in_specs=[spec] * 8 + [vspec, vspec],
        out_specs=spec,
        scratch_shapes=[pltpu.VMEM((LANES // RWKV_HEAD, RWKV_HEAD, RWKV_HEAD), F32)],
        compiler_params=_cparams("parallel", "parallel", "arbitrary"),
        name="rwkv_chunks",
    )(r, k, v, lw, kk, b, g, bon, gn_w.reshape(1, w), gn_b.reshape(1, w))


def _pad_rows(w, rows):
    return jnp.pad(w, ((0, rows - w.shape[0]), (0, 0)))


def _pad_cols(w, cols):
    return jnp.pad(w, ((0, 0), (0, cols - w.shape[1])))


def _rope_tables(seq):
    half = HEAD_DIM // 2
    inv_freq = ROPE_THETA ** (-jnp.arange(half, dtype=F32) / half)
    ang = jnp.arange(seq, dtype=F32)[:, None] * inv_freq[None, :]
    cos = jnp.cos(ang)
    sin = jnp.sin(ang)
    return jnp.concatenate([cos, cos], axis=1), jnp.concatenate([-sin, sin], axis=1)


def kernel(x, p, norm_mix_pre, norm_mix_post, norm_ffn_pre, norm_ffn_post, norm_ple_pre, norm_ple_post, w_in, w_merge_gate, lru_conv_w, lru_conv_b, lru_w_r, lru_b_r, lru_w_i, lru_b_i, lru_lambda, rwkv_mu, rwkv_w0, rwkv_w_up, rwkv_a0, rwkv_a_up, rwkv_g_up, rwkv_k_k, rwkv_k_a, rwkv_r_k, rwkv_gn_w, rwkv_gn_b, rwkv_v0, rwkv_v_down, rwkv_v_up, w_branch_a, w_branch_b, w_branch_c, w_branch_d, w_out, w_ffn_up, ffn_conv_w, ffn_conv_b, w_ffn_down, w_ple, w_ple_gate):
    batch, seq, d = x.shape
    depth = w_in.shape[0]
    t = batch * seq
    lru_w = lru_conv_w.shape[2]
    rw = rwkv_w0.shape[1]
    dil_w = 3 * len(DIL_CONFIGS) * DIL_HEADS * HEAD_DIM
    sb_w = 3 * (d // 2)
    off_b = 2 * lru_w
    off_c = off_b + dil_w
    off_d = off_c + sb_w
    rwkv_in = w_in.shape[2] - off_d
    rwkv_pad = 3 * rw + 4 * LANES
    cos, sin = _rope_tables(seq)

    xf = x.reshape(t, d)
    h = rmsnorm_bf16(xf, norm_mix_pre[0])
    v_first = None
    for i in range(depth):
        wi = w_in[i]
        w_a = wi[:, :off_b].astype(BF16)
        w_b = wi[:, off_b:off_c].astype(BF16)
        w_c = wi[:, off_c:off_d].astype(BF16)
        w_d = _pad_cols(wi[:, off_d:], rwkv_pad).astype(BF16)
        seg_a = matmul(h, w_a, F32)
        seg_b = matmul_rope(h, w_b, cos, sin, seq, 2 * dil_w // 3, BF16)
        seg_c = matmul(h, w_c, BF16)
        seg_d = matmul(h, w_d, F32)
        gates = matmul(h, w_merge_gate[i].astype(BF16), BF16, act="sigmoid")
        y_a = rglru(seg_a.reshape(batch, seq, off_b), lru_conv_w[i], lru_conv_b[i], lru_w_r[i].astype(BF16),
                    lru_b_r[i], lru_w_i[i].astype(BF16), lru_b_i[i], lru_lambda[i]).reshape(t, lru_w)
        qkv_b = seg_b.reshape(batch, seq, dil_w)
        os_, lses = [], []
        for g, (_, dilation) in enumerate(DIL_CONFIGS):
            o_g, lse_g = dilated_group(qkv_b, g, dilation)
            os_.append(o_g.reshape(t, -1))
            lses.append(lse_g.reshape(t, -1))
        y_b = dilated_combine(os_, lses)
        y_c = stick_breaking(seg_c, batch, seq)
        mu = jnp.pad(rwkv_mu[i], (0, rwkv_pad - rwkv_in))
        w_up = _pad_rows(rwkv_w_up[i], LANES).astype(BF16)
        a_up = jnp.pad(rwkv_a_up[i], ((RWKV_W_LORA, LANES - RWKV_W_LORA - RWKV_A_LORA), (0, 0))).astype(BF16)
        g_up = _pad_rows(rwkv_g_up[i], 2 * LANES).astype(BF16)
        v_res = None
        if i > 0:
            v_res = (rwkv_v0[i - 1], _pad_cols(rwkv_v_down[i - 1], LANES).astype(BF16),
                     _pad_rows(rwkv_v_up[i - 1], LANES).astype(BF16))
        r_, k_, v_, lw_, kk_, b_, g_, bon_ = rwkv_prep(
            seg_d, seq, mu, rwkv_w0[i], w_up, rwkv_a0[i], a_up, g_up, rwkv_k_k[i], rwkv_k_a[i],
            rwkv_r_k[i].reshape(-1), v_first, v_res)
        if i == 0:
            v_first = v_
        y_d = rwkv_chunks(r_, k_, v_, lw_, kk_, b_, g_, bon_, rwkv_gn_w[i], rwkv_gn_b[i], batch, seq)
        merged = merge_branches(
            (y_a, y_b, y_c, y_d), gates,
            (w_branch_a[i].astype(BF16), w_branch_b[i].astype(BF16), w_branch_c[i].astype(BF16),
             w_branch_d[i].astype(BF16)))
        xf, h = matmul_norm_res(merged, w_out[i].astype(BF16), xf, norm_mix_post[i], norm_ffn_pre[i], tk=d)
        act = ffn_up(h, w_ffn_up[i].astype(BF16), ffn_conv_w[i], ffn_conv_b[i], seq)
        xf, h = matmul_norm_res(act, w_ffn_down[i].astype(BF16), xf, norm_ffn_post[i], norm_ple_pre[i])
        g_next = norm_mix_pre[i + 1] if i + 1 < depth else norm_mix_pre[i]
        xf, h = ple_norm_res(p[i].reshape(t, -1).astype(BF16), h, w_ple[i].astype(BF16),
                             w_ple_gate[i].astype(BF16), xf, norm_ple_post[i], g_next)
    return xf.reshape(batch, seq, d)
```

```python
import functools

import jax
import jax.numpy as jnp
from jax import lax
from jax.experimental import pallas as pl
from jax.experimental.pallas import tpu as pltpu

F32 = jnp.float32
BF16 = jnp.bfloat16

LANES = 128
SUBLANES = 8
VMEM_LIMIT_BYTES = 52 * 1024 * 1024

HEAD_DIM = 128
BLOCK = 128
ROPE_THETA = 10000.0
RMS_EPS = 1e-6
NEG_INF = -1e30

LRU_BLOCKS = 8
LRU_CONV = 4
LRU_C = 8.0
DIL_CONFIGS = ((128, 1), (512, 4), (2048, 16))
DIL_HEADS = 4
RWKV_HEAD = 64
RWKV_W_LORA = 64
RWKV_A_LORA = 64
RWKV_G_LORA = 160
RWKV_V_LORA = 32
RWKV_GN_EPS = 64e-5
RWKV_CHUNK = 64
FFN_CONV = 3


def _cparams(*sem):
    return pltpu.CompilerParams(dimension_semantics=sem, vmem_limit_bytes=VMEM_LIMIT_BYTES)


def _dot(a, b):
    return jnp.dot(a, b, preferred_element_type=F32)


def _dot_nt(a, b):
    return lax.dot_general(a, b, (((1,), (1,)), ((), ())), preferred_element_type=F32)


def _dot_tn(a, b):
    return lax.dot_general(a, b, (((0,), (0,)), ((), ())), preferred_element_type=F32)


def _split_bf16(x):
    hi = x.astype(BF16)
    lo = (x - hi.astype(F32)).astype(BF16)
    return hi, lo


def _dot_exact_rhs(x, m_bf16):
    hi, lo = _split_bf16(x)
    return _dot(hi, m_bf16) + _dot(lo, m_bf16)


def _rms(x, g):
    return x * lax.rsqrt(jnp.mean(x * x, axis=-1, keepdims=True) + RMS_EPS) * g


def _gelu(x):
    return jax.nn.gelu(x, approximate=True)


def _softplus(x):
    return jnp.maximum(x, 0.0) + jnp.log1p(jnp.exp(-jnp.abs(x)))


def _rmsnorm_kernel(x_ref, g_ref, o_ref):
    o_ref[...] = _rms(x_ref[...], g_ref[...]).astype(o_ref.dtype)


def rmsnorm_bf16(x, g, tm=512):
    t, d = x.shape
    return pl.pallas_call(
        _rmsnorm_kernel,
        out_shape=jax.ShapeDtypeStruct((t, d), BF16),
        grid=(t // tm,),
        in_specs=[pl.BlockSpec((tm, d), lambda i: (i, 0)), pl.BlockSpec((1, d), lambda i: (0, 0))],
        out_specs=pl.BlockSpec((tm, d), lambda i: (i, 0)),
        compiler_params=_cparams("parallel"),
        name="rmsnorm",
    )(x, g.reshape(1, d))


def _mm_kernel(a_ref, w_ref, o_ref, *, act):
    acc = _dot(a_ref[...], w_ref[...])
    if act == "sigmoid":
        acc = jax.nn.sigmoid(acc)
    o_ref[...] = acc.astype(o_ref.dtype)


def _mm_rope_kernel(a_ref, w_ref, cos_ref, sin_ref, o_ref, *, n_rope_blocks, tn):
    j = pl.program_id(1)
    acc = _dot(a_ref[...], w_ref[...])

    @pl.when(j < n_rope_blocks)
    def _():
        cos = cos_ref[...]
        sin = sin_ref[...]
        for c in range(tn // HEAD_DIM):
            seg = acc[:, c * HEAD_DIM:(c + 1) * HEAD_DIM]
            rot = pltpu.roll(seg, HEAD_DIM // 2, axis=1)
            o_ref[:, c * HEAD_DIM:(c + 1) * HEAD_DIM] = (seg * cos + rot * sin).astype(o_ref.dtype)

    @pl.when(j >= n_rope_blocks)
    def _():
        o_ref[...] = acc.astype(o_ref.dtype)


def matmul(a, w, out_dtype, act=None, tm=512, tn=512):
    m, k = a.shape
    n = w.shape[1]
    return pl.pallas_call(
        functools.partial(_mm_kernel, act=act),
        out_shape=jax.ShapeDtypeStruct((m, n), out_dtype),
        grid=(m // tm, n // tn),
        in_specs=[pl.BlockSpec((tm, k), lambda i, j: (i, 0)), pl.BlockSpec((k, tn), lambda i, j: (0, j))],
        out_specs=pl.BlockSpec((tm, tn), lambda i, j: (i, j)),
        compiler_params=_cparams("parallel", "arbitrary"),
        name="matmul_" + (act or "plain"),
    )(a, w)


def matmul_rope(a, w, cos, sin, seq, n_rope_cols, out_dtype, tm=512, tn=512):
    m, k = a.shape
    n = w.shape[1]
    sblocks = seq // tm
    return pl.pallas_call(
        functools.partial(_mm_rope_kernel, n_rope_blocks=n_rope_cols // tn, tn=tn),
        out_shape=jax.ShapeDtypeStruct((m, n), out_dtype),
        grid=(m // tm, n // tn),
        in_specs=[
            pl.BlockSpec((tm, k), lambda i, j: (i, 0)),
            pl.BlockSpec((k, tn), lambda i, j: (0, j)),
            pl.BlockSpec((tm, HEAD_DIM), lambda i, j: (i % sblocks, 0)),
            pl.BlockSpec((tm, HEAD_DIM), lambda i, j: (i % sblocks, 0)),
        ],
        out_specs=pl.BlockSpec((tm, tn), lambda i, j: (i, j)),
        compiler_params=_cparams("parallel", "arbitrary"),
        name="matmul_rope",
    )(a, w, cos, sin)


def _mm_norm_res_kernel(a_ref, w_ref, x_ref, gpost_ref, gnext_ref, xo_ref, ho_ref, acc_ref, *, nk):
    kk = pl.program_id(1)

    @pl.when(kk == 0)
    def _():
        acc_ref[...] = jnp.zeros_like(acc_ref)

    acc_ref[...] += _dot(a_ref[...], w_ref[...])

    @pl.when(kk == nk - 1)
    def _():
        xn = x_ref[...] + _rms(acc_ref[...], gpost_ref[...])
        xo_ref[...] = xn
        ho_ref[...] = _rms(xn, gnext_ref[...]).astype(ho_ref.dtype)


def matmul_norm_res(a, w, x, g_post, g_next, tm=256, tk=512):
    m, k = a.shape
    d = w.shape[1]
    nk = k // tk
    return pl.pallas_call(
        functools.partial(_mm_norm_res_kernel, nk=nk),
        out_shape=(jax.ShapeDtypeStruct((m, d), F32), jax.ShapeDtypeStruct((m, d), BF16)),
        grid=(m // tm, nk),
        in_specs=[
            pl.BlockSpec((tm, tk), lambda i, kk: (i, kk)),
            pl.BlockSpec((tk, d), lambda i, kk: (kk, 0)),
            pl.BlockSpec((tm, d), lambda i, kk: (i, 0)),
            pl.BlockSpec((1, d), lambda i, kk: (0, 0)),
            pl.BlockSpec((1, d), lambda i, kk: (0, 0)),
        ],
        out_specs=(pl.BlockSpec((tm, d), lambda i, kk: (i, 0)), pl.BlockSpec((tm, d), lambda i, kk: (i, 0))),
        scratch_shapes=[pltpu.VMEM((tm, d), F32)],
        compiler_params=_cparams("parallel", "arbitrary"),
        name="matmul_norm_res",
    )(a, w, x, g_post.reshape(1, d), g_next.reshape(1, d))


def _ple_kernel(p_ref, h_ref, wp_ref, wg_ref, x_ref, gpost_ref, gnext_ref, xo_ref, ho_ref):
    val = _dot(p_ref[...], wp_ref[...]) * jax.nn.sigmoid(_dot(h_ref[...], wg_ref[...]))
    xn = x_ref[...] + _rms(val, gpost_ref[...])
    xo_ref[...] = xn
    ho_ref[...] = _rms(xn, gnext_ref[...]).astype(ho_ref.dtype)


def ple_norm_res(p, h, w_ple, w_gate, x, g_post, g_next, tm=256):
    m, d = x.shape
    pd = p.shape[1]
    row = lambda i: (i, 0)
    fix = lambda i: (0, 0)
    return pl.pallas_call(
        _ple_kernel,
        out_shape=(jax.ShapeDtypeStruct((m, d), F32), jax.ShapeDtypeStruct((m, d), BF16)),
        grid=(m // tm,),
        in_specs=[
            pl.BlockSpec((tm, pd), row), pl.BlockSpec((tm, d), row),
            pl.BlockSpec((pd, d), fix), pl.BlockSpec((d, d), fix),
            pl.BlockSpec((tm, d), row), pl.BlockSpec((1, d), fix), pl.BlockSpec((1, d), fix),
        ],
        out_specs=(pl.BlockSpec((tm, d), row), pl.BlockSpec((tm, d), row)),
        compiler_params=_cparams("parallel"),
        name="ple_norm_res",
    )(p, h, w_ple, w_gate, x, g_post.reshape(1, d), g_next.reshape(1, d))


def _merge_kernel(ya_ref, yb_ref, yc_ref, yd_ref, ga_ref, gb_ref, gc_ref, gd_ref,
                  wa_ref, wb_ref, wc_ref, wd_ref, o_ref):
    acc = ga_ref[...].astype(F32) * _dot(ya_ref[...], wa_ref[...])
    acc += gb_ref[...].astype(F32) * _dot(yb_ref[...], wb_ref[...])
    acc += gc_ref[...].astype(F32) * _dot(yc_ref[...], wc_ref[...])
    acc += gd_ref[...].astype(F32) * _dot(yd_ref[...], wd_ref[...])
    o_ref[...] = acc.astype(o_ref.dtype)


def merge_branches(ys, gates, ws, tm=512, tn=512):
    m = ys[0].shape[0]
    d = ws[0].shape[1]
    nb = d // tn
    in_specs = [pl.BlockSpec((tm, y.shape[1]), lambda i, j: (i, 0)) for y in ys]
    in_specs += [pl.BlockSpec((tm, tn), functools.partial(lambda i, j, b: (i, b * nb + j), b=b)) for b in range(4)]
    in_specs += [pl.BlockSpec((w.shape[0], tn), lambda i, j: (0, j)) for w in ws]
    return pl.pallas_call(
        _merge_kernel,
        out_shape=jax.ShapeDtypeStruct((m, d), BF16),
        grid=(m // tm, nb),
        in_specs=in_specs,
        out_specs=pl.BlockSpec((tm, tn), lambda i, j: (i, j)),
        compiler_params=_cparams("parallel", "arbitrary"),
        name="merge_branches",
    )(*ys, gates, gates, gates, gates, *ws)


def _ffn_up_kernel(h_ref, wg_ref, wu_ref, cwg_ref, cwu_ref, cbg_ref, cbu_ref, o_ref, bufg_ref, bufu_ref,
                   *, tm, seq_blocks):
    i = pl.program_id(1)

    @pl.when(i % seq_blocks == 0)
    def _():
        bufg_ref[0:SUBLANES, :] = jnp.zeros((SUBLANES, bufg_ref.shape[1]), F32)
        bufu_ref[0:SUBLANES, :] = jnp.zeros((SUBLANES, bufu_ref.shape[1]), F32)

    h = h_ref[...]

    def conv(w_ref, cw_ref, cb_ref, buf_ref):
        buf_ref[SUBLANES:SUBLANES + tm, :] = _dot(h, w_ref[...])
        cw = cw_ref[...]
        out = cb_ref[...] + cw[2:3, :] * buf_ref[SUBLANES:SUBLANES + tm, :]
        out += cw[1:2, :] * buf_ref[SUBLANES - 1:SUBLANES - 1 + tm, :]
        out += cw[0:1, :] * buf_ref[SUBLANES - 2:SUBLANES - 2 + tm, :]
        buf_ref[0:SUBLANES, :] = buf_ref[tm:tm + SUBLANES, :]
        return out

    g = conv(wg_ref, cwg_ref, cbg_ref, bufg_ref)
    u = conv(wu_ref, cwu_ref, cbu_ref, bufu_ref)
    o_ref[...] = (_gelu(g) * u).astype(o_ref.dtype)


def ffn_up(h, w_up, conv_w, conv_b, seq, tm=512, tn=512):
    m, d = h.shape
    dff = w_up.shape[1] // 2
    nb = dff // tn
    cb = conv_b.reshape(1, 2 * dff)
    return pl.pallas_call(
        functools.partial(_ffn_up_kernel, tm=tm, seq_blocks=seq // tm),
        out_shape=jax.ShapeDtypeStruct((m, dff), BF16),
        grid=(nb, m // tm),
        in_specs=[
            pl.BlockSpec((tm, d), lambda j, i: (i, 0)),
            pl.BlockSpec((d, tn), lambda j, i: (0, j)),
            pl.BlockSpec((d, tn), lambda j, i: (0, j + nb)),
            pl.BlockSpec((FFN_CONV, tn), lambda j, i: (0, j)),
            pl.BlockSpec((FFN_CONV, tn), lambda j, i: (0, j + nb)),
            pl.BlockSpec((1, tn), lambda j, i: (0, j)),
            pl.BlockSpec((1, tn), lambda j, i: (0, j + nb)),
        ],
        out_specs=pl.BlockSpec((tm, tn), lambda j, i: (i, j)),
        scratch_shapes=[pltpu.VMEM((tm + SUBLANES, tn), F32), pltpu.VMEM((tm + SUBLANES, tn), F32)],
        compiler_params=_cparams("parallel", "arbitrary"),
        name="ffn_up_conv_glu",
    )(h, w_up, w_up, conv_w, conv_w, cb, cb)


def _lru_kernel(x_ref, gate_ref, cw_ref, cb_ref, wr_ref, br_ref, wi_ref, bi_ref, lam_ref, o_ref,
                xbuf_ref, a0_ref, h0_ref, a1_ref, h1_ref, *, seq, pad):
    xbuf_ref[0:SUBLANES, :] = jnp.zeros((SUBLANES, LANES), F32)
    xbuf_ref[SUBLANES:SUBLANES + seq, :] = x_ref[0]
    cw = cw_ref[...]
    u = cb_ref[...] + cw[3:4, :] * xbuf_ref[SUBLANES:SUBLANES + seq, :]
    for k in range(LRU_CONV - 1):
        off = SUBLANES - (LRU_CONV - 1) + k
        u += cw[k:k + 1, :] * xbuf_ref[off:off + seq, :]
    ub = u.astype(BF16)
    r = jax.nn.sigmoid(_dot(ub, wr_ref[0]) + br_ref[...])
    ig = jax.nn.sigmoid(_dot(ub, wi_ref[0]) + bi_ref[...])
    log_a = -LRU_C * r * _softplus(-lam_ref[...])
    a = jnp.exp(log_a)
    inp = jnp.sqrt(1.0 - jnp.exp(2.0 * log_a)) * ig * u

    ones = jnp.ones((pad, LANES), F32)
    zeros = jnp.zeros((pad, LANES), F32)
    a0_ref[0:pad, :] = ones
    a1_ref[0:pad, :] = ones
    h0_ref[0:pad, :] = zeros
    h1_ref[0:pad, :] = zeros
    a0_ref[pad:pad + seq, :] = a
    h0_ref[pad:pad + seq, :] = inp
    bufs = ((a0_ref, h0_ref), (a1_ref, h1_ref))
    d = 1
    level = 0
    while d < seq:
        (a_src, h_src), (a_dst, h_dst) = bufs[level % 2], bufs[(level + 1) % 2]
        a_cur = a_src[pad:pad + seq, :]
        h_dst[pad:pad + seq, :] = h_src[pad:pad + seq, :] + a_cur * h_src[pad - d:pad - d + seq, :]
        a_dst[pad:pad + seq, :] = a_cur * a_src[pad - d:pad - d + seq, :]
        d *= 2
        level += 1
    h = bufs[level % 2][1][pad:pad + seq, :]
    o_ref[0] = (h * _gelu(gate_ref[0])).astype(o_ref.dtype)


def rglru(xg, conv_w, conv_b, w_r, b_r, w_i, b_i, lam):
    b, s, w2 = xg.shape
    w = w2 // 2
    nblk = w // LANES
    pad = s
    vec = lambda v: v.reshape(1, w)
    vspec = pl.BlockSpec((1, LANES), lambda bi, c: (0, c))
    return pl.pallas_call(
        functools.partial(_lru_kernel, seq=s, pad=pad),
        out_shape=jax.ShapeDtypeStruct((b, s, w), BF16),
        grid=(b, nblk),
        in_specs=[
            pl.BlockSpec((1, s, LANES), lambda bi, c: (bi, 0, c)),
            pl.BlockSpec((1, s, LANES), lambda bi, c: (bi, 0, c + nblk)),
            pl.BlockSpec((LRU_CONV, LANES), lambda bi, c: (0, c)),
            vspec,
            pl.BlockSpec((1, LANES, LANES), lambda bi, c: (c, 0, 0)),
            vspec,
            pl.BlockSpec((1, LANES, LANES), lambda bi, c: (c, 0, 0)),
            vspec, vspec,
        ],
        out_specs=pl.BlockSpec((1, s, LANES), lambda bi, c: (bi, 0, c)),
        scratch_shapes=[pltpu.VMEM((s + SUBLANES, LANES), F32)] + [pltpu.VMEM((pad + s, LANES), F32)] * 4,
        compiler_params=_cparams("parallel", "parallel"),
        name="rglru",
    )(xg, xg, conv_w, vec(conv_b), w_r, vec(b_r), w_i, vec(b_i), vec(lam))


def _dil_kernel(q_ref, kp_ref, kc_ref, vp_ref, vc_ref, o_ref, lse_ref):
    n = pl.program_id(3)
    q = q_ref[0]
    scale = HEAD_DIM ** -0.5
    s_p = _dot_nt(q, kp_ref[0]) * scale
    s_c = _dot_nt(q, kc_ref[0]) * scale
    qi = lax.broadcasted_iota(jnp.int32, (BLOCK, BLOCK), 0)
    kj = lax.broadcasted_iota(jnp.int32, (BLOCK, BLOCK), 1)
    lim = jnp.where(n > 0, qi, BLOCK)
    s_p = jnp.where(kj >= lim, s_p, NEG_INF)
    s_c = jnp.where(kj <= qi, s_c, NEG_INF)
    m = jnp.maximum(jnp.max(s_p, axis=-1, keepdims=True), jnp.max(s_c, axis=-1, keepdims=True))
    e_p = jnp.exp(s_p - m)
    e_c = jnp.exp(s_c - m)
    den = jnp.sum(e_p, axis=-1, keepdims=True) + jnp.sum(e_c, axis=-1, keepdims=True)
    o = _dot(e_p.astype(BF16), vp_ref[0]) + _dot(e_c.astype(BF16), vc_ref[0])
    o_ref[0] = o / den
    lse_ref[0] = jnp.broadcast_to(m + jnp.log(den), (BLOCK, HEAD_DIM))


def dilated_group(qkv, group, dilation):
    b, s, width = qkv.shape
    ncol = width // HEAD_DIM
    nheads = ncol // 3
    ln = s // dilation
    nb = ln // BLOCK
    view = qkv.reshape(b, ln, dilation * width)
    hq = group * DIL_HEADS

    def spec(which, prev):
        def imap(bi, r, h, n):
            nn = jnp.maximum(n - 1, 0) if prev else n
            return (bi, nn, r * ncol + which * nheads + hq + h)
        return pl.BlockSpec((1, BLOCK, HEAD_DIM), imap)

    out_w = DIL_HEADS * HEAD_DIM
    ospec = pl.BlockSpec((1, BLOCK, HEAD_DIM), lambda bi, r, h, n: (bi, n, r * DIL_HEADS + h))
    o, lse = pl.pallas_call(
        _dil_kernel,
        out_shape=(jax.ShapeDtypeStruct((b, ln, dilation * out_w), F32),) * 2,
        grid=(b, dilation, DIL_HEADS, nb),
        in_specs=[spec(0, False), spec(1, True), spec(1, False), spec(2, True), spec(2, False)],
        out_specs=(ospec, ospec),
        compiler_params=_cparams("parallel", "parallel", "parallel", "arbitrary"),
        name="dilated_attention",
    )(view, view, view, view, view)
    return o.reshape(b, s, out_w), lse.reshape(b, s, out_w)


def _dil_combine_kernel(o0_ref, o1_ref, o2_ref, l0_ref, l1_ref, l2_ref, y_ref):
    l0, l1, l2 = l0_ref[...], l1_ref[...], l2_ref[...]
    m = jnp.maximum(jnp.maximum(l0, l1), l2)
    e0, e1, e2 = jnp.exp(l0 - m), jnp.exp(l1 - m), jnp.exp(l2 - m)
    y = (e0 * o0_ref[...] + e1 * o1_ref[...] + e2 * o2_ref[...]) / (e0 + e1 + e2)
    y_ref[...] = y.astype(y_ref.dtype)


def dilated_combine(os_, lses, tm=512):
    m, w = os_[0].shape
    spec = pl.BlockSpec((tm, w), lambda i: (i, 0))
    return pl.pallas_call(
        _dil_combine_kernel,
        out_shape=jax.ShapeDtypeStruct((m, w), BF16),
        grid=(m // tm,),
        in_specs=[spec] * 6,
        out_specs=spec,
        compiler_params=_cparams("parallel"),
        name="dilated_combine",
    )(*os_, *lses)


def _sb_kernel(q_ref, k_ref, v_ref, o_ref, acc_ref, run_ref):
    n = pl.program_id(1)
    nh = q_ref.shape[1] // HEAD_DIM
    heads = range(nh)
    sls = [slice(h * HEAD_DIM, (h + 1) * HEAD_DIM) for h in heads]
    scale = HEAD_DIM ** -0.5
    row = lax.broadcasted_iota(jnp.int32, (BLOCK, BLOCK), 0)
    col = lax.broadcasted_iota(jnp.int32, (BLOCK, BLOCK), 1)
    later_ones = jnp.concatenate([(row > col).astype(BF16), jnp.ones((BLOCK, BLOCK), BF16)], axis=1)
    strict = col < row
    q = [q_ref[:, sl] for sl in sls]

    def block(j, diag):
        start = pl.multiple_of(j * BLOCK, BLOCK)
        z = [_dot_nt(q[h], k_ref[pl.ds(start, BLOCK), sls[h]]) * scale for h in heads]
        log_beta = [jnp.minimum(zz, 0.0) - jnp.log(1.0 + jnp.exp(-jnp.abs(zz))) for zz in z]
        log_1m = [lb - zz for lb, zz in zip(log_beta, z)]
        if diag:
            log_1m = [jnp.where(strict, x, 0.0) for x in log_1m]
        sums = [_dot_exact_rhs(x, later_ones) for x in log_1m]
        for h in heads:
            prev = jnp.zeros((BLOCK, BLOCK), F32) if diag else run_ref[h]
            att = jnp.exp(log_beta[h] + sums[h][:, :BLOCK] + prev)
            if diag:
                att = jnp.where(strict, att, 0.0)
            pv = _dot(att.astype(BF16), v_ref[pl.ds(start, BLOCK), sls[h]])
            if diag:
                acc_ref[:, sls[h]] = pv
                run_ref[h] = sums[h][:, BLOCK:]
            else:
                acc_ref[:, sls[h]] += pv
                run_ref[h] = prev + sums[h][:, BLOCK:]

    block(n, True)

    def body(t, carry):
        block(n - 1 - t, False)
        return carry

    lax.fori_loop(0, n, body, 0)
    o_ref[...] = acc_ref[...].astype(o_ref.dtype)


def stick_breaking(qkv, batch, seq):
    t, width = qkv.shape
    w = width // 3
    nq = seq // BLOCK
    return pl.pallas_call(
        _sb_kernel,
        out_shape=jax.ShapeDtypeStruct((t, w), BF16),
        grid=(batch, nq),
        in_specs=[
            pl.BlockSpec((BLOCK, w), lambda b, n: (b * nq + n, 0)),
            pl.BlockSpec((seq, w), lambda b, n: (b, 1)),
            pl.BlockSpec((seq, w), lambda b, n: (b, 2)),
        ],
        out_specs=pl.BlockSpec((BLOCK, w), lambda b, n: (b * nq + n, 0)),
        scratch_shapes=[pltpu.VMEM((BLOCK, w), F32), pltpu.VMEM((w // HEAD_DIM, BLOCK, BLOCK), F32)],
        compiler_params=_cparams("parallel", "arbitrary"),
        name="stick_breaking",
    )(qkv, qkv, qkv)


def _head_sum(x, bd):
    cols = []
    for c in range(x.shape[1] // LANES):
        cols.append(_dot_exact_rhs(x[:, c * LANES:(c + 1) * LANES], bd))
    return jnp.concatenate(cols, axis=1)


def _rwkv_prep_kernel(*refs, tm, width, seq_blocks, has_vres):
    if has_vres:
        (seg_ref, prev_ref, mu_ref, w0_ref, wup_ref, a0_ref, aup_ref, gup_ref, kk_ref, ka_ref, rk_ref,
         vf_ref, v0_ref, vdn_ref, vup_ref,
         r_o, k_o, v_o, lw_o, kk_o, b_o, g_o, bon_o, buf_ref) = refs
    else:
        (seg_ref, prev_ref, mu_ref, w0_ref, wup_ref, a0_ref, aup_ref, gup_ref, kk_ref, ka_ref, rk_ref,
         r_o, k_o, v_o, lw_o, kk_o, b_o, g_o, bon_o, buf_ref) = refs
    i = pl.program_id(0)
    seg = seg_ref[...]
    buf_ref[0:SUBLANES, :] = prev_ref[...]

    @pl.when(i % seq_blocks == 0)
    def _():
        buf_ref[0:SUBLANES, :] = jnp.zeros((SUBLANES, buf_ref.shape[1]), F32)

    buf_ref[SUBLANES:SUBLANES + tm, :] = seg
    shifted = buf_ref[SUBLANES - 1:SUBLANES - 1 + tm, :]
    xs = seg + (shifted - seg) * mu_ref[...]
    w = width
    r = xs[:, 0:w]
    k = xs[:, w:2 * w]
    v = xs[:, 2 * w:3 * w]
    low = xs[:, 3 * w:3 * w + LANES]
    g_low = xs[:, 3 * w + LANES:3 * w + 3 * LANES]
    wpre = w0_ref[...] + _dot(jnp.tanh(low).astype(BF16), wup_ref[...])
    wlog = -_softplus(-wpre) - 0.5
    lw_o[...] = -jnp.exp(wlog)
    a = jax.nn.sigmoid(a0_ref[...] + _dot(low.astype(BF16), aup_ref[...]))
    g_o[...] = _dot(jax.nn.sigmoid(g_low).astype(BF16), gup_ref[...])
    if has_vres:
        mix = jax.nn.sigmoid(v0_ref[...] + _dot(_dot(v.astype(BF16), vdn_ref[...]).astype(BF16), vup_ref[...]))
        v = v + (vf_ref[...] - v) * mix
    row = lax.broadcasted_iota(jnp.int32, (LANES, LANES), 0) // RWKV_HEAD
    col = lax.broadcasted_iota(jnp.int32, (LANES, LANES), 1) // RWKV_HEAD
    bd = (row == col).astype(BF16)
    kk = k * kk_ref[...]
    norm = jnp.sqrt(_head_sum(kk * kk, bd))
    kk = kk / jnp.maximum(norm, 1e-12)
    k2 = k * (1.0 + (a - 1.0) * ka_ref[...])
    bonus = _head_sum(r * k2 * rk_ref[...], bd) * v
    r_o[...] = r
    k_o[...] = k2
    v_o[...] = v
    kk_o[...] = kk
    b_o[...] = kk * a
    bon_o[...] = bonus


def rwkv_prep(seg, seq, mu, w0, w_up, a0, a_up, g_up, k_k, k_a, r_k, v_first, v_res, tm=256):
    t, wpad = seg.shape
    w = w0.shape[0]
    has_vres = v_res is not None
    row = lambda i: (i, 0)
    fix = lambda i: (0, 0)
    vec = lambda x: x.reshape(1, -1)
    step = tm // SUBLANES
    in_specs = [
        pl.BlockSpec((tm, wpad), row),
        pl.BlockSpec((SUBLANES, wpad), lambda i: (jnp.maximum(i * step - 1, 0), 0)),
        pl.BlockSpec((1, wpad), fix), pl.BlockSpec((1, w), fix), pl.BlockSpec((LANES, w), fix),
        pl.BlockSpec((1, w), fix), pl.BlockSpec((LANES, w), fix), pl.BlockSpec((2 * LANES, w), fix),
        pl.BlockSpec((1, w), fix), pl.BlockSpec((1, w), fix), pl.BlockSpec((1, w), fix),
    ]
    args = [seg, seg, vec(mu), vec(w0), w_up, vec(a0), a_up, g_up, vec(k_k), vec(k_a), vec(r_k)]
    if has_vres:
        v0, v_down, v_up = v_res
        in_specs += [pl.BlockSpec((tm, w), row), pl.BlockSpec((1, w), fix),
                     pl.BlockSpec((w, LANES), fix), pl.BlockSpec((LANES, w), fix)]
        args += [v_first, vec(v0), v_down, v_up]
    out = jax.ShapeDtypeStruct((t, w), F32)
    return pl.pallas_call(
        functools.partial(_rwkv_prep_kernel, tm=tm, width=w, seq_blocks=seq // tm, has_vres=has_vres),
        out_shape=(out,) * 8,
        grid=(t // tm,),
        in_specs=in_specs,
        out_specs=(pl.BlockSpec((tm, w), row),) * 8,
        scratch_shapes=[pltpu.VMEM((tm + SUBLANES, wpad), F32)],
        compiler_params=_cparams("parallel"),
        name="rwkv_prep",
    )(*args)


def _rwkv_chunk_kernel(r_ref, k_ref, v_ref, lw_ref, kk_ref, b_ref, g_ref, bon_ref, gnw_ref, gnb_ref, o_ref,
                       state_ref):
    c = pl.program_id(1)
    n = RWKV_CHUNK
    n2 = 2 * n

    @pl.when(c == 0)
    def _():
        state_ref[...] = jnp.zeros_like(state_ref)

    tri = (lax.broadcasted_iota(jnp.int32, (n, n), 1) <= lax.broadcasted_iota(jnp.int32, (n, n), 0)).astype(BF16)
    row2 = lax.broadcasted_iota(jnp.int32, (n2, n2), 0)
    col2 = lax.broadcasted_iota(jnp.int32, (n2, n2), 1)
    t2 = row2 & (n - 1)
    s2 = col2 & (n - 1)
    strict = s2 < t2
    incl = s2 <= t2
    eye = (row2 == col2).astype(F32)
    head0 = lax.broadcasted_iota(jnp.int32, (1, LANES), 1) < RWKV_HEAD
    own = jnp.concatenate([jnp.broadcast_to(head0, (n, LANES)), jnp.broadcast_to(~head0, (n, LANES))], axis=0)

    def stack(x):
        return jnp.where(own, jnp.concatenate([x, x], axis=0), 0.0)

    pairs = range(r_ref.shape[1] // LANES)
    sls = [slice(hp * LANES, (hp + 1) * LANES) for hp in pairs]
    lw = lw_ref[...]
    gsum = _dot_exact_rhs_left(tri, lw)
    p_inv = jnp.exp(-gsum)
    p_end = jnp.exp(gsum[n - 1:n, :])
    rt_f = r_ref[...] * jnp.exp(gsum)
    kt_f = k_ref[...] * p_inv
    bt_f = b_ref[...] * p_inv
    kap_f = kk_ref[...] * jnp.exp(gsum - lw)
    v_f = v_ref[...]
    rt = [stack(rt_f[:, sl]).astype(BF16) for sl in sls]
    kt = [stack(kt_f[:, sl]) for sl in sls]
    bt = [stack(bt_f[:, sl]) for sl in sls]
    kap = [stack(kap_f[:, sl]).astype(BF16) for sl in sls]
    v2 = [stack(v_f[:, sl]).astype(BF16) for sl in sls]
    prod = [_dot_nt(jnp.concatenate([kap[hp], rt[hp]], axis=0),
                    jnp.concatenate([bt[hp].astype(BF16), kt[hp].astype(BF16)], axis=0)) for hp in pairs]
    a_ab = [jnp.where(strict, prod[hp][:n2, :n2], 0.0) for hp in pairs]
    a_ak = [jnp.where(strict, prod[hp][:n2, n2:], 0.0).astype(BF16) for hp in pairs]
    a_r = [jnp.concatenate([jnp.where(incl, prod[hp][n2:, n2:], 0.0).astype(BF16),
                            jnp.where(incl, -prod[hp][n2:, :n2], 0.0).astype(BF16)], axis=1) for hp in pairs]
    x = [eye - a_ab[hp] for hp in pairs]
    q = [_dot(a_ab[hp].astype(BF16), a_ab[hp].astype(BF16)) for hp in pairs]
    steps = 1
    while True:
        x = [x[hp] + _dot(x[hp].astype(BF16), q[hp].astype(BF16)) for hp in pairs]
        steps *= 2
        if steps * 2 >= n:
            break
        q = [_dot(q[hp].astype(BF16), q[hp].astype(BF16)) for hp in pairs]
    s0 = [state_ref[hp] for hp in pairs]
    s0b = [s.astype(BF16) for s in s0]
    rhs = [_dot_nt(kap[hp], s0b[hp]) + _dot(a_ak[hp], v2[hp]) for hp in pairs]
    u2 = [_dot(x[hp].astype(BF16), rhs[hp].astype(BF16)).astype(BF16) for hp in pairs]
    vu = [jnp.concatenate([v2[hp], u2[hp]], axis=0) for hp in pairs]
    y2 = [_dot_nt(rt[hp], s0b[hp]) + _dot(a_r[hp], vu[hp]) for hp in pairs]
    inv_n = 1.0 / RWKV_HEAD
    outs = []
    for hp in pairs:
        pe = p_end[:, sls[hp]]
        kb_end = jnp.concatenate([(kt[hp] * pe).astype(BF16), (-(bt[hp] * pe)).astype(BF16)], axis=0)
        state_ref[hp] = s0[hp] * pe + _dot_tn(vu[hp], kb_end)
        mean = jnp.sum(y2[hp], axis=-1, keepdims=True) * inv_n
        cen = jnp.where(own, y2[hp] - mean, 0.0)
        var = jnp.sum(cen * cen, axis=-1, keepdims=True) * inv_n
        yn2 = cen * lax.rsqrt(var + RWKV_GN_EPS)
        outs.append(yn2[:n] + yn2[n:])
    yn = jnp.concatenate(outs, axis=1) * gnw_ref[...] + gnb_ref[...]
    o_ref[...] = ((yn + bon_ref[...]) * g_ref[...]).astype(o_ref.dtype)


def _dot_exact_rhs_left(m_bf16, x):
    hi, lo = _split_bf16(x)
    return _dot(m_bf16, hi) + _dot(m_bf16, lo)


def rwkv_chunks(r, k, v, lw, kk, b, g, bon, gn_w, gn_b, batch, seq):
    t, w = r.shape
    nc = seq // RWKV_CHUNK
    spec = pl.BlockSpec((RWKV_CHUNK, w), lambda bi, c: (bi * nc + c, 0))
    vspec = pl.BlockSpec((1, w), lambda bi, c: (0, 0))
    return pl.pallas_call(
        _rwkv_chunk_kernel,
        out_shape=jax.ShapeDtypeStruct((t, w), BF16),
        grid=(batch, nc),
        in_specs=[spec] * 8 + [vspec, vspec],
        out_specs=spec,
        scratch_shapes=[pltpu.VMEM((w // LANES, LANES, LANES), F32)],
        compiler_params=_cparams("parallel", "arbitrary"),
        name="rwkv_chunks",
    )(r, k, v, lw, kk, b, g, bon, gn_w.reshape(1, w), gn_b.reshape(1, w))


def _pad_rows(w, rows):
    return jnp.pad(w, ((0, rows - w.shape[0]), (0, 0)))


def _pad_cols(w, cols):
    return jnp.pad(w, ((0, 0), (0, cols - w.shape[1])))


def _rope_tables(seq):
    half = HEAD_DIM // 2
    inv_freq = ROPE_THETA ** (-jnp.arange(half, dtype=F32) / half)
    ang = jnp.arange(seq, dtype=F32)[:, None] * inv_freq[None, :]
    cos = jnp.cos(ang)
    sin = jnp.sin(ang)
    return jnp.concatenate([cos, cos], axis=1), jnp.concatenate([-sin, sin], axis=1)


def kernel(x, p, norm_mix_pre, norm_mix_post, norm_ffn_pre, norm_ffn_post, norm_ple_pre, norm_ple_post, w_in, w_merge_gate, lru_conv_w, lru_conv_b, lru_w_r, lru_b_r, lru_w_i, lru_b_i, lru_lambda, rwkv_mu, rwkv_w0, rwkv_w_up, rwkv_a0, rwkv_a_up, rwkv_g_up, rwkv_k_k, rwkv_k_a, rwkv_r_k, rwkv_gn_w, rwkv_gn_b, rwkv_v0, rwkv_v_down, rwkv_v_up, w_branch_a, w_branch_b, w_branch_c, w_branch_d, w_out, w_ffn_up, ffn_conv_w, ffn_conv_b, w_ffn_down, w_ple, w_ple_gate):
    batch, seq, d = x.shape
    depth = w_in.shape[0]
    t = batch * seq
    lru_w = lru_conv_w.shape[2]
    rw = rwkv_w0.shape[1]
    dil_w = 3 * len(DIL_CONFIGS) * DIL_HEADS * HEAD_DIM
    sb_w = 3 * (d // 2)
    off_b = 2 * lru_w
    off_c = off_b + dil_w
    off_d = off_c + sb_w
    rwkv_in = w_in.shape[2] - off_d
    rwkv_pad = 3 * rw + 4 * LANES
    cos, sin = _rope_tables(seq)

    xf = x.reshape(t, d)
    h = rmsnorm_bf16(xf, norm_mix_pre[0])
    v_first = None
    for i in range(depth):
        wi = w_in[i]
        w_a = wi[:, :off_b].astype(BF16)
        w_b = wi[:, off_b:off_c].astype(BF16)
        w_c = wi[:, off_c:off_d].astype(BF16)
        w_d = _pad_cols(wi[:, off_d:], rwkv_pad).astype(BF16)
        seg_a = matmul(h, w_a, F32)
        seg_b = matmul_rope(h, w_b, cos, sin, seq, 2 * dil_w // 3, BF16)
        seg_c = matmul(h, w_c, BF16)
        seg_d = matmul(h, w_d, F32)
        gates = matmul(h, w_merge_gate[i].astype(BF16), BF16, act="sigmoid")
        y_a = rglru(seg_a.reshape(batch, seq, off_b), lru_conv_w[i], lru_conv_b[i], lru_w_r[i].astype(BF16),
                    lru_b_r[i], lru_w_i[i].astype(BF16), lru_b_i[i], lru_lambda[i]).reshape(t, lru_w)
        qkv_b = seg_b.reshape(batch, seq, dil_w)
        os_, lses = [], []
        for g, (_, dilation) in enumerate(DIL_CONFIGS):
            o_g, lse_g = dilated_group(qkv_b, g, dilation)
            os_.append(o_g.reshape(t, -1))
            lses.append(lse_g.reshape(t, -1))
        y_b = dilated_combine(os_, lses)
        y_c = stick_breaking(seg_c, batch, seq)
        mu = jnp.pad(rwkv_mu[i], (0, rwkv_pad - rwkv_in))
        w_up = _pad_rows(rwkv_w_up[i], LANES).astype(BF16)
        a_up = jnp.pad(rwkv_a_up[i], ((RWKV_W_LORA, LANES - RWKV_W_LORA - RWKV_A_LORA), (0, 0))).astype(BF16)
        g_up = _pad_rows(rwkv_g_up[i], 2 * LANES).astype(BF16)
        v_res = None
        if i > 0:
            v_res = (rwkv_v0[i - 1], _pad_cols(rwkv_v_down[i - 1], LANES).astype(BF16),
                     _pad_rows(rwkv_v_up[i - 1], LANES).astype(BF16))
        r_, k_, v_, lw_, kk_, b_, g_, bon_ = rwkv_prep(
            seg_d, seq, mu, rwkv_w0[i], w_up, rwkv_a0[i], a_up, g_up, rwkv_k_k[i], rwkv_k_a[i],
            rwkv_r_k[i].reshape(-1), v_first, v_res)
        if i == 0:
            v_first = v_
        y_d = rwkv_chunks(r_, k_, v_, lw_, kk_, b_, g_, bon_, rwkv_gn_w[i], rwkv_gn_b[i], batch, seq)
        merged = merge_branches(
            (y_a, y_b, y_c, y_d), gates,
            (w_branch_a[i].astype(BF16), w_branch_b[i].astype(BF16), w_branch_c[i].astype(BF16),
             w_branch_d[i].astype(BF16)))
        xf, h = matmul_norm_res(merged, w_out[i].astype(BF16), xf, norm_mix_post[i], norm_ffn_pre[i], tk=d)
        act = ffn_up(h, w_ffn_up[i].astype(BF16), ffn_conv_w[i], ffn_conv_b[i], seq)
        xf, h = matmul_norm_res(act, w_ffn_down[i].astype(BF16), xf, norm_ffn_post[i], norm_ple_pre[i])
        g_next = norm_mix_pre[i + 1] if i + 1 < depth else norm_mix_pre[i]
        xf, h = ple_norm_res(p[i].reshape(t, -1).astype(BF16), h, w_ple[i].astype(BF16),
                             w_ple_gate[i].astype(BF16), xf, norm_ple_post[i], g_next)
    return xf.reshape(batch, seq, d)
```

```python
import functools

import jax
import jax.numpy as jnp
from jax import lax
from jax.experimental import pallas as pl
from jax.experimental.pallas import tpu as pltpu

F32 = jnp.float32
BF16 = jnp.bfloat16

LANES = 128
SUBLANES = 8
VMEM_LIMIT_BYTES = 52 * 1024 * 1024

HEAD_DIM = 128
BLOCK = 128
ROPE_THETA = 10000.0
RMS_EPS = 1e-6
NEG_INF = -1e30

LRU_BLOCKS = 8
LRU_CONV = 4
LRU_C = 8.0
DIL_CONFIGS = ((128, 1), (512, 4), (2048, 16))
DIL_HEADS = 4
RWKV_HEAD = 64
RWKV_W_LORA = 64
RWKV_A_LORA = 64
RWKV_G_LORA = 160
RWKV_V_LORA = 32
RWKV_GN_EPS = 64e-5
RWKV_CHUNK = 64
FFN_CONV = 3


def _cparams(*sem):
    return pltpu.CompilerParams(dimension_semantics=sem, vmem_limit_bytes=VMEM_LIMIT_BYTES)


def _dot(a, b):
    return jnp.dot(a, b, preferred_element_type=F32)


def _dot_nt(a, b):
    return lax.dot_general(a, b, (((1,), (1,)), ((), ())), preferred_element_type=F32)


def _dot_tn(a, b):
    return lax.dot_general(a, b, (((0,), (0,)), ((), ())), preferred_element_type=F32)


def _split_bf16(x):
    hi = x.astype(BF16)
    lo = (x - hi.astype(F32)).astype(BF16)
    return hi, lo


def _dot_exact_rhs(x, m_bf16):
    hi, lo = _split_bf16(x)
    return _dot(hi, m_bf16) + _dot(lo, m_bf16)


def _rms(x, g):
    return x * lax.rsqrt(jnp.mean(x * x, axis=-1, keepdims=True) + RMS_EPS) * g


def _gelu(x):
    return jax.nn.gelu(x, approximate=True)


def _softplus(x):
    return jnp.maximum(x, 0.0) + jnp.log1p(jnp.exp(-jnp.abs(x)))


def _rmsnorm_kernel(x_ref, g_ref, o_ref):
    o_ref[...] = _rms(x_ref[...], g_ref[...]).astype(o_ref.dtype)


def rmsnorm_bf16(x, g, tm=512):
    t, d = x.shape
    return pl.pallas_call(
        _rmsnorm_kernel,
        out_shape=jax.ShapeDtypeStruct((t, d), BF16),
        grid=(t // tm,),
        in_specs=[pl.BlockSpec((tm, d), lambda i: (i, 0)), pl.BlockSpec((1, d), lambda i: (0, 0))],
        out_specs=pl.BlockSpec((tm, d), lambda i: (i, 0)),
        compiler_params=_cparams("parallel"),
        name="rmsnorm",
    )(x, g.reshape(1, d))


def _mm_kernel(a_ref, w_ref, o_ref, *, act):
    acc = _dot(a_ref[...], w_ref[...])
    if act == "sigmoid":
        acc = jax.nn.sigmoid(acc)
    o_ref[...] = acc.astype(o_ref.dtype)


def _mm_rope_kernel(a_ref, w_ref, cos_ref, sin_ref, o_ref, *, n_rope_blocks, tn):
    j = pl.program_id(1)
    acc = _dot(a_ref[...], w_ref[...])

    @pl.when(j < n_rope_blocks)
    def _():
        cos = cos_ref[...]
        sin = sin_ref[...]
        for c in range(tn // HEAD_DIM):
            seg = acc[:, c * HEAD_DIM:(c + 1) * HEAD_DIM]
            rot = pltpu.roll(seg, HEAD_DIM // 2, axis=1)
            o_ref[:, c * HEAD_DIM:(c + 1) * HEAD_DIM] = (seg * cos + rot * sin).astype(o_ref.dtype)

    @pl.when(j >= n_rope_blocks)
    def _():
        o_ref[...] = acc.astype(o_ref.dtype)


def matmul(a, w, out_dtype, act=None, tm=1024, tn=1024):
    m, k = a.shape
    n = w.shape[1]
    tm, tn = min(tm, m), min(tn, n)
    return pl.pallas_call(
        functools.partial(_mm_kernel, act=act),
        out_shape=jax.ShapeDtypeStruct((m, n), out_dtype),
        grid=(m // tm, n // tn),
        in_specs=[pl.BlockSpec((tm, k), lambda i, j: (i, 0)), pl.BlockSpec((k, tn), lambda i, j: (0, j))],
        out_specs=pl.BlockSpec((tm, tn), lambda i, j: (i, j)),
        compiler_params=_cparams("parallel", "arbitrary"),
        name="matmul_" + (act or "plain"),
    )(a, w)


def matmul_rope(a, w, cos, sin, seq, n_rope_cols, out_dtype, tm=1024, tn=1536):
    m, k = a.shape
    n = w.shape[1]
    tm, tn = min(tm, m), min(tn, n)
    sblocks = seq // tm
    return pl.pallas_call(
        functools.partial(_mm_rope_kernel, n_rope_blocks=n_rope_cols // tn, tn=tn),
        out_shape=jax.ShapeDtypeStruct((m, n), out_dtype),
        grid=(m // tm, n // tn),
        in_specs=[
            pl.BlockSpec((tm, k), lambda i, j: (i, 0)),
            pl.BlockSpec((k, tn), lambda i, j: (0, j)),
            pl.BlockSpec((tm, HEAD_DIM), lambda i, j: (i % sblocks, 0)),
            pl.BlockSpec((tm, HEAD_DIM), lambda i, j: (i % sblocks, 0)),
        ],
        out_specs=pl.BlockSpec((tm, tn), lambda i, j: (i, j)),
        compiler_params=_cparams("parallel", "arbitrary"),
        name="matmul_rope",
    )(a, w, cos, sin)


def _mm_norm_res_kernel(a_ref, w_ref, x_ref, gpost_ref, gnext_ref, xo_ref, ho_ref, acc_ref, *, nk):
    kk = pl.program_id(1)

    @pl.when(kk == 0)
    def _():
        acc_ref[...] = jnp.zeros_like(acc_ref)

    acc_ref[...] += _dot(a_ref[...], w_ref[...])

    @pl.when(kk == nk - 1)
    def _():
        xn = x_ref[...] + _rms(acc_ref[...], gpost_ref[...])
        xo_ref[...] = xn
        ho_ref[...] = _rms(xn, gnext_ref[...]).astype(ho_ref.dtype)


def matmul_norm_res(a, w, x, g_post, g_next, tm=256, tk=512):
    m, k = a.shape
    d = w.shape[1]
    nk = k // tk
    return pl.pallas_call(
        functools.partial(_mm_norm_res_kernel, nk=nk),
        out_shape=(jax.ShapeDtypeStruct((m, d), F32), jax.ShapeDtypeStruct((m, d), BF16)),
        grid=(m // tm, nk),
        in_specs=[
            pl.BlockSpec((tm, tk), lambda i, kk: (i, kk)),
            pl.BlockSpec((tk, d), lambda i, kk: (kk, 0)),
            pl.BlockSpec((tm, d), lambda i, kk: (i, 0)),
            pl.BlockSpec((1, d), lambda i, kk: (0, 0)),
            pl.BlockSpec((1, d), lambda i, kk: (0, 0)),
        ],
        out_specs=(pl.BlockSpec((tm, d), lambda i, kk: (i, 0)), pl.BlockSpec((tm, d), lambda i, kk: (i, 0))),
        scratch_shapes=[pltpu.VMEM((tm, d), F32)],
        compiler_params=_cparams("parallel", "arbitrary"),
        name="matmul_norm_res",
    )(a, w, x, g_post.reshape(1, d), g_next.reshape(1, d))


def _ple_kernel(p_ref, h_ref, wp_ref, wg_ref, x_ref, gpost_ref, gnext_ref, xo_ref, ho_ref):
    val = _dot(p_ref[...], wp_ref[...]) * jax.nn.sigmoid(_dot(h_ref[...], wg_ref[...]))
    xn = x_ref[...] + _rms(val, gpost_ref[...])
    xo_ref[...] = xn
    ho_ref[...] = _rms(xn, gnext_ref[...]).astype(ho_ref.dtype)


def ple_norm_res(p, h, w_ple, w_gate, x, g_post, g_next, tm=256):
    m, d = x.shape
    pd = p.shape[1]
    row = lambda i: (i, 0)
    fix = lambda i: (0, 0)
    return pl.pallas_call(
        _ple_kernel,
        out_shape=(jax.ShapeDtypeStruct((m, d), F32), jax.ShapeDtypeStruct((m, d), BF16)),
        grid=(m // tm,),
        in_specs=[
            pl.BlockSpec((tm, pd), row), pl.BlockSpec((tm, d), row),
            pl.BlockSpec((pd, d), fix), pl.BlockSpec((d, d), fix),
            pl.BlockSpec((tm, d), row), pl.BlockSpec((1, d), fix), pl.BlockSpec((1, d), fix),
        ],
        out_specs=(pl.BlockSpec((tm, d), row), pl.BlockSpec((tm, d), row)),
        compiler_params=_cparams("parallel"),
        name="ple_norm_res",
    )(p, h, w_ple, w_gate, x, g_post.reshape(1, d), g_next.reshape(1, d))


def _merge_kernel(ya_ref, yb_ref, yc_ref, yd_ref, ga_ref, gb_ref, gc_ref, gd_ref,
                  wa_ref, wb_ref, wc_ref, wd_ref, o_ref):
    acc = ga_ref[...].astype(F32) * _dot(ya_ref[...], wa_ref[...])
    acc += gb_ref[...].astype(F32) * _dot(yb_ref[...], wb_ref[...])
    acc += gc_ref[...].astype(F32) * _dot(yc_ref[...], wc_ref[...])
    acc += gd_ref[...].astype(F32) * _dot(yd_ref[...], wd_ref[...])
    o_ref[...] = acc.astype(o_ref.dtype)


def merge_branches(ys, gates, ws, tm=1024, tn=512):
    m = ys[0].shape[0]
    tm = min(tm, m)
    d = ws[0].shape[1]
    nb = d // tn
    in_specs = [pl.BlockSpec((tm, y.shape[1]), lambda i, j: (i, 0)) for y in ys]
    in_specs += [pl.BlockSpec((tm, tn), functools.partial(lambda i, j, b: (i, b * nb + j), b=b)) for b in range(4)]
    in_specs += [pl.BlockSpec((w.shape[0], tn), lambda i, j: (0, j)) for w in ws]
    return pl.pallas_call(
        _merge_kernel,
        out_shape=jax.ShapeDtypeStruct((m, d), BF16),
        grid=(m // tm, nb),
        in_specs=in_specs,
        out_specs=pl.BlockSpec((tm, tn), lambda i, j: (i, j)),
        compiler_params=_cparams("parallel", "arbitrary"),
        name="merge_branches",
    )(*ys, gates, gates, gates, gates, *ws)


def _ffn_up_kernel(h_ref, wg_ref, wu_ref, cwg_ref, cwu_ref, cbg_ref, cbu_ref, o_ref, bufg_ref, bufu_ref,
                   *, tm, seq_blocks):
    i = pl.program_id(1)

    @pl.when(i % seq_blocks == 0)
    def _():
        bufg_ref[0:SUBLANES, :] = jnp.zeros((SUBLANES, bufg_ref.shape[1]), F32)
        bufu_ref[0:SUBLANES, :] = jnp.zeros((SUBLANES, bufu_ref.shape[1]), F32)

    h = h_ref[...]

    def conv(w_ref, cw_ref, cb_ref, buf_ref):
        buf_ref[SUBLANES:SUBLANES + tm, :] = _dot(h, w_ref[...])
        cw = cw_ref[...]
        out = cb_ref[...] + cw[2:3, :] * buf_ref[SUBLANES:SUBLANES + tm, :]
        out += cw[1:2, :] * buf_ref[SUBLANES - 1:SUBLANES - 1 + tm, :]
        out += cw[0:1, :] * buf_ref[SUBLANES - 2:SUBLANES - 2 + tm, :]
        buf_ref[0:SUBLANES, :] = buf_ref[tm:tm + SUBLANES, :]
        return out

    g = conv(wg_ref, cwg_ref, cbg_ref, bufg_ref)
    u = conv(wu_ref, cwu_ref, cbu_ref, bufu_ref)
    o_ref[...] = (_gelu(g) * u).astype(o_ref.dtype)


def ffn_up(h, w_up, conv_w, conv_b, seq, tm=1024, tn=512):
    m, d = h.shape
    tm = min(tm, seq)
    dff = w_up.shape[1] // 2
    nb = dff // tn
    cb = conv_b.reshape(1, 2 * dff)
    return pl.pallas_call(
        functools.partial(_ffn_up_kernel, tm=tm, seq_blocks=seq // tm),
        out_shape=jax.ShapeDtypeStruct((m, dff), BF16),
        grid=(nb, m // tm),
        in_specs=[
            pl.BlockSpec((tm, d), lambda j, i: (i, 0)),
            pl.BlockSpec((d, tn), lambda j, i: (0, j)),
            pl.BlockSpec((d, tn), lambda j, i: (0, j + nb)),
            pl.BlockSpec((FFN_CONV, tn), lambda j, i: (0, j)),
            pl.BlockSpec((FFN_CONV, tn), lambda j, i: (0, j + nb)),
            pl.BlockSpec((1, tn), lambda j, i: (0, j)),
            pl.BlockSpec((1, tn), lambda j, i: (0, j + nb)),
        ],
        out_specs=pl.BlockSpec((tm, tn), lambda j, i: (i, j)),
        scratch_shapes=[pltpu.VMEM((tm + SUBLANES, tn), F32), pltpu.VMEM((tm + SUBLANES, tn), F32)],
        compiler_params=_cparams("parallel", "arbitrary"),
        name="ffn_up_conv_glu",
    )(h, w_up, w_up, conv_w, conv_w, cb, cb)


def _lru_kernel(x_ref, gate_ref, cw_ref, cb_ref, wr_ref, br_ref, wi_ref, bi_ref, lam_ref, o_ref,
                xbuf_ref, a0_ref, h0_ref, a1_ref, h1_ref, *, seq, pad):
    xbuf_ref[0:SUBLANES, :] = jnp.zeros((SUBLANES, LANES), F32)
    xbuf_ref[SUBLANES:SUBLANES + seq, :] = x_ref[0]
    cw = cw_ref[...]
    u = cb_ref[...] + cw[3:4, :] * xbuf_ref[SUBLANES:SUBLANES + seq, :]
    for k in range(LRU_CONV - 1):
        off = SUBLANES - (LRU_CONV - 1) + k
        u += cw[k:k + 1, :] * xbuf_ref[off:off + seq, :]
    ub = u.astype(BF16)
    r = jax.nn.sigmoid(_dot(ub, wr_ref[0]) + br_ref[...])
    ig = jax.nn.sigmoid(_dot(ub, wi_ref[0]) + bi_ref[...])
    log_a = -LRU_C * r * _softplus(-lam_ref[...])
    a = jnp.exp(log_a)
    inp = jnp.sqrt(1.0 - jnp.exp(2.0 * log_a)) * ig * u

    ones = jnp.ones((pad, LANES), F32)
    zeros = jnp.zeros((pad, LANES), F32)
    a0_ref[0:pad, :] = ones
    a1_ref[0:pad, :] = ones
    h0_ref[0:pad, :] = zeros
    h1_ref[0:pad, :] = zeros
    a0_ref[pad:pad + seq, :] = a
    h0_ref[pad:pad + seq, :] = inp
    bufs = ((a0_ref, h0_ref), (a1_ref, h1_ref))
    d = 1
    level = 0
    while d < seq:
        (a_src, h_src), (a_dst, h_dst) = bufs[level % 2], bufs[(level + 1) % 2]
        a_cur = a_src[pad:pad + seq, :]
        h_dst[pad:pad + seq, :] = h_src[pad:pad + seq, :] + a_cur * h_src[pad - d:pad - d + seq, :]
        a_dst[pad:pad + seq, :] = a_cur * a_src[pad - d:pad - d + seq, :]
        d *= 2
        level += 1
    h = bufs[level % 2][1][pad:pad + seq, :]
    o_ref[0] = (h * _gelu(gate_ref[0])).astype(o_ref.dtype)


def rglru(xg, conv_w, conv_b, w_r, b_r, w_i, b_i, lam):
    b, s, w2 = xg.shape
    w = w2 // 2
    nblk = w // LANES
    pad = s
    vec = lambda v: v.reshape(1, w)
    vspec = pl.BlockSpec((1, LANES), lambda bi, c: (0, c))
    return pl.pallas_call(
        functools.partial(_lru_kernel, seq=s, pad=pad),
        out_shape=jax.ShapeDtypeStruct((b, s, w), BF16),
        grid=(b, nblk),
        in_specs=[
            pl.BlockSpec((1, s, LANES), lambda bi, c: (bi, 0, c)),
            pl.BlockSpec((1, s, LANES), lambda bi, c: (bi, 0, c + nblk)),
            pl.BlockSpec((LRU_CONV, LANES), lambda bi, c: (0, c)),
            vspec,
            pl.BlockSpec((1, LANES, LANES), lambda bi, c: (c, 0, 0)),
            vspec,
            pl.BlockSpec((1, LANES, LANES), lambda bi, c: (c, 0, 0)),
            vspec, vspec,
        ],
        out_specs=pl.BlockSpec((1, s, LANES), lambda bi, c: (bi, 0, c)),
        scratch_shapes=[pltpu.VMEM((s + SUBLANES, LANES), F32)] + [pltpu.VMEM((pad + s, LANES), F32)] * 4,
        compiler_params=_cparams("parallel", "parallel"),
        name="rglru",
    )(xg, xg, conv_w, vec(conv_b), w_r, vec(b_r), w_i, vec(b_i), vec(lam))


def _dil_kernel(q_ref, k_ref, v_ref, o_ref, acc_ref, m_ref, l_ref):
    n = pl.program_id(1)
    heads = range(DIL_HEADS)
    scale = HEAD_DIM ** -0.5
    diff = (lax.broadcasted_iota(jnp.int32, (BLOCK, BLOCK), 0)
            - lax.broadcasted_iota(jnp.int32, (BLOCK, BLOCK), 1))

    def bias(residue_mask, side):
        ok = (diff & residue_mask) == 0
        if side > 0:
            ok = ok & (diff >= 0)
        elif side < 0:
            ok = ok & (diff <= 0)
        return jnp.where(ok, 0.0, NEG_INF)

    def col(g, h):
        c = (g * DIL_HEADS + h) * HEAD_DIM
        return slice(c, c + HEAD_DIM)

    def hcol(h):
        return slice(h * HEAD_DIM, (h + 1) * HEAD_DIM)

    def phase(blocks, first=False):
        starts = [pl.multiple_of(jb * BLOCK, BLOCK) for _, jb, _ in blocks]
        s = [[_dot_nt(q_ref[:, col(g, h)], k_ref[pl.ds(st, BLOCK), col(g, h)]) * scale + bz
              for h in heads] for (g, _, bz), st in zip(blocks, starts)]
        for h in heads:
            bm = functools.reduce(jnp.maximum, [jnp.max(sb[h], axis=-1, keepdims=True) for sb in s])
            m_new = bm if first else jnp.maximum(m_ref[h], bm)
            ps = [jnp.exp(sb[h] - m_new) for sb in s]
            lsum = functools.reduce(jnp.add, [jnp.sum(p, axis=-1, keepdims=True) for p in ps])
            pv = functools.reduce(jnp.add, [_dot(p.astype(BF16), v_ref[pl.ds(st, BLOCK), col(g, h)])
                                            for p, (g, _, _), st in zip(ps, blocks, starts)])
            if first:
                l_ref[h] = lsum
                acc_ref[:, hcol(h)] = pv
            else:
                alpha = jnp.exp(m_ref[h] - m_new)
                l_ref[h] = alpha * l_ref[h] + lsum
                acc_ref[:, hcol(h)] = alpha * acc_ref[:, hcol(h)] + pv
            m_ref[h] = m_new

    (w1, d1), (w2, d2), (_, d3) = DIL_CONFIGS
    far1, far2 = w1 // BLOCK, w2 // BLOCK
    phase([(0, n, bias(d1 - 1, 1)), (1, n, bias(d2 - 1, 1)), (2, n, bias(d3 - 1, 1))], first=True)

    @pl.when(n >= 1)
    def _():
        phase([(0, n - far1, bias(d1 - 1, -1)), (1, n - 1, bias(d2 - 1, 0))])

    for back in range(2, far2):
        @pl.when(n >= back)
        def _():
            phase([(1, n - back, bias(d2 - 1, 0))])

    @pl.when(n >= far2)
    def _():
        phase([(1, n - far2, bias(d2 - 1, -1))])

    mid3 = bias(d3 - 1, 0)

    def body(t, carry):
        phase([(2, n - 1 - 2 * t, mid3), (2, n - 2 - 2 * t, mid3)])
        return carry

    lax.fori_loop(0, n // 2, body, 0)

    @pl.when(n % 2 == 1)
    def _():
        phase([(2, 0, mid3)])

    for h in heads:
        o_ref[:, hcol(h)] = (acc_ref[:, hcol(h)] / l_ref[h]).astype(o_ref.dtype)


def dilated_attention(qkv, batch, seq):
    assert DIL_CONFIGS[0] == (BLOCK, 1) and DIL_CONFIGS[2][0] // DIL_CONFIGS[2][1] == BLOCK
    assert seq <= BLOCK * DIL_CONFIGS[2][1]
    t, width = qkv.shape
    w = width // 3
    nq = seq // BLOCK
    out_w = DIL_HEADS * HEAD_DIM
    return pl.pallas_call(
        _dil_kernel,
        out_shape=jax.ShapeDtypeStruct((t, out_w), BF16),
        grid=(batch, nq),
        in_specs=[
            pl.BlockSpec((BLOCK, w), lambda b, n: (b * nq + n, 0)),
            pl.BlockSpec((seq, w), lambda b, n: (b, 1)),
            pl.BlockSpec((seq, w), lambda b, n: (b, 2)),
        ],
        out_specs=pl.BlockSpec((BLOCK, out_w), lambda b, n: (b * nq + n, 0)),
        scratch_shapes=[pltpu.VMEM((BLOCK, out_w), F32), pltpu.VMEM((DIL_HEADS, BLOCK, 1), F32),
                        pltpu.VMEM((DIL_HEADS, BLOCK, 1), F32)],
        compiler_params=_cparams("parallel", "arbitrary"),
        name="dilated_attention",
    )(qkv, qkv, qkv)


def _sb_kernel(q_ref, k_ref, v_ref, o_ref, acc_ref, run_ref):
    n = pl.program_id(1)
    nh = q_ref.shape[1] // HEAD_DIM
    heads = range(nh)
    sls = [slice(h * HEAD_DIM, (h + 1) * HEAD_DIM) for h in heads]
    scale = HEAD_DIM ** -0.5
    row = lax.broadcasted_iota(jnp.int32, (BLOCK, BLOCK), 0)
    col = lax.broadcasted_iota(jnp.int32, (BLOCK, BLOCK), 1)
    later_ones = jnp.concatenate([(row > col).astype(BF16), jnp.ones((BLOCK, BLOCK), BF16)], axis=1)
    strict = col < row
    q = [q_ref[:, sl] for sl in sls]

    def block(j, diag):
        start = pl.multiple_of(j * BLOCK, BLOCK)
        z = [_dot_nt(q[h], k_ref[pl.ds(start, BLOCK), sls[h]]) * scale for h in heads]
        log_beta = [jnp.minimum(zz, 0.0) - jnp.log(1.0 + jnp.exp(-jnp.abs(zz))) for zz in z]
        log_1m = [lb - zz for lb, zz in zip(log_beta, z)]
        if diag:
            log_1m = [jnp.where(strict, x, 0.0) for x in log_1m]
        sums = [_dot_exact_rhs(x, later_ones) for x in log_1m]
        for h in heads:
            prev = jnp.zeros((BLOCK, BLOCK), F32) if diag else run_ref[h]
            att = jnp.exp(log_beta[h] + sums[h][:, :BLOCK] + prev)
            if diag:
                att = jnp.where(strict, att, 0.0)
            pv = _dot(att.astype(BF16), v_ref[pl.ds(start, BLOCK), sls[h]])
            if diag:
                acc_ref[:, sls[h]] = pv
                run_ref[h] = sums[h][:, BLOCK:]
            else:
                acc_ref[:, sls[h]] += pv
                run_ref[h] = prev + sums[h][:, BLOCK:]

    block(n, True)

    def body(t, carry):
        block(n - 1 - t, False)
        return carry

    lax.fori_loop(0, n, body, 0)
    o_ref[...] = acc_ref[...].astype(o_ref.dtype)


def stick_breaking(qkv, batch, seq):
    t, width = qkv.shape
    w = width // 3
    nq = seq // BLOCK
    return pl.pallas_call(
        _sb_kernel,
        out_shape=jax.ShapeDtypeStruct((t, w), BF16),
        grid=(batch, nq),
        in_specs=[
            pl.BlockSpec((BLOCK, w), lambda b, n: (b * nq + n, 0)),
            pl.BlockSpec((seq, w), lambda b, n: (b, 1)),
            pl.BlockSpec((seq, w), lambda b, n: (b, 2)),
        ],
        out_specs=pl.BlockSpec((BLOCK, w), lambda b, n: (b * nq + n, 0)),
        scratch_shapes=[pltpu.VMEM((BLOCK, w), F32), pltpu.VMEM((w // HEAD_DIM, BLOCK, BLOCK), F32)],
        compiler_params=_cparams("parallel", "arbitrary"),
        name="stick_breaking",
    )(qkv, qkv, qkv)


def _head_sum(x, bd):
    cols = []
    for c in range(x.shape[1] // LANES):
        cols.append(_dot_exact_rhs(x[:, c * LANES:(c + 1) * LANES], bd))
    return jnp.concatenate(cols, axis=1)


def _rwkv_prep_kernel(*refs, tm, width, seq_blocks, has_vres):
    if has_vres:
        (seg_ref, prev_ref, mu_ref, w0_ref, wup_ref, a0_ref, aup_ref, gup_ref, kk_ref, ka_ref, rk_ref,
         vf_ref, v0_ref, vdn_ref, vup_ref,
         r_o, k_o, v_o, lw_o, kk_o, b_o, g_o, bon_o, buf_ref) = refs
    else:
        (seg_ref, prev_ref, mu_ref, w0_ref, wup_ref, a0_ref, aup_ref, gup_ref, kk_ref, ka_ref, rk_ref,
         r_o, k_o, v_o, lw_o, kk_o, b_o, g_o, bon_o, buf_ref) = refs
    i = pl.program_id(0)
    seg = seg_ref[...]
    buf_ref[0:SUBLANES, :] = prev_ref[...]

    @pl.when(i % seq_blocks == 0)
    def _():
        buf_ref[0:SUBLANES, :] = jnp.zeros((SUBLANES, buf_ref.shape[1]), F32)

    buf_ref[SUBLANES:SUBLANES + tm, :] = seg
    shifted = buf_ref[SUBLANES - 1:SUBLANES - 1 + tm, :]
    xs = seg + (shifted - seg) * mu_ref[...]
    w = width
    r = xs[:, 0:w]
    k = xs[:, w:2 * w]
    v = xs[:, 2 * w:3 * w]
    low = xs[:, 3 * w:3 * w + LANES]
    g_low = xs[:, 3 * w + LANES:3 * w + 3 * LANES]
    wpre = w0_ref[...] + _dot(jnp.tanh(low).astype(BF16), wup_ref[...])
    wlog = -_softplus(-wpre) - 0.5
    lw_o[...] = -jnp.exp(wlog)
    a = jax.nn.sigmoid(a0_ref[...] + _dot(low.astype(BF16), aup_ref[...]))
    g_o[...] = _dot(jax.nn.sigmoid(g_low).astype(BF16), gup_ref[...])
    if has_vres:
        mix = jax.nn.sigmoid(v0_ref[...] + _dot(_dot(v.astype(BF16), vdn_ref[...]).astype(BF16), vup_ref[...]))
        v = v + (vf_ref[...] - v) * mix
    row = lax.broadcasted_iota(jnp.int32, (LANES, LANES), 0) // RWKV_HEAD
    col = lax.broadcasted_iota(jnp.int32, (LANES, LANES), 1) // RWKV_HEAD
    bd = (row == col).astype(BF16)
    kk = k * kk_ref[...]
    norm = jnp.sqrt(_head_sum(kk * kk, bd))
    kk = kk / jnp.maximum(norm, 1e-12)
    k2 = k * (1.0 + (a - 1.0) * ka_ref[...])
    bonus = _head_sum(r * k2 * rk_ref[...], bd) * v
    r_o[...] = r
    k_o[...] = k2
    v_o[...] = v
    kk_o[...] = kk
    b_o[...] = kk * a
    bon_o[...] = bonus


def rwkv_prep(seg, seq, mu, w0, w_up, a0, a_up, g_up, k_k, k_a, r_k, v_first, v_res, tm=256):
    t, wpad = seg.shape
    w = w0.shape[0]
    has_vres = v_res is not None
    row = lambda i: (i, 0)
    fix = lambda i: (0, 0)
    vec = lambda x: x.reshape(1, -1)
    step = tm // SUBLANES
    in_specs = [
        pl.BlockSpec((tm, wpad), row),
        pl.BlockSpec((SUBLANES, wpad), lambda i: (jnp.maximum(i * step - 1, 0), 0)),
        pl.BlockSpec((1, wpad), fix), pl.BlockSpec((1, w), fix), pl.BlockSpec((LANES, w), fix),
        pl.BlockSpec((1, w), fix), pl.BlockSpec((LANES, w), fix), pl.BlockSpec((2 * LANES, w), fix),
        pl.BlockSpec((1, w), fix), pl.BlockSpec((1, w), fix), pl.BlockSpec((1, w), fix),
    ]
    args = [seg, seg, vec(mu), vec(w0), w_up, vec(a0), a_up, g_up, vec(k_k), vec(k_a), vec(r_k)]
    if has_vres:
        v0, v_down, v_up = v_res
        in_specs += [pl.BlockSpec((tm, w), row), pl.BlockSpec((1, w), fix),
                     pl.BlockSpec((w, LANES), fix), pl.BlockSpec((LANES, w), fix)]
        args += [v_first, vec(v0), v_down, v_up]
    out = jax.ShapeDtypeStruct((t, w), F32)
    return pl.pallas_call(
        functools.partial(_rwkv_prep_kernel, tm=tm, width=w, seq_blocks=seq // tm, has_vres=has_vres),
        out_shape=(out,) * 8,
        grid=(t // tm,),
        in_specs=in_specs,
        out_specs=(pl.BlockSpec((tm, w), row),) * 8,
        scratch_shapes=[pltpu.VMEM((tm + SUBLANES, wpad), F32)],
        compiler_params=_cparams("parallel"),
        name="rwkv_prep",
    )(*args)


def _rwkv_chunk_kernel(r_ref, k_ref, v_ref, lw_ref, kk_ref, b_ref, g_ref, bon_ref, gnw_ref, gnb_ref, o_ref,
                       state_ref):
    c = pl.program_id(1)
    n = RWKV_CHUNK
    n2 = 2 * n

    @pl.when(c == 0)
    def _():
        state_ref[...] = jnp.zeros_like(state_ref)

    tri = (lax.broadcasted_iota(jnp.int32, (n, n), 1) <= lax.broadcasted_iota(jnp.int32, (n, n), 0)).astype(BF16)
    row2 = lax.broadcasted_iota(jnp.int32, (n2, n2), 0)
    col2 = lax.broadcasted_iota(jnp.int32, (n2, n2), 1)
    t2 = row2 & (n - 1)
    s2 = col2 & (n - 1)
    strict = s2 < t2
    incl = s2 <= t2
    eye = (row2 == col2).astype(F32)
    head0 = lax.broadcasted_iota(jnp.int32, (1, LANES), 1) < RWKV_HEAD
    own = jnp.concatenate([jnp.broadcast_to(head0, (n, LANES)), jnp.broadcast_to(~head0, (n, LANES))], axis=0)

    def stack(x):
        return jnp.where(own, jnp.concatenate([x, x], axis=0), 0.0)

    pairs = range(r_ref.shape[1] // LANES)
    sls = [slice(hp * LANES, (hp + 1) * LANES) for hp in pairs]
    lw = lw_ref[...]
    gsum = _dot_exact_rhs_left(tri, lw)
    p_inv = jnp.exp(-gsum)
    p_end = jnp.exp(gsum[n - 1:n, :])
    rt_f = r_ref[...] * jnp.exp(gsum)
    kt_f = k_ref[...] * p_inv
    bt_f = b_ref[...] * p_inv
    kap_f = kk_ref[...] * jnp.exp(gsum - lw)
    v_f = v_ref[...]
    rt = [stack(rt_f[:, sl]).astype(BF16) for sl in sls]
    kt = [stack(kt_f[:, sl]) for sl in sls]
    bt = [stack(bt_f[:, sl]) for sl in sls]
    kap = [stack(kap_f[:, sl]).astype(BF16) for sl in sls]
    v2 = [stack(v_f[:, sl]).astype(BF16) for sl in sls]
    prod = [_dot_nt(jnp.concatenate([kap[hp], rt[hp]], axis=0),
                    jnp.concatenate([bt[hp].astype(BF16), kt[hp].astype(BF16)], axis=0)) for hp in pairs]
    a_ab = [jnp.where(strict, prod[hp][:n2, :n2], 0.0) for hp in pairs]
    a_ak = [jnp.where(strict, prod[hp][:n2, n2:], 0.0).astype(BF16) for hp in pairs]
    a_r = [jnp.concatenate([jnp.where(incl, prod[hp][n2:, n2:], 0.0).astype(BF16),
                            jnp.where(incl, -prod[hp][n2:, :n2], 0.0).astype(BF16)], axis=1) for hp in pairs]
    x = [eye - a_ab[hp] for hp in pairs]
    q = [_dot(a_ab[hp].astype(BF16), a_ab[hp].astype(BF16)) for hp in pairs]
    steps = 1
    while True:
        x = [x[hp] + _dot(x[hp].astype(BF16), q[hp].astype(BF16)) for hp in pairs]
        steps *= 2
        if steps * 2 >= n:
            break
        q = [_dot(q[hp].astype(BF16), q[hp].astype(BF16)) for hp in pairs]
    s0 = [state_ref[hp] for hp in pairs]
    s0b = [s.astype(BF16) for s in s0]
    rhs = [_dot_nt(kap[hp], s0b[hp]) + _dot(a_ak[hp], v2[hp]) for hp in pairs]
    u2 = [_dot(x[hp].astype(BF16), rhs[hp].astype(BF16)).astype(BF16) for hp in pairs]
    vu = [jnp.concatenate([v2[hp], u2[hp]], axis=0) for hp in pairs]
    y2 = [_dot_nt(rt[hp], s0b[hp]) + _dot(a_r[hp], vu[hp]) for hp in pairs]
    inv_n = 1.0 / RWKV_HEAD
    outs = []
    for hp in pairs:
        pe = p_end[:, sls[hp]]
        kb_end = jnp.concatenate([(kt[hp] * pe).astype(BF16), (-(bt[hp] * pe)).astype(BF16)], axis=0)
        state_ref[hp] = s0[hp] * pe + _dot_tn(vu[hp], kb_end)
        mean = jnp.sum(y2[hp], axis=-1, keepdims=True) * inv_n
        cen = jnp.where(own, y2[hp] - mean, 0.0)
        var = jnp.sum(cen * cen, axis=-1, keepdims=True) * inv_n
        yn2 = cen * lax.rsqrt(var + RWKV_GN_EPS)
        outs.append(yn2[:n] + yn2[n:])
    yn = jnp.concatenate(outs, axis=1) * gnw_ref[...] + gnb_ref[...]
    o_ref[...] = ((yn + bon_ref[...]) * g_ref[...]).astype(o_ref.dtype)


def _dot_exact_rhs_left(m_bf16, x):
    hi, lo = _split_bf16(x)
    return _dot(m_bf16, hi) + _dot(m_bf16, lo)


def rwkv_chunks(r, k, v, lw, kk, b, g, bon, gn_w, gn_b, batch, seq):
    t, w = r.shape
    nc = seq // RWKV_CHUNK
    spec = pl.BlockSpec((RWKV_CHUNK, w), lambda bi, c: (bi * nc + c, 0))
    vspec = pl.BlockSpec((1, w), lambda bi, c: (0, 0))
    return pl.pallas_call(
        _rwkv_chunk_kernel,
        out_shape=jax.ShapeDtypeStruct((t, w), BF16),
        grid=(batch, nc),
        in_specs=[spec] * 8 + [vspec, vspec],
        out_specs=spec,
        scratch_shapes=[pltpu.VMEM((w // LANES, LANES, LANES), F32)],
        compiler_params=_cparams("parallel", "arbitrary"),
        name="rwkv_chunks",
    )(r, k, v, lw, kk, b, g, bon, gn_w.reshape(1, w), gn_b.reshape(1, w))


def _pad_rows(w, rows):
    return jnp.pad(w, ((0, rows - w.shape[0]), (0, 0)))


def _pad_cols(w, cols):
    return jnp.pad(w, ((0, 0), (0, cols - w.shape[1])))


def _rope_tables(seq):
    half = HEAD_DIM // 2
    inv_freq = ROPE_THETA ** (-jnp.arange(half, dtype=F32) / half)
    ang = jnp.arange(seq, dtype=F32)[:, None] * inv_freq[None, :]
    cos = jnp.cos(ang)
    sin = jnp.sin(ang)
    return jnp.concatenate([cos, cos], axis=1), jnp.concatenate([-sin, sin], axis=1)


def kernel(x, p, norm_mix_pre, norm_mix_post, norm_ffn_pre, norm_ffn_post, norm_ple_pre, norm_ple_post, w_in, w_merge_gate, lru_conv_w, lru_conv_b, lru_w_r, lru_b_r, lru_w_i, lru_b_i, lru_lambda, rwkv_mu, rwkv_w0, rwkv_w_up, rwkv_a0, rwkv_a_up, rwkv_g_up, rwkv_k_k, rwkv_k_a, rwkv_r_k, rwkv_gn_w, rwkv_gn_b, rwkv_v0, rwkv_v_down, rwkv_v_up, w_branch_a, w_branch_b, w_branch_c, w_branch_d, w_out, w_ffn_up, ffn_conv_w, ffn_conv_b, w_ffn_down, w_ple, w_ple_gate):
    batch, seq, d = x.shape
    depth = w_in.shape[0]
    t = batch * seq
    lru_w = lru_conv_w.shape[2]
    rw = rwkv_w0.shape[1]
    dil_w = 3 * len(DIL_CONFIGS) * DIL_HEADS * HEAD_DIM
    sb_w = 3 * (d // 2)
    off_b = 2 * lru_w
    off_c = off_b + dil_w
    off_d = off_c + sb_w
    rwkv_in = w_in.shape[2] - off_d
    rwkv_pad = 3 * rw + 4 * LANES
    cos, sin = _rope_tables(seq)

    xf = x.reshape(t, d)
    h = rmsnorm_bf16(xf, norm_mix_pre[0])
    v_first = None
    for i in range(depth):
        wi = w_in[i]
        w_a = wi[:, :off_b].astype(BF16)
        w_b = wi[:, off_b:off_c].astype(BF16)
        w_c = wi[:, off_c:off_d].astype(BF16)
        w_d = _pad_cols(wi[:, off_d:], rwkv_pad).astype(BF16)
        seg_a = matmul(h, w_a, F32)
        seg_b = matmul_rope(h, w_b, cos, sin, seq, 2 * dil_w // 3, BF16)
        seg_c = matmul(h, w_c, BF16)
        seg_d = matmul(h, w_d, F32, tm=512, tn=rwkv_pad // 2)
        gates = matmul(h, w_merge_gate[i].astype(BF16), BF16, act="sigmoid")
        y_a = rglru(seg_a.reshape(batch, seq, off_b), lru_conv_w[i], lru_conv_b[i], lru_w_r[i].astype(BF16),
                    lru_b_r[i], lru_w_i[i].astype(BF16), lru_b_i[i], lru_lambda[i]).reshape(t, lru_w)
        y_b = dilated_attention(seg_b, batch, seq)
        y_c = stick_breaking(seg_c, batch, seq)
        mu = jnp.pad(rwkv_mu[i], (0, rwkv_pad - rwkv_in))
        w_up = _pad_rows(rwkv_w_up[i], LANES).astype(BF16)
        a_up = jnp.pad(rwkv_a_up[i], ((RWKV_W_LORA, LANES - RWKV_W_LORA - RWKV_A_LORA), (0, 0))).astype(BF16)
        g_up = _pad_rows(rwkv_g_up[i], 2 * LANES).astype(BF16)
        v_res = None
        if i > 0:
            v_res = (rwkv_v0[i - 1], _pad_cols(rwkv_v_down[i - 1], LANES).astype(BF16),
                     _pad_rows(rwkv_v_up[i - 1], LANES).astype(BF16))
        r_, k_, v_, lw_, kk_, b_, g_, bon_ = rwkv_prep(
            seg_d, seq, mu, rwkv_w0[i], w_up, rwkv_a0[i], a_up, g_up, rwkv_k_k[i], rwkv_k_a[i],
            rwkv_r_k[i].reshape(-1), v_first, v_res)
        if i == 0:
            v_first = v_
        y_d = rwkv_chunks(r_, k_, v_, lw_, kk_, b_, g_, bon_, rwkv_gn_w[i], rwkv_gn_b[i], batch, seq)
        merged = merge_branches(
            (y_a, y_b, y_c, y_d), gates,
            (w_branch_a[i].astype(BF16), w_branch_b[i].astype(BF16), w_branch_c[i].astype(BF16),
             w_branch_d[i].astype(BF16)))
        xf, h = matmul_norm_res(merged, w_out[i].astype(BF16), xf, norm_mix_post[i], norm_ffn_pre[i], tk=d)
        act = ffn_up(h, w_ffn_up[i].astype(BF16), ffn_conv_w[i], ffn_conv_b[i], seq)
        xf, h = matmul_norm_res(act, w_ffn_down[i].astype(BF16), xf, norm_ffn_post[i], norm_ple_pre[i],
                                tm=512, tk=w_ffn_down.shape[1] // 4)
        g_next = norm_mix_pre[i + 1] if i + 1 < depth else norm_mix_pre[i]
        xf, h = ple_norm_res(p[i].reshape(t, -1).astype(BF16), h, w_ple[i].astype(BF16),
                             w_ple_gate[i].astype(BF16), xf, norm_ple_post[i], g_next)
    return xf.reshape(batch, seq, d)
```

```python
import functools

import jax
import jax.numpy as jnp
from jax import lax
from jax.experimental import pallas as pl
from jax.experimental.pallas import tpu as pltpu

F32 = jnp.float32
BF16 = jnp.bfloat16

LANES = 128
SUBLANES = 8
VMEM_LIMIT_BYTES = 52 * 1024 * 1024

HEAD_DIM = 128
BLOCK = 128
ROPE_THETA = 10000.0
RMS_EPS = 1e-6
NEG_INF = -1e30

LRU_BLOCKS = 8
LRU_CONV = 4
LRU_C = 8.0
DIL_CONFIGS = ((128, 1), (512, 4), (2048, 16))
DIL_HEADS = 4
RWKV_HEAD = 64
RWKV_W_LORA = 64
RWKV_A_LORA = 64
RWKV_G_LORA = 160
RWKV_V_LORA = 32
RWKV_GN_EPS = 64e-5
RWKV_CHUNK = 64
FFN_CONV = 3


def _cparams(*sem):
    return pltpu.CompilerParams(dimension_semantics=sem, vmem_limit_bytes=VMEM_LIMIT_BYTES)


def _dot(a, b):
    return jnp.dot(a, b, preferred_element_type=F32)


def _dot_nt(a, b):
    return lax.dot_general(a, b, (((1,), (1,)), ((), ())), preferred_element_type=F32)


def _dot_tn(a, b):
    return lax.dot_general(a, b, (((0,), (0,)), ((), ())), preferred_element_type=F32)


def _split_bf16(x):
    hi = x.astype(BF16)
    lo = (x - hi.astype(F32)).astype(BF16)
    return hi, lo


def _dot_exact_rhs(x, m_bf16):
    hi, lo = _split_bf16(x)
    return _dot(hi, m_bf16) + _dot(lo, m_bf16)


def _rms(x, g):
    return x * lax.rsqrt(jnp.mean(x * x, axis=-1, keepdims=True) + RMS_EPS) * g


def _gelu(x):
    return jax.nn.gelu(x, approximate=True)


def _softplus(x):
    return jnp.maximum(x, 0.0) + jnp.log1p(jnp.exp(-jnp.abs(x)))


def _rmsnorm_kernel(x_ref, g_ref, o_ref):
    o_ref[...] = _rms(x_ref[...], g_ref[...]).astype(o_ref.dtype)


def rmsnorm_bf16(x, g, tm=512):
    t, d = x.shape
    return pl.pallas_call(
        _rmsnorm_kernel,
        out_shape=jax.ShapeDtypeStruct((t, d), BF16),
        grid=(t // tm,),
        in_specs=[pl.BlockSpec((tm, d), lambda i: (i, 0)), pl.BlockSpec((1, d), lambda i: (0, 0))],
        out_specs=pl.BlockSpec((tm, d), lambda i: (i, 0)),
        compiler_params=_cparams("parallel"),
        name="rmsnorm",
    )(x, g.reshape(1, d))


def _mm_kernel(a_ref, w_ref, o_ref):
    o_ref[...] = _dot(a_ref[...], w_ref[...]).astype(o_ref.dtype)


def _mm_rope_kernel(a_ref, w_ref, cos_ref, sin_ref, o_ref, *, n_rope_blocks, tn):
    j = pl.program_id(1)
    acc = _dot(a_ref[...], w_ref[...])

    @pl.when(j < n_rope_blocks)
    def _():
        cos = cos_ref[...]
        sin = sin_ref[...]
        for c in range(tn // HEAD_DIM):
            seg = acc[:, c * HEAD_DIM:(c + 1) * HEAD_DIM]
            rot = pltpu.roll(seg, HEAD_DIM // 2, axis=1)
            o_ref[:, c * HEAD_DIM:(c + 1) * HEAD_DIM] = (seg * cos + rot * sin).astype(o_ref.dtype)

    @pl.when(j >= n_rope_blocks)
    def _():
        o_ref[...] = acc.astype(o_ref.dtype)


def matmul(a, w, out_dtype, tm=1024, tn=1024):
    m, k = a.shape
    n = w.shape[1]
    tm, tn = min(tm, m), min(tn, n)
    return pl.pallas_call(
        _mm_kernel,
        out_shape=jax.ShapeDtypeStruct((m, n), out_dtype),
        grid=(m // tm, n // tn),
        in_specs=[pl.BlockSpec((tm, k), lambda i, j: (i, 0)), pl.BlockSpec((k, tn), lambda i, j: (0, j))],
        out_specs=pl.BlockSpec((tm, tn), lambda i, j: (i, j)),
        compiler_params=_cparams("parallel", "arbitrary"),
        name="matmul_plain",
    )(a, w)


def matmul_rope(a, w, cos, sin, seq, n_rope_cols, out_dtype, tm=1024, tn=1536):
    m, k = a.shape
    n = w.shape[1]
    tm, tn = min(tm, m), min(tn, n)
    sblocks = seq // tm
    return pl.pallas_call(
        functools.partial(_mm_rope_kernel, n_rope_blocks=n_rope_cols // tn, tn=tn),
        out_shape=jax.ShapeDtypeStruct((m, n), out_dtype),
        grid=(m // tm, n // tn),
        in_specs=[
            pl.BlockSpec((tm, k), lambda i, j: (i, 0)),
            pl.BlockSpec((k, tn), lambda i, j: (0, j)),
            pl.BlockSpec((tm, HEAD_DIM), lambda i, j: (i % sblocks, 0)),
            pl.BlockSpec((tm, HEAD_DIM), lambda i, j: (i % sblocks, 0)),
        ],
        out_specs=pl.BlockSpec((tm, tn), lambda i, j: (i, j)),
        compiler_params=_cparams("parallel", "arbitrary"),
        name="matmul_rope",
    )(a, w, cos, sin)


def _mm_norm_res_kernel(a_ref, w_ref, x_ref, gpost_ref, gnext_ref, xo_ref, ho_ref, acc_ref, *, nk):
    kk = pl.program_id(1)

    @pl.when(kk == 0)
    def _():
        acc_ref[...] = jnp.zeros_like(acc_ref)

    acc_ref[...] += _dot(a_ref[...], w_ref[...])

    @pl.when(kk == nk - 1)
    def _():
        xn = x_ref[...] + _rms(acc_ref[...], gpost_ref[...])
        xo_ref[...] = xn
        ho_ref[...] = _rms(xn, gnext_ref[...]).astype(ho_ref.dtype)


def matmul_norm_res(a, w, x, g_post, g_next, tm=256, tk=512):
    m, k = a.shape
    d = w.shape[1]
    nk = k // tk
    return pl.pallas_call(
        functools.partial(_mm_norm_res_kernel, nk=nk),
        out_shape=(jax.ShapeDtypeStruct((m, d), F32), jax.ShapeDtypeStruct((m, d), BF16)),
        grid=(m // tm, nk),
        in_specs=[
            pl.BlockSpec((tm, tk), lambda i, kk: (i, kk)),
            pl.BlockSpec((tk, d), lambda i, kk: (kk, 0)),
            pl.BlockSpec((tm, d), lambda i, kk: (i, 0)),
            pl.BlockSpec((1, d), lambda i, kk: (0, 0)),
            pl.BlockSpec((1, d), lambda i, kk: (0, 0)),
        ],
        out_specs=(pl.BlockSpec((tm, d), lambda i, kk: (i, 0)), pl.BlockSpec((tm, d), lambda i, kk: (i, 0))),
        scratch_shapes=[pltpu.VMEM((tm, d), F32)],
        compiler_params=_cparams("parallel", "arbitrary"),
        name="matmul_norm_res",
    )(a, w, x, g_post.reshape(1, d), g_next.reshape(1, d))


def _ple_kernel(p_ref, h_ref, wp_ref, wg_ref, x_ref, gpost_ref, gnext_ref, xo_ref, ho_ref):
    val = _dot(p_ref[...], wp_ref[...]) * jax.nn.sigmoid(_dot(h_ref[...], wg_ref[...]))
    xn = x_ref[...] + _rms(val, gpost_ref[...])
    xo_ref[...] = xn
    ho_ref[...] = _rms(xn, gnext_ref[...]).astype(ho_ref.dtype)


def ple_norm_res(p, h, w_ple, w_gate, x, g_post, g_next, tm=256):
    m, d = x.shape
    pd = p.shape[1]
    row = lambda i: (i, 0)
    fix = lambda i: (0, 0)
    return pl.pallas_call(
        _ple_kernel,
        out_shape=(jax.ShapeDtypeStruct((m, d), F32), jax.ShapeDtypeStruct((m, d), BF16)),
        grid=(m // tm,),
        in_specs=[
            pl.BlockSpec((tm, pd), row), pl.BlockSpec((tm, d), row),
            pl.BlockSpec((pd, d), fix), pl.BlockSpec((d, d), fix),
            pl.BlockSpec((tm, d), row), pl.BlockSpec((1, d), fix), pl.BlockSpec((1, d), fix),
        ],
        out_specs=(pl.BlockSpec((tm, d), row), pl.BlockSpec((tm, d), row)),
        compiler_params=_cparams("parallel"),
        name="ple_norm_res",
    )(p, h, w_ple, w_gate, x, g_post.reshape(1, d), g_next.reshape(1, d))


def _merge_kernel(h_ref, ya_ref, yb_ref, yc_ref, yd_ref, ga_ref, gb_ref, gc_ref, gd_ref,
                  wa_ref, wb_ref, wc_ref, wd_ref, o_ref):
    h = h_ref[...]
    acc = jax.nn.sigmoid(_dot(h, ga_ref[...])) * _dot(ya_ref[...], wa_ref[...])
    acc += jax.nn.sigmoid(_dot(h, gb_ref[...])) * _dot(yb_ref[...], wb_ref[...])
    acc += jax.nn.sigmoid(_dot(h, gc_ref[...])) * _dot(yc_ref[...], wc_ref[...])
    acc += jax.nn.sigmoid(_dot(h, gd_ref[...])) * _dot(yd_ref[...], wd_ref[...])
    o_ref[...] = acc.astype(o_ref.dtype)


def merge_branches(h, ys, w_gate, ws, tm=1024, tn=256):
    m, dm = h.shape
    tm = min(tm, m)
    d = ws[0].shape[1]
    nb = d // tn
    in_specs = [pl.BlockSpec((tm, dm), lambda i, j: (i, 0))]
    in_specs += [pl.BlockSpec((tm, y.shape[1]), lambda i, j: (i, 0)) for y in ys]
    in_specs += [pl.BlockSpec((dm, tn), functools.partial(lambda i, j, b: (0, b * nb + j), b=b)) for b in range(4)]
    in_specs += [pl.BlockSpec((w.shape[0], tn), lambda i, j: (0, j)) for w in ws]
    return pl.pallas_call(
        _merge_kernel,
        out_shape=jax.ShapeDtypeStruct((m, d), BF16),
        grid=(m // tm, nb),
        in_specs=in_specs,
        out_specs=pl.BlockSpec((tm, tn), lambda i, j: (i, j)),
        compiler_params=_cparams("parallel", "arbitrary"),
        name="merge_branches",
    )(h, *ys, w_gate, w_gate, w_gate, w_gate, *ws)


def _ffn_up_kernel(h_ref, wg_ref, wu_ref, cwg_ref, cwu_ref, cbg_ref, cbu_ref, o_ref, bufg_ref, bufu_ref,
                   *, tm, seq_blocks):
    i = pl.program_id(1)

    @pl.when(i % seq_blocks == 0)
    def _():
        bufg_ref[0:SUBLANES, :] = jnp.zeros((SUBLANES, bufg_ref.shape[1]), F32)
        bufu_ref[0:SUBLANES, :] = jnp.zeros((SUBLANES, bufu_ref.shape[1]), F32)

    h = h_ref[...]

    def conv(w_ref, cw_ref, cb_ref, buf_ref):
        buf_ref[SUBLANES:SUBLANES + tm, :] = _dot(h, w_ref[...])
        cw = cw_ref[...]
        out = cb_ref[...] + cw[2:3, :] * buf_ref[SUBLANES:SUBLANES + tm, :]
        out += cw[1:2, :] * buf_ref[SUBLANES - 1:SUBLANES - 1 + tm, :]
        out += cw[0:1, :] * buf_ref[SUBLANES - 2:SUBLANES - 2 + tm, :]
        buf_ref[0:SUBLANES, :] = buf_ref[tm:tm + SUBLANES, :]
        return out

    g = conv(wg_ref, cwg_ref, cbg_ref, bufg_ref)
    u = conv(wu_ref, cwu_ref, cbu_ref, bufu_ref)
    o_ref[...] = (_gelu(g) * u).astype(o_ref.dtype)


def ffn_up(h, w_up, conv_w, conv_b, seq, tm=1024, tn=512):
    m, d = h.shape
    tm = min(tm, seq)
    dff = w_up.shape[1] // 2
    nb = dff // tn
    cb = conv_b.reshape(1, 2 * dff)
    return pl.pallas_call(
        functools.partial(_ffn_up_kernel, tm=tm, seq_blocks=seq // tm),
        out_shape=jax.ShapeDtypeStruct((m, dff), BF16),
        grid=(nb, m // tm),
        in_specs=[
            pl.BlockSpec((tm, d), lambda j, i: (i, 0)),
            pl.BlockSpec((d, tn), lambda j, i: (0, j)),
            pl.BlockSpec((d, tn), lambda j, i: (0, j + nb)),
            pl.BlockSpec((FFN_CONV, tn), lambda j, i: (0, j)),
            pl.BlockSpec((FFN_CONV, tn), lambda j, i: (0, j + nb)),
            pl.BlockSpec((1, tn), lambda j, i: (0, j)),
            pl.BlockSpec((1, tn), lambda j, i: (0, j + nb)),
        ],
        out_specs=pl.BlockSpec((tm, tn), lambda j, i: (i, j)),
        scratch_shapes=[pltpu.VMEM((tm + SUBLANES, tn), F32), pltpu.VMEM((tm + SUBLANES, tn), F32)],
        compiler_params=_cparams("parallel", "arbitrary"),
        name="ffn_up_conv_glu",
    )(h, w_up, w_up, conv_w, conv_w, cb, cb)


def _lru_kernel(x_ref, gate_ref, cw_ref, cb_ref, wr_ref, br_ref, wi_ref, bi_ref, lam_ref, o_ref,
                xbuf_ref, a0_ref, h0_ref, a1_ref, h1_ref, *, seq, pad):
    xbuf_ref[0:SUBLANES, :] = jnp.zeros((SUBLANES, LANES), F32)
    xbuf_ref[SUBLANES:SUBLANES + seq, :] = x_ref[0]
    cw = cw_ref[...]
    u = cb_ref[...] + cw[3:4, :] * xbuf_ref[SUBLANES:SUBLANES + seq, :]
    for k in range(LRU_CONV - 1):
        off = SUBLANES - (LRU_CONV - 1) + k
        u += cw[k:k + 1, :] * xbuf_ref[off:off + seq, :]
    ub = u.astype(BF16)
    r = jax.nn.sigmoid(_dot(ub, wr_ref[0]) + br_ref[...])
    ig = jax.nn.sigmoid(_dot(ub, wi_ref[0]) + bi_ref[...])
    log_a = -LRU_C * r * _softplus(-lam_ref[...])
    a = jnp.exp(log_a)
    inp = jnp.sqrt(1.0 - jnp.exp(2.0 * log_a)) * ig * u

    ones = jnp.ones((pad, LANES), F32)
    zeros = jnp.zeros((pad, LANES), F32)
    a0_ref[0:pad, :] = ones
    a1_ref[0:pad, :] = ones
    h0_ref[0:pad, :] = zeros
    h1_ref[0:pad, :] = zeros
    a0_ref[pad:pad + seq, :] = a
    h0_ref[pad:pad + seq, :] = inp
    bufs = ((a0_ref, h0_ref), (a1_ref, h1_ref))
    d = 1
    level = 0
    while d < seq:
        (a_src, h_src), (a_dst, h_dst) = bufs[level % 2], bufs[(level + 1) % 2]
        a_cur = a_src[pad:pad + seq, :]
        h_dst[pad:pad + seq, :] = h_src[pad:pad + seq, :] + a_cur * h_src[pad - d:pad - d + seq, :]
        a_dst[pad:pad + seq, :] = a_cur * a_src[pad - d:pad - d + seq, :]
        d *= 2
        level += 1
    h = bufs[level % 2][1][pad:pad + seq, :]
    o_ref[0] = (h * _gelu(gate_ref[0])).astype(o_ref.dtype)


def rglru(xg, conv_w, conv_b, w_r, b_r, w_i, b_i, lam):
    b, s, w2 = xg.shape
    w = w2 // 2
    nblk = w // LANES
    pad = s
    vec = lambda v: v.reshape(1, w)
    vspec = pl.BlockSpec((1, LANES), lambda bi, c: (0, c))
    return pl.pallas_call(
        functools.partial(_lru_kernel, seq=s, pad=pad),
        out_shape=jax.ShapeDtypeStruct((b, s, w), BF16),
        grid=(b, nblk),
        in_specs=[
            pl.BlockSpec((1, s, LANES), lambda bi, c: (bi, 0, c)),
            pl.BlockSpec((1, s, LANES), lambda bi, c: (bi, 0, c + nblk)),
            pl.BlockSpec((LRU_CONV, LANES), lambda bi, c: (0, c)),
            vspec,
            pl.BlockSpec((1, LANES, LANES), lambda bi, c: (c, 0, 0)),
            vspec,
            pl.BlockSpec((1, LANES, LANES), lambda bi, c: (c, 0, 0)),
            vspec, vspec,
        ],
        out_specs=pl.BlockSpec((1, s, LANES), lambda bi, c: (bi, 0, c)),
        scratch_shapes=[pltpu.VMEM((s + SUBLANES, LANES), F32)] + [pltpu.VMEM((pad + s, LANES), F32)] * 4,
        compiler_params=_cparams("parallel", "parallel"),
        name="rglru",
    )(xg, xg, conv_w, vec(conv_b), w_r, vec(b_r), w_i, vec(b_i), vec(lam))


DIL_UNITS_PER_PHASE = 4


def _dil_kernel(*refs, seq):
    ngroups = len(DIL_CONFIGS)
    qkv_refs = refs[:3 * ngroups]
    o_ref = refs[3 * ngroups]
    qf_ref, kf_ref, vf_ref, acc_ref, m_ref, l_ref = refs[3 * ngroups + 1:]
    scale = HEAD_DIM ** -0.5
    row = lax.broadcasted_iota(jnp.int32, (BLOCK, 2 * BLOCK), 0)
    col = lax.broadcasted_iota(jnp.int32, (BLOCK, 2 * BLOCK), 1)
    bias_two = jnp.where((col >= row) & (col <= row + BLOCK), 0.0, NEG_INF)
    bias_own = jnp.where(lax.broadcasted_iota(jnp.int32, (BLOCK, BLOCK), 1)
                         <= lax.broadcasted_iota(jnp.int32, (BLOCK, BLOCK), 0), 0.0, NEG_INF)

    def run_units(g, units):
        s = [_dot_nt(q, k) * scale + (bias_own if k.shape[0] == BLOCK else bias_two) for q, k, _, _ in units]
        mx = [jnp.max(x, axis=-1, keepdims=True) for x in s]
        e = [jnp.exp(x - m).astype(BF16) for x, m in zip(s, mx)]
        pv = [_dot(p, jnp.concatenate([v, jnp.ones_like(v)], axis=1)) for p, (_, _, v, _) in zip(e, units)]
        for y, m, (_, _, _, rows) in zip(pv, mx, units):
            acc_ref[g, rows, :] = y[:, :HEAD_DIM]
            l_ref[g, rows, :] = y[:, HEAD_DIM:]
            m_ref[g, rows, :] = jnp.broadcast_to(m, (BLOCK, HEAD_DIM))

    for g, (_, d) in enumerate(DIL_CONFIGS):
        q_ref, k_ref, v_ref = qkv_refs[3 * g:3 * g + 3]
        ln = seq // d
        nblk = ln // BLOCK
        if d > 1:
            qf_ref[...] = q_ref[...].astype(F32)
            kf_ref[...] = k_ref[...].astype(F32)
            vf_ref[...] = v_ref[...].astype(F32)
        units = []
        for r in range(d):
            if d > 1:
                qr = qf_ref[pl.ds(r, ln, stride=d), :].astype(BF16)
                kr = kf_ref[pl.ds(r, ln, stride=d), :].astype(BF16)
                vr = vf_ref[pl.ds(r, ln, stride=d), :].astype(BF16)
            for nb in range(nblk):
                lo = max(nb - 1, 0) * BLOCK
                hi = (nb + 1) * BLOCK
                if d > 1:
                    unit = (qr[nb * BLOCK:hi], kr[lo:hi], vr[lo:hi], pl.ds(nb * BLOCK * d + r, BLOCK, stride=d))
                else:
                    unit = (q_ref[nb * BLOCK:hi, :], k_ref[lo:hi, :], v_ref[lo:hi, :], pl.ds(nb * BLOCK, BLOCK))
                units.append(unit)
                if len(units) == DIL_UNITS_PER_PHASE:
                    run_units(g, units)
                    units = []
        if units:
            run_units(g, units)

    rows_per_step = 2 * BLOCK
    for c in range(seq // rows_per_step):
        rs = slice(c * rows_per_step, (c + 1) * rows_per_step)
        ms = [m_ref[g, rs, :] for g in range(ngroups)]
        top = functools.reduce(jnp.maximum, ms)
        ws = [jnp.exp(m - top) for m in ms]
        num = functools.reduce(jnp.add, [w * acc_ref[g, rs, :] for g, w in enumerate(ws)])
        den = functools.reduce(jnp.add, [w * l_ref[g, rs, :] for g, w in enumerate(ws)])
        o_ref[rs, :] = (num / den).astype(o_ref.dtype)


def dilated_attention(qkv, batch, seq):
    assert all(window // d == BLOCK and seq % (BLOCK * d) == 0 for window, d in DIL_CONFIGS)
    t, width = qkv.shape
    ngroups = len(DIL_CONFIGS)
    nheads = width // (3 * HEAD_DIM)

    def spec(which, g):
        return pl.BlockSpec((seq, HEAD_DIM), lambda b, h: (b, which * nheads + g * DIL_HEADS + h))

    in_specs = [spec(which, g) for g in range(ngroups) for which in range(3)]
    return pl.pallas_call(
        functools.partial(_dil_kernel, seq=seq),
        out_shape=jax.ShapeDtypeStruct((t, DIL_HEADS * HEAD_DIM), BF16),
        grid=(batch, DIL_HEADS),
        in_specs=in_specs,
        out_specs=pl.BlockSpec((seq, HEAD_DIM), lambda b, h: (b, h)),
        scratch_shapes=[pltpu.VMEM((seq, HEAD_DIM), F32)] * 3 + [pltpu.VMEM((ngroups, seq, HEAD_DIM), F32)] * 3,
        compiler_params=_cparams("parallel", "parallel"),
        name="dilated_attention",
    )(*([qkv] * (3 * ngroups)))


def _sb_kernel(q_ref, k_ref, v_ref, o_ref, acc_ref, run_ref):
    n = pl.program_id(1)
    nh = q_ref.shape[1] // HEAD_DIM
    heads = range(nh)
    sls = [slice(h * HEAD_DIM, (h + 1) * HEAD_DIM) for h in heads]
    scale = HEAD_DIM ** -0.5
    row = lax.broadcasted_iota(jnp.int32, (BLOCK, BLOCK), 0)
    col = lax.broadcasted_iota(jnp.int32, (BLOCK, BLOCK), 1)
    later_ones = jnp.concatenate([(row > col).astype(BF16), jnp.ones((BLOCK, BLOCK), BF16)], axis=1)
    strict = col < row
    q = [q_ref[:, sl] for sl in sls]

    def block(j, diag):
        start = pl.multiple_of(j * BLOCK, BLOCK)
        z = [_dot_nt(q[h], k_ref[pl.ds(start, BLOCK), sls[h]]) * scale for h in heads]
        log_beta = [jnp.minimum(zz, 0.0) - jnp.log(1.0 + jnp.exp(-jnp.abs(zz))) for zz in z]
        log_1m = [lb - zz for lb, zz in zip(log_beta, z)]
        if diag:
            log_1m = [jnp.where(strict, x, 0.0) for x in log_1m]
        sums = [_dot_exact_rhs(x, later_ones) for x in log_1m]
        for h in heads:
            prev = jnp.zeros((BLOCK, BLOCK), F32) if diag else run_ref[h]
            att = jnp.exp(log_beta[h] + sums[h][:, :BLOCK] + prev)
            if diag:
                att = jnp.where(strict, att, 0.0)
            pv = _dot(att.astype(BF16), v_ref[pl.ds(start, BLOCK), sls[h]])
            if diag:
                acc_ref[:, sls[h]] = pv
                run_ref[h] = sums[h][:, BLOCK:]
            else:
                acc_ref[:, sls[h]] += pv
                run_ref[h] = prev + sums[h][:, BLOCK:]

    block(n, True)

    def body(t, carry):
        block(n - 1 - t, False)
        return carry

    lax.fori_loop(0, n, body, 0)
    o_ref[...] = acc_ref[...].astype(o_ref.dtype)


def stick_breaking(qkv, batch, seq):
    t, width = qkv.shape
    w = width // 3
    nq = seq // BLOCK
    return pl.pallas_call(
        _sb_kernel,
        out_shape=jax.ShapeDtypeStruct((t, w), BF16),
        grid=(batch, nq),
        in_specs=[
            pl.BlockSpec((BLOCK, w), lambda b, n: (b * nq + n, 0)),
            pl.BlockSpec((seq, w), lambda b, n: (b, 1)),
            pl.BlockSpec((seq, w), lambda b, n: (b, 2)),
        ],
        out_specs=pl.BlockSpec((BLOCK, w), lambda b, n: (b * nq + n, 0)),
        scratch_shapes=[pltpu.VMEM((BLOCK, w), F32), pltpu.VMEM((w // HEAD_DIM, BLOCK, BLOCK), F32)],
        compiler_params=_cparams("parallel", "arbitrary"),
        name="stick_breaking",
    )(qkv, qkv, qkv)


def _head_sum(x, bd):
    cols = []
    for c in range(x.shape[1] // LANES):
        cols.append(_dot_exact_rhs(x[:, c * LANES:(c + 1) * LANES], bd))
    return jnp.concatenate(cols, axis=1)


def _rwkv_prep_kernel(*refs, tm, width, seq_blocks, has_vres):
    if has_vres:
        (seg_ref, prev_ref, mu_ref, w0_ref, wup_ref, a0_ref, aup_ref, gup_ref, kk_ref, ka_ref, rk_ref,
         vf_ref, v0_ref, vdn_ref, vup_ref,
         r_o, k_o, v_o, lw_o, kk_o, b_o, g_o, bon_o, buf_ref) = refs
    else:
        (seg_ref, prev_ref, mu_ref, w0_ref, wup_ref, a0_ref, aup_ref, gup_ref, kk_ref, ka_ref, rk_ref,
         r_o, k_o, v_o, lw_o, kk_o, b_o, g_o, bon_o, buf_ref) = refs
    i = pl.program_id(0)
    seg = seg_ref[...]
    buf_ref[0:SUBLANES, :] = prev_ref[...]

    @pl.when(i % seq_blocks == 0)
    def _():
        buf_ref[0:SUBLANES, :] = jnp.zeros((SUBLANES, buf_ref.shape[1]), F32)

    buf_ref[SUBLANES:SUBLANES + tm, :] = seg
    shifted = buf_ref[SUBLANES - 1:SUBLANES - 1 + tm, :]
    xs = seg + (shifted - seg) * mu_ref[...]
    w = width
    r = xs[:, 0:w]
    k = xs[:, w:2 * w]
    v = xs[:, 2 * w:3 * w]
    low = xs[:, 3 * w:3 * w + LANES]
    g_low = xs[:, 3 * w + LANES:3 * w + 3 * LANES]
    wpre = w0_ref[...] + _dot(jnp.tanh(low).astype(BF16), wup_ref[...])
    wlog = -_softplus(-wpre) - 0.5
    lw_o[...] = -jnp.exp(wlog)
    a = jax.nn.sigmoid(a0_ref[...] + _dot(low.astype(BF16), aup_ref[...]))
    g_o[...] = _dot(jax.nn.sigmoid(g_low).astype(BF16), gup_ref[...]).astype(g_o.dtype)
    if has_vres:
        mix = jax.nn.sigmoid(v0_ref[...] + _dot(_dot(v.astype(BF16), vdn_ref[...]).astype(BF16), vup_ref[...]))
        v = v + (vf_ref[...] - v) * mix
    row = lax.broadcasted_iota(jnp.int32, (LANES, LANES), 0) // RWKV_HEAD
    col = lax.broadcasted_iota(jnp.int32, (LANES, LANES), 1) // RWKV_HEAD
    bd = (row == col).astype(BF16)
    kk = k * kk_ref[...]
    norm = jnp.sqrt(_head_sum(kk * kk, bd))
    kk = kk / jnp.maximum(norm, 1e-12)
    k2 = k * (1.0 + (a - 1.0) * ka_ref[...])
    bonus = _head_sum(r * k2 * rk_ref[...], bd) * v
    r_o[...] = r.astype(r_o.dtype)
    k_o[...] = k2.astype(k_o.dtype)
    v_o[...] = v
    kk_o[...] = kk.astype(kk_o.dtype)
    b_o[...] = (kk * a).astype(b_o.dtype)
    bon_o[...] = bonus.astype(bon_o.dtype)


def rwkv_prep(seg, seq, mu, w0, w_up, a0, a_up, g_up, k_k, k_a, r_k, v_first, v_res, tm=256):
    t, wpad = seg.shape
    w = w0.shape[0]
    has_vres = v_res is not None
    row = lambda i: (i, 0)
    fix = lambda i: (0, 0)
    vec = lambda x: x.reshape(1, -1)
    step = tm // SUBLANES
    in_specs = [
        pl.BlockSpec((tm, wpad), row),
        pl.BlockSpec((SUBLANES, wpad), lambda i: (jnp.maximum(i * step - 1, 0), 0)),
        pl.BlockSpec((1, wpad), fix), pl.BlockSpec((1, w), fix), pl.BlockSpec((LANES, w), fix),
        pl.BlockSpec((1, w), fix), pl.BlockSpec((LANES, w), fix), pl.BlockSpec((2 * LANES, w), fix),
        pl.BlockSpec((1, w), fix), pl.BlockSpec((1, w), fix), pl.BlockSpec((1, w), fix),
    ]
    args = [seg, seg, vec(mu), vec(w0), w_up, vec(a0), a_up, g_up, vec(k_k), vec(k_a), vec(r_k)]
    if has_vres:
        v0, v_down, v_up = v_res
        in_specs += [pl.BlockSpec((tm, w), row), pl.BlockSpec((1, w), fix),
                     pl.BlockSpec((w, LANES), fix), pl.BlockSpec((LANES, w), fix)]
        args += [v_first, vec(v0), v_down, v_up]
    outs = tuple(jax.ShapeDtypeStruct((t, w), dt) for dt in (BF16, BF16, F32, F32, BF16, BF16, BF16, BF16))
    return pl.pallas_call(
        functools.partial(_rwkv_prep_kernel, tm=tm, width=w, seq_blocks=seq // tm, has_vres=has_vres),
        out_shape=outs,
        grid=(t // tm,),
        in_specs=in_specs,
        out_specs=(pl.BlockSpec((tm, w), row),) * 8,
        scratch_shapes=[pltpu.VMEM((tm + SUBLANES, wpad), F32)],
        compiler_params=_cparams("parallel"),
        name="rwkv_prep",
    )(*args)


def _rwkv_chunk_kernel(r_ref, k_ref, v_ref, lw_ref, kk_ref, b_ref, g_ref, bon_ref, gnw_ref, gnb_ref, o_ref,
                       state_ref):
    c = pl.program_id(1)
    n = RWKV_CHUNK
    n2 = 2 * n

    @pl.when(c == 0)
    def _():
        state_ref[...] = jnp.zeros_like(state_ref)

    tri = (lax.broadcasted_iota(jnp.int32, (n, n), 1) <= lax.broadcasted_iota(jnp.int32, (n, n), 0)).astype(BF16)
    row2 = lax.broadcasted_iota(jnp.int32, (n2, n2), 0)
    col2 = lax.broadcasted_iota(jnp.int32, (n2, n2), 1)
    t2 = row2 & (n - 1)
    s2 = col2 & (n - 1)
    strict = s2 < t2
    incl = s2 <= t2
    eye = (row2 == col2).astype(F32)
    head0 = lax.broadcasted_iota(jnp.int32, (1, LANES), 1) < RWKV_HEAD
    own = jnp.concatenate([jnp.broadcast_to(head0, (n, LANES)), jnp.broadcast_to(~head0, (n, LANES))], axis=0)

    def stack(x):
        return jnp.where(own, jnp.concatenate([x, x], axis=0), 0.0)

    pairs = range(r_ref.shape[1] // LANES)
    sls = [slice(hp * LANES, (hp + 1) * LANES) for hp in pairs]
    lw = lw_ref[...]
    gsum = _dot_exact_rhs_left(tri, lw)
    p_inv = jnp.exp(-gsum)
    p_end = jnp.exp(gsum[n - 1:n, :])
    rt_f = r_ref[...] * jnp.exp(gsum)
    kt_f = k_ref[...] * p_inv
    bt_f = b_ref[...] * p_inv
    kap_f = kk_ref[...] * jnp.exp(gsum - lw)
    v_f = v_ref[...]
    rt = [stack(rt_f[:, sl]).astype(BF16) for sl in sls]
    kt = [stack(kt_f[:, sl]) for sl in sls]
    bt = [stack(bt_f[:, sl]) for sl in sls]
    kap = [stack(kap_f[:, sl]).astype(BF16) for sl in sls]
    v2 = [stack(v_f[:, sl]).astype(BF16) for sl in sls]
    prod = [_dot_nt(jnp.concatenate([kap[hp], rt[hp]], axis=0),
                    jnp.concatenate([bt[hp].astype(BF16), kt[hp].astype(BF16)], axis=0)) for hp in pairs]
    a_ab = [jnp.where(strict, prod[hp][:n2, :n2], 0.0) for hp in pairs]
    a_ak = [jnp.where(strict, prod[hp][:n2, n2:], 0.0).astype(BF16) for hp in pairs]
    a_r = [jnp.concatenate([jnp.where(incl, prod[hp][n2:, n2:], 0.0).astype(BF16),
                            jnp.where(incl, -prod[hp][n2:, :n2], 0.0).astype(BF16)], axis=1) for hp in pairs]
    x = [eye - a_ab[hp] for hp in pairs]
    q = [_dot(a_ab[hp].astype(BF16), a_ab[hp].astype(BF16)) for hp in pairs]
    steps = 1
    while True:
        x = [x[hp] + _dot(x[hp].astype(BF16), q[hp].astype(BF16)) for hp in pairs]
        steps *= 2
        if steps * 2 >= n:
            break
        q = [_dot(q[hp].astype(BF16), q[hp].astype(BF16)) for hp in pairs]
    s0 = [state_ref[hp] for hp in pairs]
    s0b = [s.astype(BF16) for s in s0]
    rhs = [_dot_nt(kap[hp], s0b[hp]) + _dot(a_ak[hp], v2[hp]) for hp in pairs]
    u2 = [_dot(x[hp].astype(BF16), rhs[hp].astype(BF16)).astype(BF16) for hp in pairs]
    vu = [jnp.concatenate([v2[hp], u2[hp]], axis=0) for hp in pairs]
    y2 = [_dot_nt(rt[hp], s0b[hp]) + _dot(a_r[hp], vu[hp]) for hp in pairs]
    inv_n = 1.0 / RWKV_HEAD
    outs = []
    for hp in pairs:
        pe = p_end[:, sls[hp]]
        kb_end = jnp.concatenate([(kt[hp] * pe).astype(BF16), (-(bt[hp] * pe)).astype(BF16)], axis=0)
        state_ref[hp] = s0[hp] * pe + _dot_tn(vu[hp], kb_end)
        mean = jnp.sum(y2[hp], axis=-1, keepdims=True) * inv_n
        cen = jnp.where(own, y2[hp] - mean, 0.0)
        var = jnp.sum(cen * cen, axis=-1, keepdims=True) * inv_n
        yn2 = cen * lax.rsqrt(var + RWKV_GN_EPS)
        outs.append(yn2[:n] + yn2[n:])
    yn = jnp.concatenate(outs, axis=1) * gnw_ref[...] + gnb_ref[...]
    o_ref[...] = ((yn + bon_ref[...]) * g_ref[...]).astype(o_ref.dtype)


def _dot_exact_rhs_left(m_bf16, x):
    hi, lo = _split_bf16(x)
    return _dot(m_bf16, hi) + _dot(m_bf16, lo)


def rwkv_chunks(r, k, v, lw, kk, b, g, bon, gn_w, gn_b, batch, seq):
    t, w = r.shape
    nc = seq // RWKV_CHUNK
    spec = pl.BlockSpec((RWKV_CHUNK, w), lambda bi, c: (bi * nc + c, 0))
    vspec = pl.BlockSpec((1, w), lambda bi, c: (0, 0))
    return pl.pallas_call(
        _rwkv_chunk_kernel,
        out_shape=jax.ShapeDtypeStruct((t, w), BF16),
        grid=(batch, nc),
        in_specs=[spec] * 8 + [vspec, vspec],
        out_specs=spec,
        scratch_shapes=[pltpu.VMEM((w // LANES, LANES, LANES), F32)],
        compiler_params=_cparams("parallel", "arbitrary"),
        name="rwkv_chunks",
    )(r, k, v, lw, kk, b, g, bon, gn_w.reshape(1, w), gn_b.reshape(1, w))


def _pad_rows(w, rows):
    return jnp.pad(w, ((0, rows - w.shape[0]), (0, 0)))


def _pad_cols(w, cols):
    return jnp.pad(w, ((0, 0), (0, cols - w.shape[1])))


def _rope_tables(seq):
    half = HEAD_DIM // 2
    inv_freq = ROPE_THETA ** (-jnp.arange(half, dtype=F32) / half)
    ang = jnp.arange(seq, dtype=F32)[:, None] * inv_freq[None, :]
    cos = jnp.cos(ang)
    sin = jnp.sin(ang)
    return jnp.concatenate([cos, cos], axis=1), jnp.concatenate([-sin, sin], axis=1)


def kernel(x, p, norm_mix_pre, norm_mix_post, norm_ffn_pre, norm_ffn_post, norm_ple_pre, norm_ple_post, w_in, w_merge_gate, lru_conv_w, lru_conv_b, lru_w_r, lru_b_r, lru_w_i, lru_b_i, lru_lambda, rwkv_mu, rwkv_w0, rwkv_w_up, rwkv_a0, rwkv_a_up, rwkv_g_up, rwkv_k_k, rwkv_k_a, rwkv_r_k, rwkv_gn_w, rwkv_gn_b, rwkv_v0, rwkv_v_down, rwkv_v_up, w_branch_a, w_branch_b, w_branch_c, w_branch_d, w_out, w_ffn_up, ffn_conv_w, ffn_conv_b, w_ffn_down, w_ple, w_ple_gate):
    batch, seq, d = x.shape
    depth = w_in.shape[0]
    t = batch * seq
    lru_w = lru_conv_w.shape[2]
    rw = rwkv_w0.shape[1]
    dil_w = 3 * len(DIL_CONFIGS) * DIL_HEADS * HEAD_DIM
    sb_w = 3 * (d // 2)
    off_b = 2 * lru_w
    off_c = off_b + dil_w
    off_d = off_c + sb_w
    rwkv_in = w_in.shape[2] - off_d
    rwkv_pad = 3 * rw + 4 * LANES
    cos, sin = _rope_tables(seq)

    xf = x.reshape(t, d)
    h = rmsnorm_bf16(xf, norm_mix_pre[0])
    v_first = None
    for i in range(depth):
        wi = w_in[i]
        w_a = wi[:, :off_b].astype(BF16)
        w_b = wi[:, off_b:off_c].astype(BF16)
        w_c = wi[:, off_c:off_d].astype(BF16)
        w_d = _pad_cols(wi[:, off_d:], rwkv_pad).astype(BF16)
        seg_a = matmul(h, w_a, F32)
        seg_b = matmul_rope(h, w_b, cos, sin, seq, 2 * dil_w // 3, BF16)
        seg_c = matmul(h, w_c, BF16)
        seg_d = matmul(h, w_d, F32, tm=512, tn=rwkv_pad // 2)
        y_a = rglru(seg_a.reshape(batch, seq, off_b), lru_conv_w[i], lru_conv_b[i], lru_w_r[i].astype(BF16),
                    lru_b_r[i], lru_w_i[i].astype(BF16), lru_b_i[i], lru_lambda[i]).reshape(t, lru_w)
        y_b = dilated_attention(seg_b, batch, seq)
        y_c = stick_breaking(seg_c, batch, seq)
        mu = jnp.pad(rwkv_mu[i], (0, rwkv_pad - rwkv_in))
        w_up = _pad_rows(rwkv_w_up[i], LANES).astype(BF16)
        a_up = jnp.pad(rwkv_a_up[i], ((RWKV_W_LORA, LANES - RWKV_W_LORA - RWKV_A_LORA), (0, 0))).astype(BF16)
        g_up = _pad_rows(rwkv_g_up[i], 2 * LANES).astype(BF16)
        v_res = None
        if i > 0:
            v_res = (rwkv_v0[i - 1], _pad_cols(rwkv_v_down[i - 1], LANES).astype(BF16),
                     _pad_rows(rwkv_v_up[i - 1], LANES).astype(BF16))
        r_, k_, v_, lw_, kk_, b_, g_, bon_ = rwkv_prep(
            seg_d, seq, mu, rwkv_w0[i], w_up, rwkv_a0[i], a_up, g_up, rwkv_k_k[i], rwkv_k_a[i],
            rwkv_r_k[i].reshape(-1), v_first, v_res)
        if i == 0:
            v_first = v_
        y_d = rwkv_chunks(r_, k_, v_, lw_, kk_, b_, g_, bon_, rwkv_gn_w[i], rwkv_gn_b[i], batch, seq)
        merged = merge_branches(
            h, (y_a, y_b, y_c, y_d), w_merge_gate[i].astype(BF16),
            (w_branch_a[i].astype(BF16), w_branch_b[i].astype(BF16), w_branch_c[i].astype(BF16),
             w_branch_d[i].astype(BF16)))
        xf, h = matmul_norm_res(merged, w_out[i].astype(BF16), xf, norm_mix_post[i], norm_ffn_pre[i], tk=d)
        act = ffn_up(h, w_ffn_up[i].astype(BF16), ffn_conv_w[i], ffn_conv_b[i], seq)
        xf, h = matmul_norm_res(act, w_ffn_down[i].astype(BF16), xf, norm_ffn_post[i], norm_ple_pre[i],
                                tm=512, tk=w_ffn_down.shape[1] // 4)
        g_next = norm_mix_pre[i + 1] if i + 1 < depth else norm_mix_pre[i]
        xf, h = ple_norm_res(p[i].reshape(t, -1).astype(BF16), h, w_ple[i].astype(BF16),
                             w_ple_gate[i].astype(BF16), xf, norm_ple_post[i], g_next)
    return xf.reshape(batch, seq, d)
```

```python
import functools

import jax
import jax.numpy as jnp
from jax import lax
from jax.experimental import pallas as pl
from jax.experimental.pallas import tpu as pltpu

F32 = jnp.float32
BF16 = jnp.bfloat16

LANES = 128
SUBLANES = 8
VMEM_LIMIT_BYTES = 52 * 1024 * 1024

HEAD_DIM = 128
BLOCK = 128
ROPE_THETA = 10000.0
RMS_EPS = 1e-6
NEG_INF = -1e30

LRU_BLOCKS = 8
LRU_CONV = 4
LRU_C = 8.0
DIL_CONFIGS = ((128, 1), (512, 4), (2048, 16))
DIL_HEADS = 4
RWKV_HEAD = 64
RWKV_W_LORA = 64
RWKV_A_LORA = 64
RWKV_G_LORA = 160
RWKV_V_LORA = 32
RWKV_GN_EPS = 64e-5
RWKV_CHUNK = 64
RWKV_CHUNKS_PER_STEP = 2
FFN_CONV = 3
SB_KEYS = 256


def _cparams(*sem):
    return pltpu.CompilerParams(dimension_semantics=sem, vmem_limit_bytes=VMEM_LIMIT_BYTES)


def _dot(a, b):
    return jnp.dot(a, b, preferred_element_type=F32)


def _dot_nt(a, b):
    return lax.dot_general(a, b, (((1,), (1,)), ((), ())), preferred_element_type=F32)


def _dot_tn(a, b):
    return lax.dot_general(a, b, (((0,), (0,)), ((), ())), preferred_element_type=F32)


def _split_bf16(x):
    hi = x.astype(BF16)
    lo = (x - hi.astype(F32)).astype(BF16)
    return hi, lo


def _dot_exact_rhs(x, m_bf16):
    hi, lo = _split_bf16(x)
    return _dot(hi, m_bf16) + _dot(lo, m_bf16)


def _rms(x, g):
    return x * lax.rsqrt(jnp.mean(x * x, axis=-1, keepdims=True) + RMS_EPS) * g


def _gelu(x):
    return jax.nn.gelu(x, approximate=True)


def _softplus(x):
    return jnp.maximum(x, 0.0) + jnp.log1p(jnp.exp(-jnp.abs(x)))


def _rmsnorm_kernel(x_ref, g_ref, o_ref):
    o_ref[...] = _rms(x_ref[...], g_ref[...]).astype(o_ref.dtype)


def rmsnorm_bf16(x, g, tm=512):
    t, d = x.shape
    return pl.pallas_call(
        _rmsnorm_kernel,
        out_shape=jax.ShapeDtypeStruct((t, d), BF16),
        grid=(t // tm,),
        in_specs=[pl.BlockSpec((tm, d), lambda i: (i, 0)), pl.BlockSpec((1, d), lambda i: (0, 0))],
        out_specs=pl.BlockSpec((tm, d), lambda i: (i, 0)),
        compiler_params=_cparams("parallel"),
        name="rmsnorm",
    )(x, g.reshape(1, d))


def _mm_kernel(a_ref, w_ref, o_ref):
    o_ref[...] = _dot(a_ref[...], w_ref[...]).astype(o_ref.dtype)


def _mm_rope_kernel(a_ref, w_ref, cos_ref, sin_ref, o_ref, *, n_rope_blocks, tn):
    j = pl.program_id(1)
    acc = _dot(a_ref[...], w_ref[...])

    @pl.when(j < n_rope_blocks)
    def _():
        cos = cos_ref[...]
        sin = sin_ref[...]
        for c in range(tn // HEAD_DIM):
            seg = acc[:, c * HEAD_DIM:(c + 1) * HEAD_DIM]
            rot = pltpu.roll(seg, HEAD_DIM // 2, axis=1)
            o_ref[:, c * HEAD_DIM:(c + 1) * HEAD_DIM] = (seg * cos + rot * sin).astype(o_ref.dtype)

    @pl.when(j >= n_rope_blocks)
    def _():
        o_ref[...] = acc.astype(o_ref.dtype)


def matmul(a, w, out_dtype, tm=1024, tn=1024):
    m, k = a.shape
    n = w.shape[1]
    tm, tn = min(tm, m), min(tn, n)
    return pl.pallas_call(
        _mm_kernel,
        out_shape=jax.ShapeDtypeStruct((m, n), out_dtype),
        grid=(m // tm, n // tn),
        in_specs=[pl.BlockSpec((tm, k), lambda i, j: (i, 0)), pl.BlockSpec((k, tn), lambda i, j: (0, j))],
        out_specs=pl.BlockSpec((tm, tn), lambda i, j: (i, j)),
        compiler_params=_cparams("parallel", "arbitrary"),
        name="matmul_plain",
    )(a, w)


def matmul_rope(a, w, cos, sin, seq, n_rope_cols, out_dtype, tm=1024, tn=1536):
    m, k = a.shape
    n = w.shape[1]
    tm, tn = min(tm, m), min(tn, n)
    sblocks = seq // tm
    return pl.pallas_call(
        functools.partial(_mm_rope_kernel, n_rope_blocks=n_rope_cols // tn, tn=tn),
        out_shape=jax.ShapeDtypeStruct((m, n), out_dtype),
        grid=(m // tm, n // tn),
        in_specs=[
            pl.BlockSpec((tm, k), lambda i, j: (i, 0)),
            pl.BlockSpec((k, tn), lambda i, j: (0, j)),
            pl.BlockSpec((tm, HEAD_DIM), lambda i, j: (i % sblocks, 0)),
            pl.BlockSpec((tm, HEAD_DIM), lambda i, j: (i % sblocks, 0)),
        ],
        out_specs=pl.BlockSpec((tm, tn), lambda i, j: (i, j)),
        compiler_params=_cparams("parallel", "arbitrary"),
        name="matmul_rope",
    )(a, w, cos, sin)


def _mm_norm_res_kernel(a_ref, w_ref, x_ref, gpost_ref, gnext_ref, xo_ref, ho_ref, *acc, nk):
    def finish(val):
        xn = x_ref[...] + _rms(val, gpost_ref[...])
        xo_ref[...] = xn
        ho_ref[...] = _rms(xn, gnext_ref[...]).astype(ho_ref.dtype)

    if nk == 1:
        finish(_dot(a_ref[...], w_ref[...]))
        return
    acc_ref, = acc
    kk = pl.program_id(1)

    @pl.when(kk == 0)
    def _():
        acc_ref[...] = _dot(a_ref[...], w_ref[...])

    @pl.when(kk > 0)
    def _():
        acc_ref[...] += _dot(a_ref[...], w_ref[...])

    @pl.when(kk == nk - 1)
    def _():
        finish(acc_ref[...])


def matmul_norm_res(a, w, x, g_post, g_next, tm=256, tk=512):
    m, k = a.shape
    d = w.shape[1]
    tm = min(tm, m)
    nk = k // tk
    return pl.pallas_call(
        functools.partial(_mm_norm_res_kernel, nk=nk),
        out_shape=(jax.ShapeDtypeStruct((m, d), F32), jax.ShapeDtypeStruct((m, d), BF16)),
        grid=(m // tm, nk),
        in_specs=[
            pl.BlockSpec((tm, tk), lambda i, kk: (i, kk)),
            pl.BlockSpec((tk, d), lambda i, kk: (kk, 0)),
            pl.BlockSpec((tm, d), lambda i, kk: (i, 0)),
            pl.BlockSpec((1, d), lambda i, kk: (0, 0)),
            pl.BlockSpec((1, d), lambda i, kk: (0, 0)),
        ],
        out_specs=(pl.BlockSpec((tm, d), lambda i, kk: (i, 0)), pl.BlockSpec((tm, d), lambda i, kk: (i, 0))),
        scratch_shapes=[pltpu.VMEM((tm, d), F32)] if nk > 1 else [],
        compiler_params=_cparams("parallel", "arbitrary"),
        name="matmul_norm_res",
    )(a, w, x, g_post.reshape(1, d), g_next.reshape(1, d))


def _ple_kernel(p_ref, h_ref, wp_ref, wg_ref, x_ref, gpost_ref, gnext_ref, xo_ref, ho_ref):
    val = _dot(p_ref[...], wp_ref[...]) * jax.nn.sigmoid(_dot(h_ref[...], wg_ref[...]))
    xn = x_ref[...] + _rms(val, gpost_ref[...])
    xo_ref[...] = xn
    ho_ref[...] = _rms(xn, gnext_ref[...]).astype(ho_ref.dtype)


def ple_norm_res(p, h, w_ple, w_gate, x, g_post, g_next, tm=512):
    m, d = x.shape
    tm = min(tm, m)
    pd = p.shape[1]
    row = lambda i: (i, 0)
    fix = lambda i: (0, 0)
    return pl.pallas_call(
        _ple_kernel,
        out_shape=(jax.ShapeDtypeStruct((m, d), F32), jax.ShapeDtypeStruct((m, d), BF16)),
        grid=(m // tm,),
        in_specs=[
            pl.BlockSpec((tm, pd), row), pl.BlockSpec((tm, d), row),
            pl.BlockSpec((pd, d), fix), pl.BlockSpec((d, d), fix),
            pl.BlockSpec((tm, d), row), pl.BlockSpec((1, d), fix), pl.BlockSpec((1, d), fix),
        ],
        out_specs=(pl.BlockSpec((tm, d), row), pl.BlockSpec((tm, d), row)),
        compiler_params=_cparams("parallel"),
        name="ple_norm_res",
    )(p, h, w_ple, w_gate, x, g_post.reshape(1, d), g_next.reshape(1, d))


def _merge_kernel(h_ref, ya_ref, yb_ref, yc_ref, yd_ref, ga_ref, gb_ref, gc_ref, gd_ref,
                  wa_ref, wb_ref, wc_ref, wd_ref, o_ref):
    h = h_ref[...]
    acc = jax.nn.sigmoid(_dot(h, ga_ref[...])) * _dot(ya_ref[...], wa_ref[...])
    acc += jax.nn.sigmoid(_dot(h, gb_ref[...])) * _dot(yb_ref[...], wb_ref[...])
    acc += jax.nn.sigmoid(_dot(h, gc_ref[...])) * _dot(yc_ref[...], wc_ref[...])
    acc += jax.nn.sigmoid(_dot(h, gd_ref[...])) * _dot(yd_ref[...], wd_ref[...])
    o_ref[...] = acc.astype(o_ref.dtype)


def merge_branches(h, ys, w_gate, ws, tm=1024, tn=256):
    m, dm = h.shape
    tm = min(tm, m)
    d = ws[0].shape[1]
    nb = d // tn
    in_specs = [pl.BlockSpec((tm, dm), lambda i, j: (i, 0))]
    in_specs += [pl.BlockSpec((tm, y.shape[1]), lambda i, j: (i, 0)) for y in ys]
    in_specs += [pl.BlockSpec((dm, tn), functools.partial(lambda i, j, b: (0, b * nb + j), b=b)) for b in range(4)]
    in_specs += [pl.BlockSpec((w.shape[0], tn), lambda i, j: (0, j)) for w in ws]
    return pl.pallas_call(
        _merge_kernel,
        out_shape=jax.ShapeDtypeStruct((m, d), BF16),
        grid=(m // tm, nb),
        in_specs=in_specs,
        out_specs=pl.BlockSpec((tm, tn), lambda i, j: (i, j)),
        compiler_params=_cparams("parallel", "arbitrary"),
        name="merge_branches",
    )(h, *ys, w_gate, w_gate, w_gate, w_gate, *ws)


def _ffn_up_kernel(h_ref, wg_ref, wu_ref, cwg_ref, cwu_ref, cbg_ref, cbu_ref, o_ref, bufg_ref, bufu_ref,
                   *, tm, seq_blocks):
    i = pl.program_id(1)

    @pl.when(i % seq_blocks == 0)
    def _():
        bufg_ref[0:SUBLANES, :] = jnp.zeros((SUBLANES, bufg_ref.shape[1]), F32)
        bufu_ref[0:SUBLANES, :] = jnp.zeros((SUBLANES, bufu_ref.shape[1]), F32)

    h = h_ref[...]

    def conv(w_ref, cw_ref, cb_ref, buf_ref):
        buf_ref[SUBLANES:SUBLANES + tm, :] = _dot(h, w_ref[...])
        cw = cw_ref[...]
        out = cb_ref[...] + cw[2:3, :] * buf_ref[SUBLANES:SUBLANES + tm, :]
        out += cw[1:2, :] * buf_ref[SUBLANES - 1:SUBLANES - 1 + tm, :]
        out += cw[0:1, :] * buf_ref[SUBLANES - 2:SUBLANES - 2 + tm, :]
        buf_ref[0:SUBLANES, :] = buf_ref[tm:tm + SUBLANES, :]
        return out

    g = conv(wg_ref, cwg_ref, cbg_ref, bufg_ref)
    u = conv(wu_ref, cwu_ref, cbu_ref, bufu_ref)
    o_ref[...] = (_gelu(g) * u).astype(o_ref.dtype)


def ffn_up(h, w_up, conv_w, conv_b, seq, tm=1024, tn=512):
    m, d = h.shape
    tm = min(tm, seq)
    dff = w_up.shape[1] // 2
    nb = dff // tn
    cb = conv_b.reshape(1, 2 * dff)
    return pl.pallas_call(
        functools.partial(_ffn_up_kernel, tm=tm, seq_blocks=seq // tm),
        out_shape=jax.ShapeDtypeStruct((m, dff), BF16),
        grid=(nb, m // tm),
        in_specs=[
            pl.BlockSpec((tm, d), lambda j, i: (i, 0)),
            pl.BlockSpec((d, tn), lambda j, i: (0, j)),
            pl.BlockSpec((d, tn), lambda j, i: (0, j + nb)),
            pl.BlockSpec((FFN_CONV, tn), lambda j, i: (0, j)),
            pl.BlockSpec((FFN_CONV, tn), lambda j, i: (0, j + nb)),
            pl.BlockSpec((1, tn), lambda j, i: (0, j)),
            pl.BlockSpec((1, tn), lambda j, i: (0, j + nb)),
        ],
        out_specs=pl.BlockSpec((tm, tn), lambda j, i: (i, j)),
        scratch_shapes=[pltpu.VMEM((tm + SUBLANES, tn), F32), pltpu.VMEM((tm + SUBLANES, tn), F32)],
        compiler_params=_cparams("parallel", "arbitrary"),
        name="ffn_up_conv_glu",
    )(h, w_up, w_up, conv_w, conv_w, cb, cb)


def _lru_kernel(x_ref, gate_ref, cw_ref, cb_ref, wr_ref, br_ref, wi_ref, bi_ref, lam_ref, o_ref,
                xbuf_ref, a0_ref, h0_ref, a1_ref, h1_ref, *, seq, pad):
    xbuf_ref[0:SUBLANES, :] = jnp.zeros((SUBLANES, LANES), F32)
    xbuf_ref[SUBLANES:SUBLANES + seq, :] = x_ref[0]
    cw = cw_ref[...]
    u = cb_ref[...] + cw[3:4, :] * xbuf_ref[SUBLANES:SUBLANES + seq, :]
    for k in range(LRU_CONV - 1):
        off = SUBLANES - (LRU_CONV - 1) + k
        u += cw[k:k + 1, :] * xbuf_ref[off:off + seq, :]
    ub = u.astype(BF16)
    r = jax.nn.sigmoid(_dot(ub, wr_ref[0]) + br_ref[...])
    ig = jax.nn.sigmoid(_dot(ub, wi_ref[0]) + bi_ref[...])
    log_a = -LRU_C * r * _softplus(-lam_ref[...])
    a = jnp.exp(log_a)
    inp = jnp.sqrt(1.0 - jnp.exp(2.0 * log_a)) * ig * u

    ones = jnp.ones((pad, LANES), F32)
    zeros = jnp.zeros((pad, LANES), F32)
    a0_ref[0:pad, :] = ones
    a1_ref[0:pad, :] = ones
    h0_ref[0:pad, :] = zeros
    h1_ref[0:pad, :] = zeros
    a0_ref[pad:pad + seq, :] = a
    h0_ref[pad:pad + seq, :] = inp
    bufs = ((a0_ref, h0_ref), (a1_ref, h1_ref))
    d = 1
    level = 0
    while d < seq:
        (a_src, h_src), (a_dst, h_dst) = bufs[level % 2], bufs[(level + 1) % 2]
        a_cur = a_src[pad:pad + seq, :]
        h_dst[pad:pad + seq, :] = h_src[pad:pad + seq, :] + a_cur * h_src[pad - d:pad - d + seq, :]
        a_dst[pad:pad + seq, :] = a_cur * a_src[pad - d:pad - d + seq, :]
        d *= 2
        level += 1
    h = bufs[level % 2][1][pad:pad + seq, :]
    o_ref[0] = (h * _gelu(gate_ref[0])).astype(o_ref.dtype)


def rglru(xg, conv_w, conv_b, w_r, b_r, w_i, b_i, lam):
    b, s, w2 = xg.shape
    w = w2 // 2
    nblk = w // LANES
    pad = s
    vec = lambda v: v.reshape(1, w)
    vspec = pl.BlockSpec((1, LANES), lambda bi, c: (0, c))
    return pl.pallas_call(
        functools.partial(_lru_kernel, seq=s, pad=pad),
        out_shape=jax.ShapeDtypeStruct((b, s, w), BF16),
        grid=(b, nblk),
        in_specs=[
            pl.BlockSpec((1, s, LANES), lambda bi, c: (bi, 0, c)),
            pl.BlockSpec((1, s, LANES), lambda bi, c: (bi, 0, c + nblk)),
            pl.BlockSpec((LRU_CONV, LANES), lambda bi, c: (0, c)),
            vspec,
            pl.BlockSpec((1, LANES, LANES), lambda bi, c: (c, 0, 0)),
            vspec,
            pl.BlockSpec((1, LANES, LANES), lambda bi, c: (c, 0, 0)),
            vspec, vspec,
        ],
        out_specs=pl.BlockSpec((1, s, LANES), lambda bi, c: (bi, 0, c)),
        scratch_shapes=[pltpu.VMEM((s + SUBLANES, LANES), F32)] + [pltpu.VMEM((pad + s, LANES), F32)] * 4,
        compiler_params=_cparams("parallel", "parallel"),
        name="rglru",
    )(xg, xg, conv_w, vec(conv_b), w_r, vec(b_r), w_i, vec(b_i), vec(lam))


DIL_UNITS_PER_PHASE = 4


def _dil_kernel(*refs, seq):
    ngroups = len(DIL_CONFIGS)
    qkv_refs = refs[:3 * ngroups]
    o_ref = refs[3 * ngroups]
    qf_ref, kf_ref, vf_ref, acc_ref, m_ref, l_ref = refs[3 * ngroups + 1:]
    scale = HEAD_DIM ** -0.5
    row = lax.broadcasted_iota(jnp.int32, (BLOCK, 2 * BLOCK), 0)
    col = lax.broadcasted_iota(jnp.int32, (BLOCK, 2 * BLOCK), 1)
    bias_two = jnp.where((col >= row) & (col <= row + BLOCK), 0.0, NEG_INF)
    bias_own = jnp.where(lax.broadcasted_iota(jnp.int32, (BLOCK, BLOCK), 1)
                         <= lax.broadcasted_iota(jnp.int32, (BLOCK, BLOCK), 0), 0.0, NEG_INF)

    def run_units(g, units):
        s = [_dot_nt(q, k) * scale + (bias_own if k.shape[0] == BLOCK else bias_two) for q, k, _, _ in units]
        mx = [jnp.max(x, axis=-1, keepdims=True) for x in s]
        e = [jnp.exp(x - m).astype(BF16) for x, m in zip(s, mx)]
        pv = [_dot(p, jnp.concatenate([v, jnp.ones_like(v)], axis=1)) for p, (_, _, v, _) in zip(e, units)]
        for y, m, (_, _, _, rows) in zip(pv, mx, units):
            acc_ref[g, rows, :] = y[:, :HEAD_DIM]
            l_ref[g, rows, :] = y[:, HEAD_DIM:]
            m_ref[g, rows, :] = jnp.broadcast_to(m, (BLOCK, HEAD_DIM))

    for g, (_, d) in enumerate(DIL_CONFIGS):
        q_ref, k_ref, v_ref = qkv_refs[3 * g:3 * g + 3]
        ln = seq // d
        nblk = ln // BLOCK
        if d > 1:
            qf_ref[...] = q_ref[...].astype(F32)
            kf_ref[...] = k_ref[...].astype(F32)
            vf_ref[...] = v_ref[...].astype(F32)
        units = []
        for r in range(d):
            if d > 1:
                qr = qf_ref[pl.ds(r, ln, stride=d), :].astype(BF16)
                kr = kf_ref[pl.ds(r, ln, stride=d), :].astype(BF16)
                vr = vf_ref[pl.ds(r, ln, stride=d), :].astype(BF16)
            for nb in range(nblk):
                lo = max(nb - 1, 0) * BLOCK
                hi = (nb + 1) * BLOCK
                if d > 1:
                    unit = (qr[nb * BLOCK:hi], kr[lo:hi], vr[lo:hi], pl.ds(nb * BLOCK * d + r, BLOCK, stride=d))
                else:
                    unit = (q_ref[nb * BLOCK:hi, :], k_ref[lo:hi, :], v_ref[lo:hi, :], pl.ds(nb * BLOCK, BLOCK))
                units.append(unit)
                if len(units) == DIL_UNITS_PER_PHASE:
                    run_units(g, units)
                    units = []
        if units:
            run_units(g, units)

    rows_per_step = 2 * BLOCK
    for c in range(seq // rows_per_step):
        rs = slice(c * rows_per_step, (c + 1) * rows_per_step)
        ms = [m_ref[g, rs, :] for g in range(ngroups)]
        top = functools.reduce(jnp.maximum, ms)
        ws = [jnp.exp(m - top) for m in ms]
        num = functools.reduce(jnp.add, [w * acc_ref[g, rs, :] for g, w in enumerate(ws)])
        den = functools.reduce(jnp.add, [w * l_ref[g, rs, :] for g, w in enumerate(ws)])
        o_ref[rs, :] = (num / den).astype(o_ref.dtype)


def dilated_attention(qkv, batch, seq):
    assert all(window // d == BLOCK and seq % (BLOCK * d) == 0 for window, d in DIL_CONFIGS)
    t, width = qkv.shape
    ngroups = len(DIL_CONFIGS)
    nheads = width // (3 * HEAD_DIM)

    def spec(which, g):
        return pl.BlockSpec((seq, HEAD_DIM), lambda b, h: (b, which * nheads + g * DIL_HEADS + h))

    in_specs = [spec(which, g) for g in range(ngroups) for which in range(3)]
    return pl.pallas_call(
        functools.partial(_dil_kernel, seq=seq),
        out_shape=jax.ShapeDtypeStruct((t, DIL_HEADS * HEAD_DIM), BF16),
        grid=(batch, DIL_HEADS),
        in_specs=in_specs,
        out_specs=pl.BlockSpec((seq, HEAD_DIM), lambda b, h: (b, h)),
        scratch_shapes=[pltpu.VMEM((seq, HEAD_DIM), F32)] * 3 + [pltpu.VMEM((ngroups, seq, HEAD_DIM), F32)] * 3,
        compiler_params=_cparams("parallel", "parallel"),
        name="dilated_attention",
    )(*([qkv] * (3 * ngroups)))


def _sb_kernel(q_ref, k_ref, v_ref, o_ref, acc_ref, run_ref):
    n = pl.program_id(1)
    nh = q_ref.shape[1] // HEAD_DIM
    heads = range(nh)
    sls = [slice(h * HEAD_DIM, (h + 1) * HEAD_DIM) for h in heads]
    scale = HEAD_DIM ** -0.5
    kb = SB_KEYS
    assert kb == 2 * BLOCK
    from_s = (lax.broadcasted_iota(jnp.int32, (kb, kb), 0) >= lax.broadcasted_iota(jnp.int32, (kb, kb), 1)).astype(BF16)
    off = (n % 2) * BLOCK
    before = (lax.broadcasted_iota(jnp.int32, (BLOCK, kb), 1) - lax.broadcasted_iota(jnp.int32, (BLOCK, kb), 0)) < off
    q = [q_ref[:, sl] for sl in sls]

    def block(jb, diag):
        start = pl.multiple_of(jb * kb, kb)
        z = [_dot_nt(q[h], k_ref[pl.ds(start, kb), sls[h]]) * scale for h in heads]
        log_beta = [jnp.minimum(zz, 0.0) - jnp.log(1.0 + jnp.exp(-jnp.abs(zz))) for zz in z]
        log_1m = [lb - zz for lb, zz in zip(log_beta, z)]
        if diag:
            log_1m = [jnp.where(before, x, 0.0) for x in log_1m]
        incl = [_dot_exact_rhs(x, from_s) for x in log_1m]
        for h in heads:
            att = jnp.exp(z[h] + incl[h]) if diag else jnp.exp(z[h] + incl[h] + run_ref[h])
            if diag:
                att = jnp.where(before, att, 0.0)
            pv = _dot(att.astype(BF16), v_ref[pl.ds(start, kb), sls[h]])
            total = jnp.broadcast_to(incl[h][:, 0:1], (BLOCK, kb))
            if diag:
                acc_ref[:, sls[h]] = pv
                run_ref[h] = total
            else:
                acc_ref[:, sls[h]] += pv
                run_ref[h] = run_ref[h] + total

    full = n // 2
    block(full, True)

    def body(t, carry):
        block(full - 1 - t, False)
        return carry

    lax.fori_loop(0, full, body, 0)
    o_ref[...] = acc_ref[...].astype(o_ref.dtype)


def stick_breaking(qkv, batch, seq):
    t, width = qkv.shape
    w = width // 3
    nq = seq // BLOCK
    assert seq % SB_KEYS == 0
    return pl.pallas_call(
        _sb_kernel,
        out_shape=jax.ShapeDtypeStruct((t, w), BF16),
        grid=(batch, nq),
        in_specs=[
            pl.BlockSpec((BLOCK, w), lambda b, n: (b * nq + n, 0)),
            pl.BlockSpec((seq, w), lambda b, n: (b, 1)),
            pl.BlockSpec((seq, w), lambda b, n: (b, 2)),
        ],
        out_specs=pl.BlockSpec((BLOCK, w), lambda b, n: (b * nq + n, 0)),
        scratch_shapes=[pltpu.VMEM((BLOCK, w), F32), pltpu.VMEM((w // HEAD_DIM, BLOCK, SB_KEYS), F32)],
        compiler_params=_cparams("parallel", "arbitrary"),
        name="stick_breaking",
    )(qkv, qkv, qkv)


def _head_sum(x, bd):
    cols = []
    for c in range(x.shape[1] // LANES):
        cols.append(_dot_exact_rhs(x[:, c * LANES:(c + 1) * LANES], bd))
    return jnp.concatenate(cols, axis=1)


def _rwkv_prep_kernel(*refs, tm, width, seq_blocks, has_vres):
    if has_vres:
        (seg_ref, prev_ref, mu_ref, w0_ref, wup_ref, a0_ref, aup_ref, gup_ref, kk_ref, ka_ref, rk_ref,
         vf_ref, v0_ref, vdn_ref, vup_ref,
         r_o, k_o, v_o, lw_o, kk_o, b_o, g_o, bon_o, buf_ref) = refs
    else:
        (seg_ref, prev_ref, mu_ref, w0_ref, wup_ref, a0_ref, aup_ref, gup_ref, kk_ref, ka_ref, rk_ref,
         r_o, k_o, v_o, lw_o, kk_o, b_o, g_o, bon_o, buf_ref) = refs
    i = pl.program_id(0)
    seg = seg_ref[...]
    buf_ref[0:SUBLANES, :] = prev_ref[...]

    @pl.when(i % seq_blocks == 0)
    def _():
        buf_ref[0:SUBLANES, :] = jnp.zeros((SUBLANES, buf_ref.shape[1]), F32)

    buf_ref[SUBLANES:SUBLANES + tm, :] = seg
    shifted = buf_ref[SUBLANES - 1:SUBLANES - 1 + tm, :]
    xs = seg + (shifted - seg) * mu_ref[...]
    w = width
    r = xs[:, 0:w]
    k = xs[:, w:2 * w]
    v = xs[:, 2 * w:3 * w]
    low = xs[:, 3 * w:3 * w + LANES]
    g_low = xs[:, 3 * w + LANES:3 * w + 3 * LANES]
    wpre = w0_ref[...] + _dot(jnp.tanh(low).astype(BF16), wup_ref[...])
    wlog = -_softplus(-wpre) - 0.5
    lw_o[...] = -jnp.exp(wlog)
    a = jax.nn.sigmoid(a0_ref[...] + _dot(low.astype(BF16), aup_ref[...]))
    g_o[...] = _dot(jax.nn.sigmoid(g_low).astype(BF16), gup_ref[...]).astype(g_o.dtype)
    if has_vres:
        mix = jax.nn.sigmoid(v0_ref[...] + _dot(_dot(v.astype(BF16), vdn_ref[...]).astype(BF16), vup_ref[...]))
        v = v + (vf_ref[...] - v) * mix
    row = lax.broadcasted_iota(jnp.int32, (LANES, LANES), 0) // RWKV_HEAD
    col = lax.broadcasted_iota(jnp.int32, (LANES, LANES), 1) // RWKV_HEAD
    bd = (row == col).astype(BF16)
    kk = k * kk_ref[...]
    norm = jnp.sqrt(_head_sum(kk * kk, bd))
    kk = kk / jnp.maximum(norm, 1e-12)
    k2 = k * (1.0 + (a - 1.0) * ka_ref[...])
    bonus = _head_sum(r * k2 * rk_ref[...], bd) * v
    r_o[...] = r.astype(r_o.dtype)
    k_o[...] = k2.astype(k_o.dtype)
    v_o[...] = v
    kk_o[...] = kk.astype(kk_o.dtype)
    b_o[...] = (kk * a).astype(b_o.dtype)
    bon_o[...] = bonus.astype(bon_o.dtype)


def rwkv_prep(seg, seq, mu, w0, w_up, a0, a_up, g_up, k_k, k_a, r_k, v_first, v_res, tm=256):
    t, wpad = seg.shape
    w = w0.shape[0]
    has_vres = v_res is not None
    row = lambda i: (i, 0)
    fix = lambda i: (0, 0)
    vec = lambda x: x.reshape(1, -1)
    step = tm // SUBLANES
    in_specs = [
        pl.BlockSpec((tm, wpad), row),
        pl.BlockSpec((SUBLANES, wpad), lambda i: (jnp.maximum(i * step - 1, 0), 0)),
        pl.BlockSpec((1, wpad), fix), pl.BlockSpec((1, w), fix), pl.BlockSpec((LANES, w), fix),
        pl.BlockSpec((1, w), fix), pl.BlockSpec((LANES, w), fix), pl.BlockSpec((2 * LANES, w), fix),
        pl.BlockSpec((1, w), fix), pl.BlockSpec((1, w), fix), pl.BlockSpec((1, w), fix),
    ]
    args = [seg, seg, vec(mu), vec(w0), w_up, vec(a0), a_up, g_up, vec(k_k), vec(k_a), vec(r_k)]
    if has_vres:
        v0, v_down, v_up = v_res
        in_specs += [pl.BlockSpec((tm, w), row), pl.BlockSpec((1, w), fix),
                     pl.BlockSpec((w, LANES), fix), pl.BlockSpec((LANES, w), fix)]
        args += [v_first, vec(v0), v_down, v_up]
    outs = tuple(jax.ShapeDtypeStruct((t, w), dt) for dt in (BF16, BF16, F32, F32, BF16, BF16, BF16, BF16))
    return pl.pallas_call(
        functools.partial(_rwkv_prep_kernel, tm=tm, width=w, seq_blocks=seq // tm, has_vres=has_vres),
        out_shape=outs,
        grid=(t // tm,),
        in_specs=in_specs,
        out_specs=(pl.BlockSpec((tm, w), row),) * 8,
        scratch_shapes=[pltpu.VMEM((tm + SUBLANES, wpad), F32)],
        compiler_params=_cparams("parallel"),
        name="rwkv_prep",
    )(*args)


def _rwkv_chunk_kernel(r_ref, k_ref, v_ref, lw_ref, kk_ref, b_ref, g_ref, bon_ref, gnw_ref, gnb_ref, o_ref,
                       state_ref):
    c = pl.program_id(1)
    n = RWKV_CHUNK
    n2 = 2 * n

    @pl.when(c == 0)
    def _():
        state_ref[...] = jnp.zeros_like(state_ref)

    tri = (lax.broadcasted_iota(jnp.int32, (n, n), 1) <= lax.broadcasted_iota(jnp.int32, (n, n), 0)).astype(BF16)
    row2 = lax.broadcasted_iota(jnp.int32, (n2, n2), 0)
    col2 = lax.broadcasted_iota(jnp.int32, (n2, n2), 1)
    t2 = row2 & (n - 1)
    s2 = col2 & (n - 1)
    strict = s2 < t2
    incl = s2 <= t2
    eye = (row2 == col2).astype(F32)
    head0 = lax.broadcasted_iota(jnp.int32, (1, LANES), 1) < RWKV_HEAD
    own = jnp.concatenate([jnp.broadcast_to(head0, (n, LANES)), jnp.broadcast_to(~head0, (n, LANES))], axis=0)

    def stack(x):
        return jnp.where(own, jnp.concatenate([x, x], axis=0), 0.0)

    npairs = r_ref.shape[1] // LANES
    nchunks = r_ref.shape[0] // n
    sls = [slice(hp * LANES, (hp + 1) * LANES) for hp in range(npairs)]
    units = [(ci, hp) for ci in range(nchunks) for hp in range(npairs)]
    pre = []
    for ci in range(nchunks):
        rows = slice(ci * n, (ci + 1) * n)
        lw = lw_ref[rows, :]
        gsum = _dot_exact_rhs_left(tri, lw)
        p_inv = jnp.exp(-gsum)
        pre.append(dict(p_end=jnp.exp(gsum[n - 1:n, :]), rt=r_ref[rows, :] * jnp.exp(gsum), kt=k_ref[rows, :] * p_inv,
                        bt=b_ref[rows, :] * p_inv, kap=kk_ref[rows, :] * jnp.exp(gsum - lw), v=v_ref[rows, :]))
    rt = {u: stack(pre[u[0]]["rt"][:, sls[u[1]]]).astype(BF16) for u in units}
    kt = {u: stack(pre[u[0]]["kt"][:, sls[u[1]]]) for u in units}
    bt = {u: stack(pre[u[0]]["bt"][:, sls[u[1]]]) for u in units}
    kap = {u: stack(pre[u[0]]["kap"][:, sls[u[1]]]).astype(BF16) for u in units}
    v2 = {u: stack(pre[u[0]]["v"][:, sls[u[1]]]).astype(BF16) for u in units}
    prod = {u: _dot_nt(jnp.concatenate([kap[u], rt[u]], axis=0),
                       jnp.concatenate([bt[u].astype(BF16), kt[u].astype(BF16)], axis=0)) for u in units}
    a_ab = {u: jnp.where(strict, prod[u][:n2, :n2], 0.0) for u in units}
    a_ak = {u: jnp.where(strict, prod[u][:n2, n2:], 0.0).astype(BF16) for u in units}
    a_r = {u: jnp.concatenate([jnp.where(incl, prod[u][n2:, n2:], 0.0).astype(BF16),
                               jnp.where(incl, -prod[u][n2:, :n2], 0.0).astype(BF16)], axis=1) for u in units}
    x = {u: eye - a_ab[u] for u in units}
    q = {u: _dot(a_ab[u].astype(BF16), a_ab[u].astype(BF16)) for u in units}
    steps = 1
    while True:
        x = {u: x[u] + _dot(x[u].astype(BF16), q[u].astype(BF16)) for u in units}
        steps *= 2
        if steps * 2 >= n:
            break
        q = {u: _dot(q[u].astype(BF16), q[u].astype(BF16)) for u in units}
    inv_n = 1.0 / RWKV_HEAD
    state = [state_ref[hp] for hp in range(npairs)]
    for ci in range(nchunks):
        rows = slice(ci * n, (ci + 1) * n)
        us = [(ci, hp) for hp in range(npairs)]
        s0b = [s.astype(BF16) for s in state]
        rhs = [_dot_nt(kap[u], s0b[u[1]]) + _dot(a_ak[u], v2[u]) for u in us]
        u2 = [_dot(x[u].astype(BF16), r_.astype(BF16)).astype(BF16) for u, r_ in zip(us, rhs)]
        vu = [jnp.concatenate([v2[u], w_], axis=0) for u, w_ in zip(us, u2)]
        y2 = [_dot_nt(rt[u], s0b[u[1]]) + _dot(a_r[u], vu_) for u, vu_ in zip(us, vu)]
        outs = []
        for hp, u in enumerate(us):
            pe = pre[ci]["p_end"][:, sls[hp]]
            kb_end = jnp.concatenate([(kt[u] * pe).astype(BF16), (-(bt[u] * pe)).astype(BF16)], axis=0)
            state[hp] = state[hp] * pe + _dot_tn(vu[hp], kb_end)
            mean = jnp.sum(y2[hp], axis=-1, keepdims=True) * inv_n
            cen = jnp.where(own, y2[hp] - mean, 0.0)
            var = jnp.sum(cen * cen, axis=-1, keepdims=True) * inv_n
            yn2 = cen * lax.rsqrt(var + RWKV_GN_EPS)
            outs.append(yn2[:n] + yn2[n:])
        yn = jnp.concatenate(outs, axis=1) * gnw_ref[...] + gnb_ref[...]
        o_ref[rows, :] = ((yn + bon_ref[rows, :]) * g_ref[rows, :]).astype(o_ref.dtype)
    for hp in range(npairs):
        state_ref[hp] = state[hp]


def _dot_exact_rhs_left(m_bf16, x):
    hi, lo = _split_bf16(x)
    return _dot(m_bf16, hi) + _dot(m_bf16, lo)


def rwkv_chunks(r, k, v, lw, kk, b, g, bon, gn_w, gn_b, batch, seq):
    t, w = r.shape
    rows = RWKV_CHUNKS_PER_STEP * RWKV_CHUNK
    nc = seq // rows
    spec = pl.BlockSpec((rows, w), lambda bi, c: (bi * nc + c, 0))
    vspec = pl.BlockSpec((1, w), lambda bi, c: (0, 0))
    return pl.pallas_call(
        _rwkv_chunk_kernel,
        out_shape=jax.ShapeDtypeStruct((t, w), BF16),
        grid=(batch, nc),
        in_specs=[spec] * 8 + [vspec, vspec],
        out_specs=spec,
        scratch_shapes=[pltpu.VMEM((w // LANES, LANES, LANES), F32)],
        compiler_params=_cparams("parallel", "arbitrary"),
        name="rwkv_chunks",
    )(r, k, v, lw, kk, b, g, bon, gn_w.reshape(1, w), gn_b.reshape(1, w))


def _pad_rows(w, rows):
    return jnp.pad(w, ((0, rows - w.shape[0]), (0, 0)))


def _pad_cols(w, cols):
    return jnp.pad(w, ((0, 0), (0, cols - w.shape[1])))


def _rope_tables(seq):
    half = HEAD_DIM // 2
    inv_freq = ROPE_THETA ** (-jnp.arange(half, dtype=F32) / half)
    ang = jnp.arange(seq, dtype=F32)[:, None] * inv_freq[None, :]
    cos = jnp.cos(ang)
    sin = jnp.sin(ang)
    return jnp.concatenate([cos, cos], axis=1), jnp.concatenate([-sin, sin], axis=1)


def kernel(x, p, norm_mix_pre, norm_mix_post, norm_ffn_pre, norm_ffn_post, norm_ple_pre, norm_ple_post, w_in, w_merge_gate, lru_conv_w, lru_conv_b, lru_w_r, lru_b_r, lru_w_i, lru_b_i, lru_lambda, rwkv_mu, rwkv_w0, rwkv_w_up, rwkv_a0, rwkv_a_up, rwkv_g_up, rwkv_k_k, rwkv_k_a, rwkv_r_k, rwkv_gn_w, rwkv_gn_b, rwkv_v0, rwkv_v_down, rwkv_v_up, w_branch_a, w_branch_b, w_branch_c, w_branch_d, w_out, w_ffn_up, ffn_conv_w, ffn_conv_b, w_ffn_down, w_ple, w_ple_gate):
    batch, seq, d = x.shape
    depth = w_in.shape[0]
    t = batch * seq
    lru_w = lru_conv_w.shape[2]
    rw = rwkv_w0.shape[1]
    dil_w = 3 * len(DIL_CONFIGS) * DIL_HEADS * HEAD_DIM
    sb_w = 3 * (d // 2)
    off_b = 2 * lru_w
    off_c = off_b + dil_w
    off_d = off_c + sb_w
    rwkv_in = w_in.shape[2] - off_d
    rwkv_pad = 3 * rw + 4 * LANES
    cos, sin = _rope_tables(seq)

    xf = x.reshape(t, d)
    h = rmsnorm_bf16(xf, norm_mix_pre[0])
    v_first = None
    for i in range(depth):
        wi = w_in[i]
        w_a = wi[:, :off_b].astype(BF16)
        w_b = wi[:, off_b:off_c].astype(BF16)
        w_c = wi[:, off_c:off_d].astype(BF16)
        w_d = _pad_cols(wi[:, off_d:], rwkv_pad).astype(BF16)
        seg_a = matmul(h, w_a, F32)
        seg_b = matmul_rope(h, w_b, cos, sin, seq, 2 * dil_w // 3, BF16)
        seg_c = matmul(h, w_c, BF16)
        seg_d = matmul(h, w_d, F32, tm=512, tn=rwkv_pad // 2)
        y_a = rglru(seg_a.reshape(batch, seq, off_b), lru_conv_w[i], lru_conv_b[i], lru_w_r[i].astype(BF16),
                    lru_b_r[i], lru_w_i[i].astype(BF16), lru_b_i[i], lru_lambda[i]).reshape(t, lru_w)
        y_b = dilated_attention(seg_b, batch, seq)
        y_c = stick_breaking(seg_c, batch, seq)
        mu = jnp.pad(rwkv_mu[i], (0, rwkv_pad - rwkv_in))
        w_up = _pad_rows(rwkv_w_up[i], LANES).astype(BF16)
        a_up = jnp.pad(rwkv_a_up[i], ((RWKV_W_LORA, LANES - RWKV_W_LORA - RWKV_A_LORA), (0, 0))).astype(BF16)
        g_up = _pad_rows(rwkv_g_up[i], 2 * LANES).astype(BF16)
        v_res = None
        if i > 0:
            v_res = (rwkv_v0[i - 1], _pad_cols(rwkv_v_down[i - 1], LANES).astype(BF16),
                     _pad_rows(rwkv_v_up[i - 1], LANES).astype(BF16))
        r_, k_, v_, lw_, kk_, b_, g_, bon_ = rwkv_prep(
            seg_d, seq, mu, rwkv_w0[i], w_up, rwkv_a0[i], a_up, g_up, rwkv_k_k[i], rwkv_k_a[i],
            rwkv_r_k[i].reshape(-1), v_first, v_res)
        if i == 0:
            v_first = v_
        y_d = rwkv_chunks(r_, k_, v_, lw_, kk_, b_, g_, bon_, rwkv_gn_w[i], rwkv_gn_b[i], batch, seq)
        merged = merge_branches(
            h, (y_a, y_b, y_c, y_d), w_merge_gate[i].astype(BF16),
            (w_branch_a[i].astype(BF16), w_branch_b[i].astype(BF16), w_branch_c[i].astype(BF16),
             w_branch_d[i].astype(BF16)))
        xf, h = matmul_norm_res(merged, w_out[i].astype(BF16), xf, norm_mix_post[i], norm_ffn_pre[i], tm=512, tk=d)
        act = ffn_up(h, w_ffn_up[i].astype(BF16), ffn_conv_w[i], ffn_conv_b[i], seq)
        xf, h = matmul_norm_res(act, w_ffn_down[i].astype(BF16), xf, norm_ffn_post[i], norm_ple_pre[i],
                                tm=512, tk=w_ffn_down.shape[1] // 4)
        g_next = norm_mix_pre[i + 1] if i + 1 < depth else norm_mix_pre[i]
        xf, h = ple_norm_res(p[i].reshape(t, -1).astype(BF16), h, w_ple[i].astype(BF16),
                             w_ple_gate[i].astype(BF16), xf, norm_ple_post[i], g_next)
    return xf.reshape(batch, seq, d)
```

```python
import functools

import jax
import jax.numpy as jnp
from jax import lax
from jax.experimental import pallas as pl
from jax.experimental.pallas import tpu as pltpu

F32 = jnp.float32
BF16 = jnp.bfloat16

LANES = 128
SUBLANES = 8
VMEM_LIMIT_BYTES = 52 * 1024 * 1024

HEAD_DIM = 128
BLOCK = 128
ROPE_THETA = 10000.0
RMS_EPS = 1e-6
NEG_INF = -1e30

LRU_BLOCKS = 8
LRU_CONV = 4
LRU_C = 8.0
DIL_CONFIGS = ((128, 1), (512, 4), (2048, 16))
DIL_HEADS = 4
RWKV_HEAD = 64
RWKV_W_LORA = 64
RWKV_A_LORA = 64
RWKV_G_LORA = 160
RWKV_V_LORA = 32
RWKV_GN_EPS = 64e-5
RWKV_CHUNK = 64
RWKV_CHUNKS_PER_STEP = 2
FFN_CONV = 3
SB_KEYS = 256
EPILOGUE_ROWS = 256


def _cparams(*sem):
    return pltpu.CompilerParams(dimension_semantics=sem, vmem_limit_bytes=VMEM_LIMIT_BYTES)


def _dot(a, b):
    return jnp.dot(a, b, preferred_element_type=F32)


def _dot_nt(a, b):
    return lax.dot_general(a, b, (((1,), (1,)), ((), ())), preferred_element_type=F32)


def _dot_tn(a, b):
    return lax.dot_general(a, b, (((0,), (0,)), ((), ())), preferred_element_type=F32)


def _split_bf16(x):
    hi = x.astype(BF16)
    lo = (x - hi.astype(F32)).astype(BF16)
    return hi, lo


def _dot_exact_rhs(x, m_bf16):
    hi, lo = _split_bf16(x)
    return _dot(hi, m_bf16) + _dot(lo, m_bf16)


def _rms(x, g):
    return x * lax.rsqrt(jnp.mean(x * x, axis=-1, keepdims=True) + RMS_EPS) * g


def _gelu(x):
    return jax.nn.gelu(x, approximate=True)


def _softplus(x):
    return jnp.maximum(x, 0.0) + jnp.log1p(jnp.exp(-jnp.abs(x)))


def _row_tiles(rows, sub=EPILOGUE_ROWS):
    sub = min(sub, rows)
    return [slice(s, s + sub) for s in range(0, rows, sub)]


def _rmsnorm_kernel(x_ref, g_ref, o_ref):
    o_ref[...] = _rms(x_ref[...], g_ref[...]).astype(o_ref.dtype)


def rmsnorm_bf16(x, g, tm=512):
    t, d = x.shape
    return pl.pallas_call(
        _rmsnorm_kernel,
        out_shape=jax.ShapeDtypeStruct((t, d), BF16),
        grid=(t // tm,),
        in_specs=[pl.BlockSpec((tm, d), lambda i: (i, 0)), pl.BlockSpec((1, d), lambda i: (0, 0))],
        out_specs=pl.BlockSpec((tm, d), lambda i: (i, 0)),
        compiler_params=_cparams("parallel"),
        name="rmsnorm",
    )(x, g.reshape(1, d))


def _mm_kernel(a_ref, w_ref, o_ref):
    o_ref[...] = _dot(a_ref[...], w_ref[...]).astype(o_ref.dtype)


def _mm_rope_kernel(a_ref, w_ref, cos_ref, sin_ref, o_ref, *, n_rope_blocks, tn):
    j = pl.program_id(1)
    row_tiles = _row_tiles(a_ref.shape[0])

    @pl.when(j < n_rope_blocks)
    def _():
        for rows in row_tiles:
            acc = _dot(a_ref[rows, :], w_ref[...])
            cos = cos_ref[rows, :]
            sin = sin_ref[rows, :]
            for c in range(tn // HEAD_DIM):
                seg = acc[:, c * HEAD_DIM:(c + 1) * HEAD_DIM]
                rot = pltpu.roll(seg, HEAD_DIM // 2, axis=1)
                o_ref[rows, c * HEAD_DIM:(c + 1) * HEAD_DIM] = (seg * cos + rot * sin).astype(o_ref.dtype)

    @pl.when(j >= n_rope_blocks)
    def _():
        o_ref[...] = _dot(a_ref[...], w_ref[...]).astype(o_ref.dtype)


def matmul(a, w, out_dtype, tm=1024, tn=1024):
    m, k = a.shape
    n = w.shape[1]
    tm, tn = min(tm, m), min(tn, n)
    return pl.pallas_call(
        _mm_kernel,
        out_shape=jax.ShapeDtypeStruct((m, n), out_dtype),
        grid=(m // tm, n // tn),
        in_specs=[pl.BlockSpec((tm, k), lambda i, j: (i, 0)), pl.BlockSpec((k, tn), lambda i, j: (0, j))],
        out_specs=pl.BlockSpec((tm, tn), lambda i, j: (i, j)),
        compiler_params=_cparams("parallel", "arbitrary"),
        name="matmul_plain",
    )(a, w)


def matmul_rope(a, w, cos, sin, seq, n_rope_cols, out_dtype, tm=1024, tn=1536):
    m, k = a.shape
    n = w.shape[1]
    tm, tn = min(tm, m), min(tn, n)
    sblocks = seq // tm
    return pl.pallas_call(
        functools.partial(_mm_rope_kernel, n_rope_blocks=n_rope_cols // tn, tn=tn),
        out_shape=jax.ShapeDtypeStruct((m, n), out_dtype),
        grid=(m // tm, n // tn),
        in_specs=[
            pl.BlockSpec((tm, k), lambda i, j: (i, 0)),
            pl.BlockSpec((k, tn), lambda i, j: (0, j)),
            pl.BlockSpec((tm, HEAD_DIM), lambda i, j: (i % sblocks, 0)),
            pl.BlockSpec((tm, HEAD_DIM), lambda i, j: (i % sblocks, 0)),
        ],
        out_specs=pl.BlockSpec((tm, tn), lambda i, j: (i, j)),
        compiler_params=_cparams("parallel", "arbitrary"),
        name="matmul_rope",
    )(a, w, cos, sin)


def _mm_norm_res_kernel(a_ref, w_ref, x_ref, gpost_ref, gnext_ref, xo_ref, ho_ref, *acc, nk):
    row_tiles = _row_tiles(a_ref.shape[0])

    def finish(rows, val):
        xn = x_ref[rows, :] + _rms(val, gpost_ref[...])
        xo_ref[rows, :] = xn
        ho_ref[rows, :] = _rms(xn, gnext_ref[...]).astype(ho_ref.dtype)

    if nk == 1:
        for rows in row_tiles:
            finish(rows, _dot(a_ref[rows, :], w_ref[...]))
        return
    acc_ref, = acc
    kk = pl.program_id(1)

    @pl.when(kk == 0)
    def _():
        acc_ref[...] = _dot(a_ref[...], w_ref[...])

    @pl.when((kk > 0) & (kk < nk - 1))
    def _():
        acc_ref[...] += _dot(a_ref[...], w_ref[...])

    @pl.when(kk == nk - 1)
    def _():
        for rows in row_tiles:
            finish(rows, acc_ref[rows, :] + _dot(a_ref[rows, :], w_ref[...]))


def matmul_norm_res(a, w, x, g_post, g_next, tm=256, tk=512):
    m, k = a.shape
    d = w.shape[1]
    tm = min(tm, m)
    nk = k // tk
    return pl.pallas_call(
        functools.partial(_mm_norm_res_kernel, nk=nk),
        out_shape=(jax.ShapeDtypeStruct((m, d), F32), jax.ShapeDtypeStruct((m, d), BF16)),
        grid=(m // tm, nk),
        in_specs=[
            pl.BlockSpec((tm, tk), lambda i, kk: (i, kk)),
            pl.BlockSpec((tk, d), lambda i, kk: (kk, 0)),
            pl.BlockSpec((tm, d), lambda i, kk: (i, 0)),
            pl.BlockSpec((1, d), lambda i, kk: (0, 0)),
            pl.BlockSpec((1, d), lambda i, kk: (0, 0)),
        ],
        out_specs=(pl.BlockSpec((tm, d), lambda i, kk: (i, 0)), pl.BlockSpec((tm, d), lambda i, kk: (i, 0))),
        scratch_shapes=[pltpu.VMEM((tm, d), F32)] if nk > 1 else [],
        compiler_params=_cparams("parallel", "arbitrary"),
        name="matmul_norm_res",
    )(a, w, x, g_post.reshape(1, d), g_next.reshape(1, d))


def _ple_kernel(p_ref, h_ref, wp_ref, wg_ref, x_ref, gpost_ref, gnext_ref, xo_ref, ho_ref):
    for rows in _row_tiles(x_ref.shape[0]):
        val = _dot(p_ref[rows, :], wp_ref[...]) * jax.nn.sigmoid(_dot(h_ref[rows, :], wg_ref[...]))
        xn = x_ref[rows, :] + _rms(val, gpost_ref[...])
        xo_ref[rows, :] = xn
        ho_ref[rows, :] = _rms(xn, gnext_ref[...]).astype(ho_ref.dtype)


def ple_norm_res(p, h, w_ple, w_gate, x, g_post, g_next, tm=512):
    m, d = x.shape
    tm = min(tm, m)
    pd = p.shape[1]
    row = lambda i: (i, 0)
    fix = lambda i: (0, 0)
    return pl.pallas_call(
        _ple_kernel,
        out_shape=(jax.ShapeDtypeStruct((m, d), F32), jax.ShapeDtypeStruct((m, d), BF16)),
        grid=(m // tm,),
        in_specs=[
            pl.BlockSpec((tm, pd), row), pl.BlockSpec((tm, d), row),
            pl.BlockSpec((pd, d), fix), pl.BlockSpec((d, d), fix),
            pl.BlockSpec((tm, d), row), pl.BlockSpec((1, d), fix), pl.BlockSpec((1, d), fix),
        ],
        out_specs=(pl.BlockSpec((tm, d), row), pl.BlockSpec((tm, d), row)),
        compiler_params=_cparams("parallel"),
        name="ple_norm_res",
    )(p, h, w_ple, w_gate, x, g_post.reshape(1, d), g_next.reshape(1, d))


def _merge_kernel(h_ref, ya_ref, yb_ref, yc_ref, yd_ref, ga_ref, gb_ref, gc_ref, gd_ref,
                  wa_ref, wb_ref, wc_ref, wd_ref, o_ref):
    h = h_ref[...]
    acc = jax.nn.sigmoid(_dot(h, ga_ref[...])) * _dot(ya_ref[...], wa_ref[...])
    acc += jax.nn.sigmoid(_dot(h, gb_ref[...])) * _dot(yb_ref[...], wb_ref[...])
    acc += jax.nn.sigmoid(_dot(h, gc_ref[...])) * _dot(yc_ref[...], wc_ref[...])
    acc += jax.nn.sigmoid(_dot(h, gd_ref[...])) * _dot(yd_ref[...], wd_ref[...])
    o_ref[...] = acc.astype(o_ref.dtype)


def merge_branches(h, ys, w_gate, ws, tm=1024, tn=256):
    m, dm = h.shape
    tm = min(tm, m)
    d = ws[0].shape[1]
    nb = d // tn
    in_specs = [pl.BlockSpec((tm, dm), lambda i, j: (i, 0))]
    in_specs += [pl.BlockSpec((tm, y.shape[1]), lambda i, j: (i, 0)) for y in ys]
    in_specs += [pl.BlockSpec((dm, tn), functools.partial(lambda i, j, b: (0, b * nb + j), b=b)) for b in range(4)]
    in_specs += [pl.BlockSpec((w.shape[0], tn), lambda i, j: (0, j)) for w in ws]
    return pl.pallas_call(
        _merge_kernel,
        out_shape=jax.ShapeDtypeStruct((m, d), BF16),
        grid=(m // tm, nb),
        in_specs=in_specs,
        out_specs=pl.BlockSpec((tm, tn), lambda i, j: (i, j)),
        compiler_params=_cparams("parallel", "arbitrary"),
        name="merge_branches",
    )(h, *ys, w_gate, w_gate, w_gate, w_gate, *ws)


def _ffn_up_kernel(h_ref, wg_ref, wu_ref, cwg_ref, cwu_ref, cbg_ref, cbu_ref, o_ref, bufg_ref, bufu_ref,
                   *, tm, seq_blocks):
    i = pl.program_id(1)

    @pl.when(i % seq_blocks == 0)
    def _():
        bufg_ref[0:SUBLANES, :] = jnp.zeros((SUBLANES, bufg_ref.shape[1]), F32)
        bufu_ref[0:SUBLANES, :] = jnp.zeros((SUBLANES, bufu_ref.shape[1]), F32)

    h = h_ref[...]

    def conv(w_ref, cw_ref, cb_ref, buf_ref):
        buf_ref[SUBLANES:SUBLANES + tm, :] = _dot(h, w_ref[...])
        cw = cw_ref[...]
        out = cb_ref[...] + cw[2:3, :] * buf_ref[SUBLANES:SUBLANES + tm, :]
        out += cw[1:2, :] * buf_ref[SUBLANES - 1:SUBLANES - 1 + tm, :]
        out += cw[0:1, :] * buf_ref[SUBLANES - 2:SUBLANES - 2 + tm, :]
        buf_ref[0:SUBLANES, :] = buf_ref[tm:tm + SUBLANES, :]
        return out

    g = conv(wg_ref, cwg_ref, cbg_ref, bufg_ref)
    u = conv(wu_ref, cwu_ref, cbu_ref, bufu_ref)
    o_ref[...] = (_gelu(g) * u).astype(o_ref.dtype)


def ffn_up(h, w_up, conv_w, conv_b, seq, tm=1024, tn=512):
    m, d = h.shape
    tm = min(tm, seq)
    dff = w_up.shape[1] // 2
    nb = dff // tn
    cb = conv_b.reshape(1, 2 * dff)
    return pl.pallas_call(
        functools.partial(_ffn_up_kernel, tm=tm, seq_blocks=seq // tm),
        out_shape=jax.ShapeDtypeStruct((m, dff), BF16),
        grid=(nb, m // tm),
        in_specs=[
            pl.BlockSpec((tm, d), lambda j, i: (i, 0)),
            pl.BlockSpec((d, tn), lambda j, i: (0, j)),
            pl.BlockSpec((d, tn), lambda j, i: (0, j + nb)),
            pl.BlockSpec((FFN_CONV, tn), lambda j, i: (0, j)),
            pl.BlockSpec((FFN_CONV, tn), lambda j, i: (0, j + nb)),
            pl.BlockSpec((1, tn), lambda j, i: (0, j)),
            pl.BlockSpec((1, tn), lambda j, i: (0, j + nb)),
        ],
        out_specs=pl.BlockSpec((tm, tn), lambda j, i: (i, j)),
        scratch_shapes=[pltpu.VMEM((tm + SUBLANES, tn), F32), pltpu.VMEM((tm + SUBLANES, tn), F32)],
        compiler_params=_cparams("parallel", "arbitrary"),
        name="ffn_up_conv_glu",
    )(h, w_up, w_up, conv_w, conv_w, cb, cb)


def _lru_kernel(x_ref, gate_ref, cw_ref, cb_ref, wr_ref, br_ref, wi_ref, bi_ref, lam_ref, o_ref,
                xbuf_ref, a0_ref, h0_ref, a1_ref, h1_ref, *, seq, pad):
    xbuf_ref[0:SUBLANES, :] = jnp.zeros((SUBLANES, LANES), F32)
    xbuf_ref[SUBLANES:SUBLANES + seq, :] = x_ref[0]
    cw = cw_ref[...]
    u = cb_ref[...] + cw[3:4, :] * xbuf_ref[SUBLANES:SUBLANES + seq, :]
    for k in range(LRU_CONV - 1):
        off = SUBLANES - (LRU_CONV - 1) + k
        u += cw[k:k + 1, :] * xbuf_ref[off:off + seq, :]
    ub = u.astype(BF16)
    r = jax.nn.sigmoid(_dot(ub, wr_ref[0]) + br_ref[...])
    ig = jax.nn.sigmoid(_dot(ub, wi_ref[0]) + bi_ref[...])
    log_a = -LRU_C * r * _softplus(-lam_ref[...])
    a = jnp.exp(log_a)
    inp = jnp.sqrt(1.0 - a * a) * ig * u

    ones = jnp.ones((pad, LANES), F32)
    zeros = jnp.zeros((pad, LANES), F32)
    a0_ref[0:pad, :] = ones
    a1_ref[0:pad, :] = ones
    h0_ref[0:pad, :] = zeros
    h1_ref[0:pad, :] = zeros
    a0_ref[pad:pad + seq, :] = a
    h0_ref[pad:pad + seq, :] = inp
    bufs = ((a0_ref, h0_ref), (a1_ref, h1_ref))
    d = 1
    level = 0
    while d < seq:
        (a_src, h_src), (a_dst, h_dst) = bufs[level % 2], bufs[(level + 1) % 2]
        a_cur = a_src[pad:pad + seq, :]
        h_dst[pad:pad + seq, :] = h_src[pad:pad + seq, :] + a_cur * h_src[pad - d:pad - d + seq, :]
        a_dst[pad:pad + seq, :] = a_cur * a_src[pad - d:pad - d + seq, :]
        d *= 2
        level += 1
    h = bufs[level % 2][1][pad:pad + seq, :]
    o_ref[0] = (h * _gelu(gate_ref[0])).astype(o_ref.dtype)


def rglru(xg, conv_w, conv_b, w_r, b_r, w_i, b_i, lam):
    b, s, w2 = xg.shape
    w = w2 // 2
    nblk = w // LANES
    pad = s
    vec = lambda v: v.reshape(1, w)
    vspec = pl.BlockSpec((1, LANES), lambda bi, c: (0, c))
    return pl.pallas_call(
        functools.partial(_lru_kernel, seq=s, pad=pad),
        out_shape=jax.ShapeDtypeStruct((b, s, w), BF16),
        grid=(b, nblk),
        in_specs=[
            pl.BlockSpec((1, s, LANES), lambda bi, c: (bi, 0, c)),
            pl.BlockSpec((1, s, LANES), lambda bi, c: (bi, 0, c + nblk)),
            pl.BlockSpec((LRU_CONV, LANES), lambda bi, c: (0, c)),
            vspec,
            pl.BlockSpec((1, LANES, LANES), lambda bi, c: (c, 0, 0)),
            vspec,
            pl.BlockSpec((1, LANES, LANES), lambda bi, c: (c, 0, 0)),
            vspec, vspec,
        ],
        out_specs=pl.BlockSpec((1, s, LANES), lambda bi, c: (bi, 0, c)),
        scratch_shapes=[pltpu.VMEM((s + SUBLANES, LANES), F32)] + [pltpu.VMEM((pad + s, LANES), F32)] * 4,
        compiler_params=_cparams("parallel", "parallel"),
        name="rglru",
    )(xg, xg, conv_w, vec(conv_b), w_r, vec(b_r), w_i, vec(b_i), vec(lam))


DIL_UNITS_PER_PHASE = 4


def _dil_kernel(*refs, seq):
    ngroups = len(DIL_CONFIGS)
    qkv_refs = refs[:3 * ngroups]
    o_ref = refs[3 * ngroups]
    qf_ref, kf_ref, vf_ref, acc_ref, m_ref, l_ref = refs[3 * ngroups + 1:]
    scale = HEAD_DIM ** -0.5
    row = lax.broadcasted_iota(jnp.int32, (BLOCK, 2 * BLOCK), 0)
    col = lax.broadcasted_iota(jnp.int32, (BLOCK, 2 * BLOCK), 1)
    bias_two = jnp.where((col >= row) & (col <= row + BLOCK), 0.0, NEG_INF)
    bias_own = jnp.where(lax.broadcasted_iota(jnp.int32, (BLOCK, BLOCK), 1)
                         <= lax.broadcasted_iota(jnp.int32, (BLOCK, BLOCK), 0), 0.0, NEG_INF)

    def run_units(g, units):
        s = [_dot_nt(q, k) * scale + (bias_own if k.shape[0] == BLOCK else bias_two) for q, k, _, _ in units]
        mx = [jnp.max(x, axis=-1, keepdims=True) for x in s]
        e = [jnp.exp(x - m).astype(BF16) for x, m in zip(s, mx)]
        pv = [_dot(p, jnp.concatenate([v, jnp.ones_like(v)], axis=1)) for p, (_, _, v, _) in zip(e, units)]
        for y, m, (_, _, _, rows) in zip(pv, mx, units):
            acc_ref[g, rows, :] = y[:, :HEAD_DIM]
            l_ref[g, rows, :] = y[:, HEAD_DIM:]
            m_ref[g, rows, :] = jnp.broadcast_to(m, (BLOCK, HEAD_DIM))

    for g, (_, d) in enumerate(DIL_CONFIGS):
        q_ref, k_ref, v_ref = qkv_refs[3 * g:3 * g + 3]
        ln = seq // d
        nblk = ln // BLOCK
        if d > 1:
            qf_ref[...] = q_ref[...].astype(F32)
            kf_ref[...] = k_ref[...].astype(F32)
            vf_ref[...] = v_ref[...].astype(F32)
        units = []
        for r in range(d):
            if d > 1:
                qr = qf_ref[pl.ds(r, ln, stride=d), :].astype(BF16)
                kr = kf_ref[pl.ds(r, ln, stride=d), :].astype(BF16)
                vr = vf_ref[pl.ds(r, ln, stride=d), :].astype(BF16)
            for nb in range(nblk):
                lo = max(nb - 1, 0) * BLOCK
                hi = (nb + 1) * BLOCK
                if d > 1:
                    unit = (qr[nb * BLOCK:hi], kr[lo:hi], vr[lo:hi], pl.ds(nb * BLOCK * d + r, BLOCK, stride=d))
                else:
                    unit = (q_ref[nb * BLOCK:hi, :], k_ref[lo:hi, :], v_ref[lo:hi, :], pl.ds(nb * BLOCK, BLOCK))
                units.append(unit)
                if len(units) == DIL_UNITS_PER_PHASE:
                    run_units(g, units)
                    units = []
        if units:
            run_units(g, units)

    rows_per_step = 2 * BLOCK
    for c in range(seq // rows_per_step):
        rs = slice(c * rows_per_step, (c + 1) * rows_per_step)
        ms = [m_ref[g, rs, :] for g in range(ngroups)]
        top = functools.reduce(jnp.maximum, ms)
        ws = [jnp.exp(m - top) for m in ms]
        num = functools.reduce(jnp.add, [w * acc_ref[g, rs, :] for g, w in enumerate(ws)])
        den = functools.reduce(jnp.add, [w * l_ref[g, rs, :] for g, w in enumerate(ws)])
        o_ref[rs, :] = (num / den).astype(o_ref.dtype)


def dilated_attention(qkv, batch, seq):
    assert all(window // d == BLOCK and seq % (BLOCK * d) == 0 for window, d in DIL_CONFIGS)
    t, width = qkv.shape
    ngroups = len(DIL_CONFIGS)
    nheads = width // (3 * HEAD_DIM)

    def spec(which, g):
        return pl.BlockSpec((seq, HEAD_DIM), lambda b, h: (b, which * nheads + g * DIL_HEADS + h))

    in_specs = [spec(which, g) for g in range(ngroups) for which in range(3)]
    return pl.pallas_call(
        functools.partial(_dil_kernel, seq=seq),
        out_shape=jax.ShapeDtypeStruct((t, DIL_HEADS * HEAD_DIM), BF16),
        grid=(batch, DIL_HEADS),
        in_specs=in_specs,
        out_specs=pl.BlockSpec((seq, HEAD_DIM), lambda b, h: (b, h)),
        scratch_shapes=[pltpu.VMEM((seq, HEAD_DIM), F32)] * 3 + [pltpu.VMEM((ngroups, seq, HEAD_DIM), F32)] * 3,
        compiler_params=_cparams("parallel", "parallel"),
        name="dilated_attention",
    )(*([qkv] * (3 * ngroups)))


def _sb_kernel(q_ref, k_ref, v_ref, o_ref, acc_ref, run_ref):
    n = pl.program_id(1)
    nh = q_ref.shape[1] // HEAD_DIM
    heads = range(nh)
    sls = [slice(h * HEAD_DIM, (h + 1) * HEAD_DIM) for h in heads]
    scale = HEAD_DIM ** -0.5
    kb = SB_KEYS
    assert kb == 2 * BLOCK
    from_s = (lax.broadcasted_iota(jnp.int32, (kb, kb), 0) >= lax.broadcasted_iota(jnp.int32, (kb, kb), 1)).astype(BF16)
    off = (n % 2) * BLOCK
    before = (lax.broadcasted_iota(jnp.int32, (BLOCK, kb), 1) - lax.broadcasted_iota(jnp.int32, (BLOCK, kb), 0)) < off
    q = [q_ref[:, sl] for sl in sls]

    def block(jb, diag):
        start = pl.multiple_of(jb * kb, kb)
        z = [_dot_nt(q[h], k_ref[pl.ds(start, kb), sls[h]]) * scale for h in heads]
        log_beta = [jnp.minimum(zz, 0.0) - jnp.log(1.0 + jnp.exp(-jnp.abs(zz))) for zz in z]
        log_1m = [lb - zz for lb, zz in zip(log_beta, z)]
        if diag:
            log_1m = [jnp.where(before, x, 0.0) for x in log_1m]
        incl = [_dot_exact_rhs(x, from_s) for x in log_1m]
        for h in heads:
            att = jnp.exp(z[h] + incl[h]) if diag else jnp.exp(z[h] + incl[h] + run_ref[h])
            if diag:
                att = jnp.where(before, att, 0.0)
            pv = _dot(att.astype(BF16), v_ref[pl.ds(start, kb), sls[h]])
            total = jnp.broadcast_to(incl[h][:, 0:1], (BLOCK, kb))
            if diag:
                acc_ref[:, sls[h]] = pv
                run_ref[h] = total
            else:
                acc_ref[:, sls[h]] += pv
                run_ref[h] = run_ref[h] + total

    full = n // 2
    block(full, True)

    def body(t, carry):
        block(full - 1 - t, False)
        return carry

    lax.fori_loop(0, full, body, 0)
    o_ref[...] = acc_ref[...].astype(o_ref.dtype)


def stick_breaking(qkv, batch, seq):
    t, width = qkv.shape
    w = width // 3
    nq = seq // BLOCK
    assert seq % SB_KEYS == 0
    return pl.pallas_call(
        _sb_kernel,
        out_shape=jax.ShapeDtypeStruct((t, w), BF16),
        grid=(batch, nq),
        in_specs=[
            pl.BlockSpec((BLOCK, w), lambda b, n: (b * nq + n, 0)),
            pl.BlockSpec((seq, w), lambda b, n: (b, 1)),
            pl.BlockSpec((seq, w), lambda b, n: (b, 2)),
        ],
        out_specs=pl.BlockSpec((BLOCK, w), lambda b, n: (b * nq + n, 0)),
        scratch_shapes=[pltpu.VMEM((BLOCK, w), F32), pltpu.VMEM((w // HEAD_DIM, BLOCK, SB_KEYS), F32)],
        compiler_params=_cparams("parallel", "arbitrary"),
        name="stick_breaking",
    )(qkv, qkv, qkv)


def _head_sum(x, bd):
    cols = []
    for c in range(x.shape[1] // LANES):
        cols.append(_dot_exact_rhs(x[:, c * LANES:(c + 1) * LANES], bd))
    return jnp.concatenate(cols, axis=1)


def _rwkv_prep_kernel(*refs, tm, width, seq_blocks, has_vres):
    if has_vres:
        (seg_ref, prev_ref, mu_ref, w0_ref, wup_ref, a0_ref, aup_ref, gup_ref, kk_ref, ka_ref, rk_ref,
         vf_ref, v0_ref, vdn_ref, vup_ref,
         r_o, k_o, v_o, lw_o, kk_o, b_o, g_o, bon_o, buf_ref) = refs
    else:
        (seg_ref, prev_ref, mu_ref, w0_ref, wup_ref, a0_ref, aup_ref, gup_ref, kk_ref, ka_ref, rk_ref,
         r_o, k_o, v_o, lw_o, kk_o, b_o, g_o, bon_o, buf_ref) = refs
    i = pl.program_id(0)
    seg = seg_ref[...]
    buf_ref[0:SUBLANES, :] = prev_ref[...]

    @pl.when(i % seq_blocks == 0)
    def _():
        buf_ref[0:SUBLANES, :] = jnp.zeros((SUBLANES, buf_ref.shape[1]), F32)

    buf_ref[SUBLANES:SUBLANES + tm, :] = seg
    shifted = buf_ref[SUBLANES - 1:SUBLANES - 1 + tm, :]
    xs = seg + (shifted - seg) * mu_ref[...]
    w = width
    r = xs[:, 0:w]
    k = xs[:, w:2 * w]
    v = xs[:, 2 * w:3 * w]
    low = xs[:, 3 * w:3 * w + LANES]
    g_low = xs[:, 3 * w + LANES:3 * w + 3 * LANES]
    wpre = w0_ref[...] + _dot(jnp.tanh(low).astype(BF16), wup_ref[...])
    wlog = -_softplus(-wpre) - 0.5
    lw_o[...] = -jnp.exp(wlog)
    a = jax.nn.sigmoid(a0_ref[...] + _dot(low.astype(BF16), aup_ref[...]))
    g_o[...] = _dot(jax.nn.sigmoid(g_low).astype(BF16), gup_ref[...]).astype(g_o.dtype)
    if has_vres:
        mix = jax.nn.sigmoid(v0_ref[...] + _dot(_dot(v.astype(BF16), vdn_ref[...]).astype(BF16), vup_ref[...]))
        v = v + (vf_ref[...] - v) * mix
    row = lax.broadcasted_iota(jnp.int32, (LANES, LANES), 0) // RWKV_HEAD
    col = lax.broadcasted_iota(jnp.int32, (LANES, LANES), 1) // RWKV_HEAD
    bd = (row == col).astype(BF16)
    kk = k * kk_ref[...]
    norm = jnp.sqrt(_head_sum(kk * kk, bd))
    kk = kk / jnp.maximum(norm, 1e-12)
    k2 = k * (1.0 + (a - 1.0) * ka_ref[...])
    bonus = _head_sum(r * k2 * rk_ref[...], bd) * v
    r_o[...] = r.astype(r_o.dtype)
    k_o[...] = k2.astype(k_o.dtype)
    v_o[...] = v
    kk_o[...] = kk.astype(kk_o.dtype)
    b_o[...] = (kk * a).astype(b_o.dtype)
    bon_o[...] = bonus.astype(bon_o.dtype)


def rwkv_prep(seg, seq, mu, w0, w_up, a0, a_up, g_up, k_k, k_a, r_k, v_first, v_res, tm=256):
    t, wpad = seg.shape
    w = w0.shape[0]
    has_vres = v_res is not None
    row = lambda i: (i, 0)
    fix = lambda i: (0, 0)
    vec = lambda x: x.reshape(1, -1)
    step = tm // SUBLANES
    in_specs = [
        pl.BlockSpec((tm, wpad), row),
        pl.BlockSpec((SUBLANES, wpad), lambda i: (jnp.maximum(i * step - 1, 0), 0)),
        pl.BlockSpec((1, wpad), fix), pl.BlockSpec((1, w), fix), pl.BlockSpec((LANES, w), fix),
        pl.BlockSpec((1, w), fix), pl.BlockSpec((LANES, w), fix), pl.BlockSpec((2 * LANES, w), fix),
        pl.BlockSpec((1, w), fix), pl.BlockSpec((1, w), fix), pl.BlockSpec((1, w), fix),
    ]
    args = [seg, seg, vec(mu), vec(w0), w_up, vec(a0), a_up, g_up, vec(k_k), vec(k_a), vec(r_k)]
    if has_vres:
        v0, v_down, v_up = v_res
        in_specs += [pl.BlockSpec((tm, w), row), pl.BlockSpec((1, w), fix),
                     pl.BlockSpec((w, LANES), fix), pl.BlockSpec((LANES, w), fix)]
        args += [v_first, vec(v0), v_down, v_up]
    outs = tuple(jax.ShapeDtypeStruct((t, w), dt) for dt in (BF16, BF16, F32, F32, BF16, BF16, BF16, BF16))
    return pl.pallas_call(
        functools.partial(_rwkv_prep_kernel, tm=tm, width=w, seq_blocks=seq // tm, has_vres=has_vres),
        out_shape=outs,
        grid=(t // tm,),
        in_specs=in_specs,
        out_specs=(pl.BlockSpec((tm, w), row),) * 8,
        scratch_shapes=[pltpu.VMEM((tm + SUBLANES, wpad), F32)],
        compiler_params=_cparams("parallel"),
        name="rwkv_prep",
    )(*args)


def _rwkv_chunk_kernel(r_ref, k_ref, v_ref, lw_ref, kk_ref, b_ref, g_ref, bon_ref, gnw_ref, gnb_ref, o_ref,
                       state_ref):
    c = pl.program_id(1)
    n = RWKV_CHUNK
    n2 = 2 * n

    @pl.when(c == 0)
    def _():
        state_ref[...] = jnp.zeros_like(state_ref)

    tri = (lax.broadcasted_iota(jnp.int32, (n, n), 1) <= lax.broadcasted_iota(jnp.int32, (n, n), 0)).astype(BF16)
    row2 = lax.broadcasted_iota(jnp.int32, (n2, n2), 0)
    col2 = lax.broadcasted_iota(jnp.int32, (n2, n2), 1)
    t2 = row2 & (n - 1)
    s2 = col2 & (n - 1)
    strict = s2 < t2
    incl = s2 <= t2
    eye = (row2 == col2).astype(F32)
    head0 = lax.broadcasted_iota(jnp.int32, (1, LANES), 1) < RWKV_HEAD
    own = jnp.concatenate([jnp.broadcast_to(head0, (n, LANES)), jnp.broadcast_to(~head0, (n, LANES))], axis=0)

    def stack(x):
        return jnp.where(own, jnp.concatenate([x, x], axis=0), 0.0)

    npairs = r_ref.shape[1] // LANES
    nchunks = r_ref.shape[0] // n
    sls = [slice(hp * LANES, (hp + 1) * LANES) for hp in range(npairs)]
    units = [(ci, hp) for ci in range(nchunks) for hp in range(npairs)]
    pre = []
    for ci in range(nchunks):
        rows = slice(ci * n, (ci + 1) * n)
        lw = lw_ref[rows, :]
        gsum = _dot_exact_rhs_left(tri, lw)
        p_inv = jnp.exp(-gsum)
        pre.append(dict(p_end=jnp.exp(gsum[n - 1:n, :]), rt=r_ref[rows, :] * jnp.exp(gsum), kt=k_ref[rows, :] * p_inv,
                        bt=b_ref[rows, :] * p_inv, kap=kk_ref[rows, :] * jnp.exp(gsum - lw), v=v_ref[rows, :]))
    rt = {u: stack(pre[u[0]]["rt"][:, sls[u[1]]]).astype(BF16) for u in units}
    kt = {u: stack(pre[u[0]]["kt"][:, sls[u[1]]]) for u in units}
    bt = {u: stack(pre[u[0]]["bt"][:, sls[u[1]]]) for u in units}
    kap = {u: stack(pre[u[0]]["kap"][:, sls[u[1]]]).astype(BF16) for u in units}
    v2 = {u: stack(pre[u[0]]["v"][:, sls[u[1]]]).astype(BF16) for u in units}
    prod = {u: _dot_nt(jnp.concatenate([kap[u], rt[u]], axis=0),
                       jnp.concatenate([bt[u].astype(BF16), kt[u].astype(BF16)], axis=0)) for u in units}
    a_ab = {u: jnp.where(strict, prod[u][:n2, :n2], 0.0) for u in units}
    a_ak = {u: jnp.where(strict, prod[u][:n2, n2:], 0.0).astype(BF16) for u in units}
    a_r = {u: jnp.concatenate([jnp.where(incl, prod[u][n2:, n2:], 0.0).astype(BF16),
                               jnp.where(incl, -prod[u][n2:, :n2], 0.0).astype(BF16)], axis=1) for u in units}
    x = {u: eye - a_ab[u] for u in units}
    q = {u: _dot(a_ab[u].astype(BF16), a_ab[u].astype(BF16)) for u in units}
    steps = 1
    while True:
        x = {u: x[u] + _dot(x[u].astype(BF16), q[u].astype(BF16)) for u in units}
        steps *= 2
        if steps * 2 >= n:
            break
        q = {u: _dot(q[u].astype(BF16), q[u].astype(BF16)) for u in units}
    inv_n = 1.0 / RWKV_HEAD
    state = [state_ref[hp] for hp in range(npairs)]
    for ci in range(nchunks):
        rows = slice(ci * n, (ci + 1) * n)
        us = [(ci, hp) for hp in range(npairs)]
        s0b = [s.astype(BF16) for s in state]
        rhs = [_dot_nt(kap[u], s0b[u[1]]) + _dot(a_ak[u], v2[u]) for u in us]
        u2 = [_dot(x[u].astype(BF16), r_.astype(BF16)).astype(BF16) for u, r_ in zip(us, rhs)]
        vu = [jnp.concatenate([v2[u], w_], axis=0) for u, w_ in zip(us, u2)]
        y2 = [_dot_nt(rt[u], s0b[u[1]]) + _dot(a_r[u], vu_) for u, vu_ in zip(us, vu)]
        outs = []
        for hp, u in enumerate(us):
            pe = pre[ci]["p_end"][:, sls[hp]]
            kb_end = jnp.concatenate([(kt[u] * pe).astype(BF16), (-(bt[u] * pe)).astype(BF16)], axis=0)
            state[hp] = state[hp] * pe + _dot_tn(vu[hp], kb_end)
            mean = jnp.sum(y2[hp], axis=-1, keepdims=True) * inv_n
            cen = jnp.where(own, y2[hp] - mean, 0.0)
            var = jnp.sum(cen * cen, axis=-1, keepdims=True) * inv_n
            yn2 = cen * lax.rsqrt(var + RWKV_GN_EPS)
            outs.append(yn2[:n] + yn2[n:])
        yn = jnp.concatenate(outs, axis=1) * gnw_ref[...] + gnb_ref[...]
        o_ref[rows, :] = ((yn + bon_ref[rows, :]) * g_ref[rows, :]).astype(o_ref.dtype)
    for hp in range(npairs):
        state_ref[hp] = state[hp]


def _dot_exact_rhs_left(m_bf16, x):
    hi, lo = _split_bf16(x)
    return _dot(m_bf16, hi) + _dot(m_bf16, lo)


def rwkv_chunks(r, k, v, lw, kk, b, g, bon, gn_w, gn_b, batch, seq):
    t, w = r.shape
    rows = RWKV_CHUNKS_PER_STEP * RWKV_CHUNK
    nc = seq // rows
    spec = pl.BlockSpec((rows, w), lambda bi, c: (bi * nc + c, 0))
    vspec = pl.BlockSpec((1, w), lambda bi, c: (0, 0))
    return pl.pallas_call(
        _rwkv_chunk_kernel,
        out_shape=jax.ShapeDtypeStruct((t, w), BF16),
        grid=(batch, nc),
        in_specs=[spec] * 8 + [vspec, vspec],
        out_specs=spec,
        scratch_shapes=[pltpu.VMEM((w // LANES, LANES, LANES), F32)],
        compiler_params=_cparams("parallel", "arbitrary"),
        name="rwkv_chunks",
    )(r, k, v, lw, kk, b, g, bon, gn_w.reshape(1, w), gn_b.reshape(1, w))


def _pad_rows(w, rows):
    return jnp.pad(w, ((0, rows - w.shape[0]), (0, 0)))


def _pad_cols(w, cols):
    return jnp.pad(w, ((0, 0), (0, cols - w.shape[1])))


def _rope_tables(seq):
    half = HEAD_DIM // 2
    inv_freq = ROPE_THETA ** (-jnp.arange(half, dtype=F32) / half)
    ang = jnp.arange(seq, dtype=F32)[:, None] * inv_freq[None, :]
    cos = jnp.cos(ang)
    sin = jnp.sin(ang)
    return jnp.concatenate([cos, cos], axis=1), jnp.concatenate([-sin, sin], axis=1)


def kernel(x, p, norm_mix_pre, norm_mix_post, norm_ffn_pre, norm_ffn_post, norm_ple_pre, norm_ple_post, w_in, w_merge_gate, lru_conv_w, lru_conv_b, lru_w_r, lru_b_r, lru_w_i, lru_b_i, lru_lambda, rwkv_mu, rwkv_w0, rwkv_w_up, rwkv_a0, rwkv_a_up, rwkv_g_up, rwkv_k_k, rwkv_k_a, rwkv_r_k, rwkv_gn_w, rwkv_gn_b, rwkv_v0, rwkv_v_down, rwkv_v_up, w_branch_a, w_branch_b, w_branch_c, w_branch_d, w_out, w_ffn_up, ffn_conv_w, ffn_conv_b, w_ffn_down, w_ple, w_ple_gate):
    batch, seq, d = x.shape
    depth = w_in.shape[0]
    t = batch * seq
    lru_w = lru_conv_w.shape[2]
    rw = rwkv_w0.shape[1]
    dil_w = 3 * len(DIL_CONFIGS) * DIL_HEADS * HEAD_DIM
    sb_w = 3 * (d // 2)
    off_b = 2 * lru_w
    off_c = off_b + dil_w
    off_d = off_c + sb_w
    rwkv_in = w_in.shape[2] - off_d
    rwkv_pad = 3 * rw + 4 * LANES
    cos, sin = _rope_tables(seq)

    xf = x.reshape(t, d)
    h = rmsnorm_bf16(xf, norm_mix_pre[0])
    v_first = None
    for i in range(depth):
        wi = w_in[i]
        w_a = wi[:, :off_b].astype(BF16)
        w_b = wi[:, off_b:off_c].astype(BF16)
        w_c = wi[:, off_c:off_d].astype(BF16)
        w_d = _pad_cols(wi[:, off_d:], rwkv_pad).astype(BF16)
        seg_a = matmul(h, w_a, F32)
        seg_b = matmul_rope(h, w_b, cos, sin, seq, 2 * dil_w // 3, BF16)
        seg_c = matmul(h, w_c, BF16)
        seg_d = matmul(h, w_d, F32, tm=512, tn=rwkv_pad // 2)
        y_a = rglru(seg_a.reshape(batch, seq, off_b), lru_conv_w[i], lru_conv_b[i], lru_w_r[i].astype(BF16),
                    lru_b_r[i], lru_w_i[i].astype(BF16), lru_b_i[i], lru_lambda[i]).reshape(t, lru_w)
        y_b = dilated_attention(seg_b, batch, seq)
        y_c = stick_breaking(seg_c, batch, seq)
        mu = jnp.pad(rwkv_mu[i], (0, rwkv_pad - rwkv_in))
        w_up = _pad_rows(rwkv_w_up[i], LANES).astype(BF16)
        a_up = jnp.pad(rwkv_a_up[i], ((RWKV_W_LORA, LANES - RWKV_W_LORA - RWKV_A_LORA), (0, 0))).astype(BF16)
        g_up = _pad_rows(rwkv_g_up[i], 2 * LANES).astype(BF16)
        v_res = None
        if i > 0:
            v_res = (rwkv_v0[i - 1], _pad_cols(rwkv_v_down[i - 1], LANES).astype(BF16),
                     _pad_rows(rwkv_v_up[i - 1], LANES).astype(BF16))
        r_, k_, v_, lw_, kk_, b_, g_, bon_ = rwkv_prep(
            seg_d, seq, mu, rwkv_w0[i], w_up, rwkv_a0[i], a_up, g_up, rwkv_k_k[i], rwkv_k_a[i],
            rwkv_r_k[i].reshape(-1), v_first, v_res)
        if i == 0:
            v_first = v_
        y_d = rwkv_chunks(r_, k_, v_, lw_, kk_, b_, g_, bon_, rwkv_gn_w[i], rwkv_gn_b[i], batch, seq)
        merged = merge_branches(
            h, (y_a, y_b, y_c, y_d), w_merge_gate[i].astype(BF16),
            (w_branch_a[i].astype(BF16), w_branch_b[i].astype(BF16), w_branch_c[i].astype(BF16),
             w_branch_d[i].astype(BF16)))
        xf, h = matmul_norm_res(merged, w_out[i].astype(BF16), xf, norm_mix_post[i], norm_ffn_pre[i], tm=512, tk=d)
        act = ffn_up(h, w_ffn_up[i].astype(BF16), ffn_conv_w[i], ffn_conv_b[i], seq)
        xf, h = matmul_norm_res(act, w_ffn_down[i].astype(BF16), xf, norm_ffn_post[i], norm_ple_pre[i],
                                tm=512, tk=w_ffn_down.shape[1] // 4)
        g_next = norm_mix_pre[i + 1] if i + 1 < depth else norm_mix_pre[i]
        xf, h = ple_norm_res(p[i].reshape(t, -1).astype(BF16), h, w_ple[i].astype(BF16),
                             w_ple_gate[i].astype(BF16), xf, norm_ple_post[i], g_next)
    return xf.reshape(batch, seq, d)
```

```python
import functools

import jax
import jax.numpy as jnp
from jax import lax
from jax.experimental import pallas as pl
from jax.experimental.pallas import tpu as pltpu

F32 = jnp.float32
BF16 = jnp.bfloat16

LANES = 128
SUBLANES = 8
VMEM_LIMIT_BYTES = 52 * 1024 * 1024

HEAD_DIM = 128
BLOCK = 128
ROPE_THETA = 10000.0
RMS_EPS = 1e-6
NEG_INF = -1e30

LRU_BLOCKS = 8
LRU_CONV = 4
LRU_C = 8.0
DIL_CONFIGS = ((128, 1), (512, 4), (2048, 16))
DIL_HEADS = 4
RWKV_HEAD = 64
RWKV_W_LORA = 64
RWKV_A_LORA = 64
RWKV_G_LORA = 160
RWKV_V_LORA = 32
RWKV_GN_EPS = 64e-5
RWKV_CHUNK = 64
RWKV_CHUNKS_PER_STEP = 2
FFN_CONV = 3
SB_KEYS = 256
EPILOGUE_ROWS = 256


def _cparams(*sem):
    return pltpu.CompilerParams(dimension_semantics=sem, vmem_limit_bytes=VMEM_LIMIT_BYTES)


def _dot(a, b):
    return jnp.dot(a, b, preferred_element_type=F32)


def _dot_nt(a, b):
    return lax.dot_general(a, b, (((1,), (1,)), ((), ())), preferred_element_type=F32)


def _dot_tn(a, b):
    return lax.dot_general(a, b, (((0,), (0,)), ((), ())), preferred_element_type=F32)


def _split_bf16(x):
    hi = x.astype(BF16)
    lo = (x - hi.astype(F32)).astype(BF16)
    return hi, lo


def _dot_exact_rhs(x, m_bf16):
    hi, lo = _split_bf16(x)
    return _dot(hi, m_bf16) + _dot(lo, m_bf16)


def _rms(x, g):
    return x * lax.rsqrt(jnp.mean(x * x, axis=-1, keepdims=True) + RMS_EPS) * g


def _gelu(x):
    return jax.nn.gelu(x, approximate=True)


def _softplus(x):
    return jnp.maximum(x, 0.0) + jnp.log1p(jnp.exp(-jnp.abs(x)))


def _row_tiles(rows, sub=EPILOGUE_ROWS):
    sub = min(sub, rows)
    return [slice(s, s + sub) for s in range(0, rows, sub)]


def _rmsnorm_kernel(x_ref, g_ref, o_ref):
    o_ref[...] = _rms(x_ref[...], g_ref[...]).astype(o_ref.dtype)


def rmsnorm_bf16(x, g, tm=512):
    t, d = x.shape
    return pl.pallas_call(
        _rmsnorm_kernel,
        out_shape=jax.ShapeDtypeStruct((t, d), BF16),
        grid=(t // tm,),
        in_specs=[pl.BlockSpec((tm, d), lambda i: (i, 0)), pl.BlockSpec((1, d), lambda i: (0, 0))],
        out_specs=pl.BlockSpec((tm, d), lambda i: (i, 0)),
        compiler_params=_cparams("parallel"),
        name="rmsnorm",
    )(x, g.reshape(1, d))


def _mm_kernel(a_ref, w_ref, o_ref):
    o_ref[...] = _dot(a_ref[...], w_ref[...]).astype(o_ref.dtype)


def _mm_rope_kernel(a_ref, w_ref, cos_ref, sin_ref, o_ref, *, n_rope_blocks, tn):
    j = pl.program_id(1)
    row_tiles = _row_tiles(a_ref.shape[0])

    @pl.when(j < n_rope_blocks)
    def _():
        for rows in row_tiles:
            acc = _dot(a_ref[rows, :], w_ref[...])
            cos = cos_ref[rows, :]
            sin = sin_ref[rows, :]
            for c in range(tn // HEAD_DIM):
                seg = acc[:, c * HEAD_DIM:(c + 1) * HEAD_DIM]
                rot = pltpu.roll(seg, HEAD_DIM // 2, axis=1)
                o_ref[rows, c * HEAD_DIM:(c + 1) * HEAD_DIM] = (seg * cos + rot * sin).astype(o_ref.dtype)

    @pl.when(j >= n_rope_blocks)
    def _():
        o_ref[...] = _dot(a_ref[...], w_ref[...]).astype(o_ref.dtype)


def matmul(a, w, out_dtype, tm=1024, tn=1024):
    m, k = a.shape
    n = w.shape[1]
    tm, tn = min(tm, m), min(tn, n)
    return pl.pallas_call(
        _mm_kernel,
        out_shape=jax.ShapeDtypeStruct((m, n), out_dtype),
        grid=(m // tm, n // tn),
        in_specs=[pl.BlockSpec((tm, k), lambda i, j: (i, 0)), pl.BlockSpec((k, tn), lambda i, j: (0, j))],
        out_specs=pl.BlockSpec((tm, tn), lambda i, j: (i, j)),
        compiler_params=_cparams("parallel", "arbitrary"),
        name="matmul_plain",
    )(a, w)


def matmul_rope(a, w, cos, sin, seq, n_rope_cols, out_dtype, tm=1024, tn=1536):
    m, k = a.shape
    n = w.shape[1]
    tm, tn = min(tm, m), min(tn, n)
    sblocks = seq // tm
    return pl.pallas_call(
        functools.partial(_mm_rope_kernel, n_rope_blocks=n_rope_cols // tn, tn=tn),
        out_shape=jax.ShapeDtypeStruct((m, n), out_dtype),
        grid=(m // tm, n // tn),
        in_specs=[
            pl.BlockSpec((tm, k), lambda i, j: (i, 0)),
            pl.BlockSpec((k, tn), lambda i, j: (0, j)),
            pl.BlockSpec((tm, HEAD_DIM), lambda i, j: (i % sblocks, 0)),
            pl.BlockSpec((tm, HEAD_DIM), lambda i, j: (i % sblocks, 0)),
        ],
        out_specs=pl.BlockSpec((tm, tn), lambda i, j: (i, j)),
        compiler_params=_cparams("parallel", "arbitrary"),
        name="matmul_rope",
    )(a, w, cos, sin)


def _mm_norm_res_kernel(a_ref, w_ref, x_ref, gpost_ref, gnext_ref, xo_ref, ho_ref, *acc, nk):
    row_tiles = _row_tiles(a_ref.shape[0])

    def finish(rows, val):
        xn = x_ref[rows, :] + _rms(val, gpost_ref[...])
        xo_ref[rows, :] = xn
        ho_ref[rows, :] = _rms(xn, gnext_ref[...]).astype(ho_ref.dtype)

    if nk == 1:
        for rows in row_tiles:
            finish(rows, _dot(a_ref[rows, :], w_ref[...]))
        return
    acc_ref, = acc
    kk = pl.program_id(1)

    @pl.when(kk == 0)
    def _():
        acc_ref[...] = _dot(a_ref[...], w_ref[...])

    @pl.when((kk > 0) & (kk < nk - 1))
    def _():
        acc_ref[...] += _dot(a_ref[...], w_ref[...])

    @pl.when(kk == nk - 1)
    def _():
        for rows in row_tiles:
            finish(rows, acc_ref[rows, :] + _dot(a_ref[rows, :], w_ref[...]))


def matmul_norm_res(a, w, x, g_post, g_next, tm=256, tk=512):
    m, k = a.shape
    d = w.shape[1]
    tm = min(tm, m)
    nk = k // tk
    return pl.pallas_call(
        functools.partial(_mm_norm_res_kernel, nk=nk),
        out_shape=(jax.ShapeDtypeStruct((m, d), F32), jax.ShapeDtypeStruct((m, d), BF16)),
        grid=(m // tm, nk),
        in_specs=[
            pl.BlockSpec((tm, tk), lambda i, kk: (i, kk)),
            pl.BlockSpec((tk, d), lambda i, kk: (kk, 0)),
            pl.BlockSpec((tm, d), lambda i, kk: (i, 0)),
            pl.BlockSpec((1, d), lambda i, kk: (0, 0)),
            pl.BlockSpec((1, d), lambda i, kk: (0, 0)),
        ],
        out_specs=(pl.BlockSpec((tm, d), lambda i, kk: (i, 0)), pl.BlockSpec((tm, d), lambda i, kk: (i, 0))),
        scratch_shapes=[pltpu.VMEM((tm, d), F32)] if nk > 1 else [],
        compiler_params=_cparams("parallel", "arbitrary"),
        name="matmul_norm_res",
    )(a, w, x, g_post.reshape(1, d), g_next.reshape(1, d))


def _ple_kernel(p_ref, h_ref, wp_ref, wg_ref, x_ref, gpost_ref, gnext_ref, xo_ref, ho_ref):
    for rows in _row_tiles(x_ref.shape[0]):
        val = _dot(p_ref[rows, :], wp_ref[...]) * jax.nn.sigmoid(_dot(h_ref[rows, :], wg_ref[...]))
        xn = x_ref[rows, :] + _rms(val, gpost_ref[...])
        xo_ref[rows, :] = xn
        ho_ref[rows, :] = _rms(xn, gnext_ref[...]).astype(ho_ref.dtype)


def ple_norm_res(p, h, w_ple, w_gate, x, g_post, g_next, tm=512):
    m, d = x.shape
    tm = min(tm, m)
    pd = p.shape[1]
    row = lambda i: (i, 0)
    fix = lambda i: (0, 0)
    return pl.pallas_call(
        _ple_kernel,
        out_shape=(jax.ShapeDtypeStruct((m, d), F32), jax.ShapeDtypeStruct((m, d), BF16)),
        grid=(m // tm,),
        in_specs=[
            pl.BlockSpec((tm, pd), row), pl.BlockSpec((tm, d), row),
            pl.BlockSpec((pd, d), fix), pl.BlockSpec((d, d), fix),
            pl.BlockSpec((tm, d), row), pl.BlockSpec((1, d), fix), pl.BlockSpec((1, d), fix),
        ],
        out_specs=(pl.BlockSpec((tm, d), row), pl.BlockSpec((tm, d), row)),
        compiler_params=_cparams("parallel"),
        name="ple_norm_res",
    )(p, h, w_ple, w_gate, x, g_post.reshape(1, d), g_next.reshape(1, d))


def _merge_kernel(h_ref, ya_ref, yb_ref, yc_ref, yd_ref, ga_ref, gb_ref, gc_ref, gd_ref,
                  wa_ref, wb_ref, wc_ref, wd_ref, o_ref):
    h = h_ref[...]
    acc = jax.nn.sigmoid(_dot(h, ga_ref[...])) * _dot(ya_ref[...], wa_ref[...])
    acc += jax.nn.sigmoid(_dot(h, gb_ref[...])) * _dot(yb_ref[...], wb_ref[...])
    acc += jax.nn.sigmoid(_dot(h, gc_ref[...])) * _dot(yc_ref[...], wc_ref[...])
    acc += jax.nn.sigmoid(_dot(h, gd_ref[...])) * _dot(yd_ref[...], wd_ref[...])
    o_ref[...] = acc.astype(o_ref.dtype)


def merge_branches(h, ys, w_gate, ws, tm=1024, tn=256):
    m, dm = h.shape
    tm = min(tm, m)
    d = ws[0].shape[1]
    nb = d // tn
    in_specs = [pl.BlockSpec((tm, dm), lambda i, j: (i, 0))]
    in_specs += [pl.BlockSpec((tm, y.shape[1]), lambda i, j: (i, 0)) for y in ys]
    in_specs += [pl.BlockSpec((dm, tn), functools.partial(lambda i, j, b: (0, b * nb + j), b=b)) for b in range(4)]
    in_specs += [pl.BlockSpec((w.shape[0], tn), lambda i, j: (0, j)) for w in ws]
    return pl.pallas_call(
        _merge_kernel,
        out_shape=jax.ShapeDtypeStruct((m, d), BF16),
        grid=(m // tm, nb),
        in_specs=in_specs,
        out_specs=pl.BlockSpec((tm, tn), lambda i, j: (i, j)),
        compiler_params=_cparams("parallel", "arbitrary"),
        name="merge_branches",
    )(h, *ys, w_gate, w_gate, w_gate, w_gate, *ws)


def _ffn_up_kernel(h_ref, wg32_ref, wu32_ref, cwg_ref, cwu_ref, cbg_ref, cbu_ref, o_ref, bufg_ref, bufu_ref,
                   wg_ref, wu_ref, *, tm, seq_blocks):
    i = pl.program_id(1)

    @pl.when(i == 0)
    def _():
        wg_ref[...] = wg32_ref[...].astype(BF16)
        wu_ref[...] = wu32_ref[...].astype(BF16)

    @pl.when(i % seq_blocks == 0)
    def _():
        bufg_ref[0:SUBLANES, :] = jnp.zeros((SUBLANES, bufg_ref.shape[1]), F32)
        bufu_ref[0:SUBLANES, :] = jnp.zeros((SUBLANES, bufu_ref.shape[1]), F32)

    h = h_ref[...]

    def conv(w_ref, cw_ref, cb_ref, buf_ref):
        buf_ref[SUBLANES:SUBLANES + tm, :] = _dot(h, w_ref[...])
        cw = cw_ref[...]
        out = cb_ref[...] + cw[2:3, :] * buf_ref[SUBLANES:SUBLANES + tm, :]
        out += cw[1:2, :] * buf_ref[SUBLANES - 1:SUBLANES - 1 + tm, :]
        out += cw[0:1, :] * buf_ref[SUBLANES - 2:SUBLANES - 2 + tm, :]
        buf_ref[0:SUBLANES, :] = buf_ref[tm:tm + SUBLANES, :]
        return out

    g = conv(wg_ref, cwg_ref, cbg_ref, bufg_ref)
    u = conv(wu_ref, cwu_ref, cbu_ref, bufu_ref)
    o_ref[...] = (_gelu(g) * u).astype(o_ref.dtype)


def ffn_up(h, w_up_all, layer, conv_w, conv_b, seq, tm=1024, tn=512):
    m, d = h.shape
    tm = min(tm, seq)
    dff = w_up_all.shape[2] // 2
    nb = dff // tn
    cb = conv_b.reshape(1, 2 * dff)
    return pl.pallas_call(
        functools.partial(_ffn_up_kernel, tm=tm, seq_blocks=seq // tm),
        out_shape=jax.ShapeDtypeStruct((m, dff), BF16),
        grid=(nb, m // tm),
        in_specs=[
            pl.BlockSpec((tm, d), lambda j, i: (i, 0)),
            pl.BlockSpec((None, d, tn), lambda j, i: (layer, 0, j)),
            pl.BlockSpec((None, d, tn), lambda j, i: (layer, 0, j + nb)),
            pl.BlockSpec((FFN_CONV, tn), lambda j, i: (0, j)),
            pl.BlockSpec((FFN_CONV, tn), lambda j, i: (0, j + nb)),
            pl.BlockSpec((1, tn), lambda j, i: (0, j)),
            pl.BlockSpec((1, tn), lambda j, i: (0, j + nb)),
        ],
        out_specs=pl.BlockSpec((tm, tn), lambda j, i: (i, j)),
        scratch_shapes=[pltpu.VMEM((tm + SUBLANES, tn), F32), pltpu.VMEM((tm + SUBLANES, tn), F32),
                        pltpu.VMEM((d, tn), BF16), pltpu.VMEM((d, tn), BF16)],
        compiler_params=_cparams("parallel", "arbitrary"),
        name="ffn_up_conv_glu",
    )(h, w_up_all, w_up_all, conv_w, conv_w, cb, cb)


def _lru_kernel(x_ref, gate_ref, cw_ref, cb_ref, wr_ref, br_ref, wi_ref, bi_ref, lam_ref, o_ref,
                xbuf_ref, a0_ref, h0_ref, a1_ref, h1_ref, *, seq, pad):
    xbuf_ref[0:SUBLANES, :] = jnp.zeros((SUBLANES, LANES), F32)
    xbuf_ref[SUBLANES:SUBLANES + seq, :] = x_ref[0]
    cw = cw_ref[...]
    u = cb_ref[...] + cw[3:4, :] * xbuf_ref[SUBLANES:SUBLANES + seq, :]
    for k in range(LRU_CONV - 1):
        off = SUBLANES - (LRU_CONV - 1) + k
        u += cw[k:k + 1, :] * xbuf_ref[off:off + seq, :]
    ub = u.astype(BF16)
    r = jax.nn.sigmoid(_dot(ub, wr_ref[0]) + br_ref[...])
    ig = jax.nn.sigmoid(_dot(ub, wi_ref[0]) + bi_ref[...])
    log_a = -LRU_C * r * _softplus(-lam_ref[...])
    a = jnp.exp(log_a)
    inp = jnp.sqrt(1.0 - a * a) * ig * u

    ones = jnp.ones((pad, LANES), F32)
    zeros = jnp.zeros((pad, LANES), F32)
    a0_ref[0:pad, :] = ones
    a1_ref[0:pad, :] = ones
    h0_ref[0:pad, :] = zeros
    h1_ref[0:pad, :] = zeros
    a0_ref[pad:pad + seq, :] = a
    h0_ref[pad:pad + seq, :] = inp
    bufs = ((a0_ref, h0_ref), (a1_ref, h1_ref))
    d = 1
    level = 0
    while d < seq:
        (a_src, h_src), (a_dst, h_dst) = bufs[level % 2], bufs[(level + 1) % 2]
        a_cur = a_src[pad:pad + seq, :]
        h_dst[pad:pad + seq, :] = h_src[pad:pad + seq, :] + a_cur * h_src[pad - d:pad - d + seq, :]
        a_dst[pad:pad + seq, :] = a_cur * a_src[pad - d:pad - d + seq, :]
        d *= 2
        level += 1
    h = bufs[level % 2][1][pad:pad + seq, :]
    o_ref[0] = (h * _gelu(gate_ref[0])).astype(o_ref.dtype)


def rglru(xg, conv_w, conv_b, w_r, b_r, w_i, b_i, lam):
    b, s, w2 = xg.shape
    w = w2 // 2
    nblk = w // LANES
    pad = s
    vec = lambda v: v.reshape(1, w)
    vspec = pl.BlockSpec((1, LANES), lambda bi, c: (0, c))
    return pl.pallas_call(
        functools.partial(_lru_kernel, seq=s, pad=pad),
        out_shape=jax.ShapeDtypeStruct((b, s, w), BF16),
        grid=(b, nblk),
        in_specs=[
            pl.BlockSpec((1, s, LANES), lambda bi, c: (bi, 0, c)),
            pl.BlockSpec((1, s, LANES), lambda bi, c: (bi, 0, c + nblk)),
            pl.BlockSpec((LRU_CONV, LANES), lambda bi, c: (0, c)),
            vspec,
            pl.BlockSpec((1, LANES, LANES), lambda bi, c: (c, 0, 0)),
            vspec,
            pl.BlockSpec((1, LANES, LANES), lambda bi, c: (c, 0, 0)),
            vspec, vspec,
        ],
        out_specs=pl.BlockSpec((1, s, LANES), lambda bi, c: (bi, 0, c)),
        scratch_shapes=[pltpu.VMEM((s + SUBLANES, LANES), F32)] + [pltpu.VMEM((pad + s, LANES), F32)] * 4,
        compiler_params=_cparams("parallel", "parallel"),
        name="rglru",
    )(xg, xg, conv_w, vec(conv_b), w_r, vec(b_r), w_i, vec(b_i), vec(lam))


DIL_UNITS_PER_PHASE = 4


def _dil_kernel(*refs, seq):
    ngroups = len(DIL_CONFIGS)
    qkv_refs = refs[:3 * ngroups]
    o_ref = refs[3 * ngroups]
    qf_ref, kf_ref, vf_ref, acc_ref, m_ref, l_ref = refs[3 * ngroups + 1:]
    scale = HEAD_DIM ** -0.5
    row = lax.broadcasted_iota(jnp.int32, (BLOCK, 2 * BLOCK), 0)
    col = lax.broadcasted_iota(jnp.int32, (BLOCK, 2 * BLOCK), 1)
    bias_two = jnp.where((col >= row) & (col <= row + BLOCK), 0.0, NEG_INF)
    bias_own = jnp.where(lax.broadcasted_iota(jnp.int32, (BLOCK, BLOCK), 1)
                         <= lax.broadcasted_iota(jnp.int32, (BLOCK, BLOCK), 0), 0.0, NEG_INF)

    def run_units(g, units):
        s = [_dot_nt(q, k) * scale + (bias_own if k.shape[0] == BLOCK else bias_two) for q, k, _, _ in units]
        mx = [jnp.max(x, axis=-1, keepdims=True) for x in s]
        e = [jnp.exp(x - m).astype(BF16) for x, m in zip(s, mx)]
        pv = [_dot(p, jnp.concatenate([v, jnp.ones_like(v)], axis=1)) for p, (_, _, v, _) in zip(e, units)]
        for y, m, (_, _, _, rows) in zip(pv, mx, units):
            acc_ref[g, rows, :] = y[:, :HEAD_DIM]
            l_ref[g, rows, :] = y[:, HEAD_DIM:]
            m_ref[g, rows, :] = jnp.broadcast_to(m, (BLOCK, HEAD_DIM))

    for g, (_, d) in enumerate(DIL_CONFIGS):
        q_ref, k_ref, v_ref = qkv_refs[3 * g:3 * g + 3]
        ln = seq // d
        nblk = ln // BLOCK
        if d > 1:
            qf_ref[...] = q_ref[...].astype(F32)
            kf_ref[...] = k_ref[...].astype(F32)
            vf_ref[...] = v_ref[...].astype(F32)
        units = []
        for r in range(d):
            if d > 1:
                qr = qf_ref[pl.ds(r, ln, stride=d), :].astype(BF16)
                kr = kf_ref[pl.ds(r, ln, stride=d), :].astype(BF16)
                vr = vf_ref[pl.ds(r, ln, stride=d), :].astype(BF16)
            for nb in range(nblk):
                lo = max(nb - 1, 0) * BLOCK
                hi = (nb + 1) * BLOCK
                if d > 1:
                    unit = (qr[nb * BLOCK:hi], kr[lo:hi], vr[lo:hi], pl.ds(nb * BLOCK * d + r, BLOCK, stride=d))
                else:
                    unit = (q_ref[nb * BLOCK:hi, :], k_ref[lo:hi, :], v_ref[lo:hi, :], pl.ds(nb * BLOCK, BLOCK))
                units.append(unit)
                if len(units) == DIL_UNITS_PER_PHASE:
                    run_units(g, units)
                    units = []
        if units:
            run_units(g, units)

    rows_per_step = 2 * BLOCK
    for c in range(seq // rows_per_step):
        rs = slice(c * rows_per_step, (c + 1) * rows_per_step)
        ms = [m_ref[g, rs, :] for g in range(ngroups)]
        top = functools.reduce(jnp.maximum, ms)
        ws = [jnp.exp(m - top) for m in ms]
        num = functools.reduce(jnp.add, [w * acc_ref[g, rs, :] for g, w in enumerate(ws)])
        den = functools.reduce(jnp.add, [w * l_ref[g, rs, :] for g, w in enumerate(ws)])
        o_ref[rs, :] = (num / den).astype(o_ref.dtype)


def dilated_attention(qkv, batch, seq):
    assert all(window // d == BLOCK and seq % (BLOCK * d) == 0 for window, d in DIL_CONFIGS)
    t, width = qkv.shape
    ngroups = len(DIL_CONFIGS)
    nheads = width // (3 * HEAD_DIM)

    def spec(which, g):
        return pl.BlockSpec((seq, HEAD_DIM), lambda b, h: (b, which * nheads + g * DIL_HEADS + h))

    in_specs = [spec(which, g) for g in range(ngroups) for which in range(3)]
    return pl.pallas_call(
        functools.partial(_dil_kernel, seq=seq),
        out_shape=jax.ShapeDtypeStruct((t, DIL_HEADS * HEAD_DIM), BF16),
        grid=(batch, DIL_HEADS),
        in_specs=in_specs,
        out_specs=pl.BlockSpec((seq, HEAD_DIM), lambda b, h: (b, h)),
        scratch_shapes=[pltpu.VMEM((seq, HEAD_DIM), F32)] * 3 + [pltpu.VMEM((ngroups, seq, HEAD_DIM), F32)] * 3,
        compiler_params=_cparams("parallel", "parallel"),
        name="dilated_attention",
    )(*([qkv] * (3 * ngroups)))


def _sb_kernel(q_ref, k_ref, v_ref, o_ref, acc_ref, run_ref):
    n = pl.program_id(1)
    nh = q_ref.shape[1] // HEAD_DIM
    heads = range(nh)
    sls = [slice(h * HEAD_DIM, (h + 1) * HEAD_DIM) for h in heads]
    scale = HEAD_DIM ** -0.5
    kb = SB_KEYS
    assert kb == 2 * BLOCK
    from_s = (lax.broadcasted_iota(jnp.int32, (kb, kb), 0) >= lax.broadcasted_iota(jnp.int32, (kb, kb), 1)).astype(BF16)
    off = (n % 2) * BLOCK
    before = (lax.broadcasted_iota(jnp.int32, (BLOCK, kb), 1) - lax.broadcasted_iota(jnp.int32, (BLOCK, kb), 0)) < off
    q = [q_ref[:, sl] for sl in sls]

    def block(jb, diag):
        start = pl.multiple_of(jb * kb, kb)
        z = [_dot_nt(q[h], k_ref[pl.ds(start, kb), sls[h]]) * scale for h in heads]
        log_beta = [jnp.minimum(zz, 0.0) - jnp.log(1.0 + jnp.exp(-jnp.abs(zz))) for zz in z]
        log_1m = [lb - zz for lb, zz in zip(log_beta, z)]
        if diag:
            log_1m = [jnp.where(before, x, 0.0) for x in log_1m]
        incl = [_dot_exact_rhs(x, from_s) for x in log_1m]
        for h in heads:
            att = jnp.exp(z[h] + incl[h]) if diag else jnp.exp(z[h] + incl[h] + run_ref[h])
            if diag:
                att = jnp.where(before, att, 0.0)
            pv = _dot(att.astype(BF16), v_ref[pl.ds(start, kb), sls[h]])
            total = jnp.broadcast_to(incl[h][:, 0:1], (BLOCK, kb))
            if diag:
                acc_ref[:, sls[h]] = pv
                run_ref[h] = total
            else:
                acc_ref[:, sls[h]] += pv
                run_ref[h] = run_ref[h] + total

    full = n // 2
    block(full, True)

    def body(t, carry):
        block(full - 1 - t, False)
        return carry

    lax.fori_loop(0, full, body, 0)
    o_ref[...] = acc_ref[...].astype(o_ref.dtype)


def stick_breaking(qkv, batch, seq):
    t, width = qkv.shape
    w = width // 3
    nq = seq // BLOCK
    assert seq % SB_KEYS == 0
    return pl.pallas_call(
        _sb_kernel,
        out_shape=jax.ShapeDtypeStruct((t, w), BF16),
        grid=(batch, nq),
        in_specs=[
            pl.BlockSpec((BLOCK, w), lambda b, n: (b * nq + n, 0)),
            pl.BlockSpec((seq, w), lambda b, n: (b, 1)),
            pl.BlockSpec((seq, w), lambda b, n: (b, 2)),
        ],
        out_specs=pl.BlockSpec((BLOCK, w), lambda b, n: (b * nq + n, 0)),
        scratch_shapes=[pltpu.VMEM((BLOCK, w), F32), pltpu.VMEM((w // HEAD_DIM, BLOCK, SB_KEYS), F32)],
        compiler_params=_cparams("parallel", "arbitrary"),
        name="stick_breaking",
    )(qkv, qkv, qkv)


def _head_sum(x, bd):
    cols = []
    for c in range(x.shape[1] // LANES):
        cols.append(_dot_exact_rhs(x[:, c * LANES:(c + 1) * LANES], bd))
    return jnp.concatenate(cols, axis=1)


def _rwkv_prep_kernel(*refs, tm, width, seq_blocks, has_vres):
    if has_vres:
        (seg_ref, prev_ref, mu_ref, w0_ref, wup_ref, a0_ref, aup_ref, gup_ref, kk_ref, ka_ref, rk_ref,
         vf_ref, v0_ref, vdn_ref, vup_ref,
         r_o, k_o, v_o, lw_o, kk_o, b_o, g_o, bon_o, buf_ref) = refs
    else:
        (seg_ref, prev_ref, mu_ref, w0_ref, wup_ref, a0_ref, aup_ref, gup_ref, kk_ref, ka_ref, rk_ref,
         r_o, k_o, v_o, lw_o, kk_o, b_o, g_o, bon_o, buf_ref) = refs
    i = pl.program_id(0)
    seg = seg_ref[...]
    buf_ref[0:SUBLANES, :] = prev_ref[...]

    @pl.when(i % seq_blocks == 0)
    def _():
        buf_ref[0:SUBLANES, :] = jnp.zeros((SUBLANES, buf_ref.shape[1]), F32)

    buf_ref[SUBLANES:SUBLANES + tm, :] = seg
    shifted = buf_ref[SUBLANES - 1:SUBLANES - 1 + tm, :]
    xs = seg + (shifted - seg) * mu_ref[...]
    w = width
    r = xs[:, 0:w]
    k = xs[:, w:2 * w]
    v = xs[:, 2 * w:3 * w]
    low = xs[:, 3 * w:3 * w + LANES]
    g_low = xs[:, 3 * w + LANES:3 * w + 3 * LANES]
    wpre = w0_ref[...] + _dot(jnp.tanh(low).astype(BF16), wup_ref[...])
    wlog = -_softplus(-wpre) - 0.5
    lw_o[...] = -jnp.exp(wlog)
    a = jax.nn.sigmoid(a0_ref[...] + _dot(low.astype(BF16), aup_ref[...]))
    g_o[...] = _dot(jax.nn.sigmoid(g_low).astype(BF16), gup_ref[...]).astype(g_o.dtype)
    if has_vres:
        mix = jax.nn.sigmoid(v0_ref[...] + _dot(_dot(v.astype(BF16), vdn_ref[...]).astype(BF16), vup_ref[...]))
        v = v + (vf_ref[...] - v) * mix
    row = lax.broadcasted_iota(jnp.int32, (LANES, LANES), 0) // RWKV_HEAD
    col = lax.broadcasted_iota(jnp.int32, (LANES, LANES), 1) // RWKV_HEAD
    bd = (row == col).astype(BF16)
    kk = k * kk_ref[...]
    norm = jnp.sqrt(_head_sum(kk * kk, bd))
    kk = kk / jnp.maximum(norm, 1e-12)
    k2 = k * (1.0 + (a - 1.0) * ka_ref[...])
    bonus = _head_sum(r * k2 * rk_ref[...], bd) * v
    r_o[...] = r.astype(r_o.dtype)
    k_o[...] = k2.astype(k_o.dtype)
    v_o[...] = v
    kk_o[...] = kk.astype(kk_o.dtype)
    b_o[...] = (kk * a).astype(b_o.dtype)
    bon_o[...] = bonus.astype(bon_o.dtype)


def rwkv_prep(seg, seq, mu, w0, w_up, a0, a_up, g_up, k_k, k_a, r_k, v_first, v_res, tm=256):
    t, wpad = seg.shape
    w = w0.shape[0]
    has_vres = v_res is not None
    row = lambda i: (i, 0)
    fix = lambda i: (0, 0)
    vec = lambda x: x.reshape(1, -1)
    step = tm // SUBLANES
    in_specs = [
        pl.BlockSpec((tm, wpad), row),
        pl.BlockSpec((SUBLANES, wpad), lambda i: (jnp.maximum(i * step - 1, 0), 0)),
        pl.BlockSpec((1, wpad), fix), pl.BlockSpec((1, w), fix), pl.BlockSpec((LANES, w), fix),
        pl.BlockSpec((1, w), fix), pl.BlockSpec((LANES, w), fix), pl.BlockSpec((2 * LANES, w), fix),
        pl.BlockSpec((1, w), fix), pl.BlockSpec((1, w), fix), pl.BlockSpec((1, w), fix),
    ]
    args = [seg, seg, vec(mu), vec(w0), w_up, vec(a0), a_up, g_up, vec(k_k), vec(k_a), vec(r_k)]
    if has_vres:
        v0, v_down, v_up = v_res
        in_specs += [pl.BlockSpec((tm, w), row), pl.BlockSpec((1, w), fix),
                     pl.BlockSpec((w, LANES), fix), pl.BlockSpec((LANES, w), fix)]
        args += [v_first, vec(v0), v_down, v_up]
    outs = tuple(jax.ShapeDtypeStruct((t, w), dt) for dt in (BF16, BF16, F32, F32, BF16, BF16, BF16, BF16))
    return pl.pallas_call(
        functools.partial(_rwkv_prep_kernel, tm=tm, width=w, seq_blocks=seq // tm, has_vres=has_vres),
        out_shape=outs,
        grid=(t // tm,),
        in_specs=in_specs,
        out_specs=(pl.BlockSpec((tm, w), row),) * 8,
        scratch_shapes=[pltpu.VMEM((tm + SUBLANES, wpad), F32)],
        compiler_params=_cparams("parallel"),
        name="rwkv_prep",
    )(*args)


def _rwkv_chunk_kernel(r_ref, k_ref, v_ref, lw_ref, kk_ref, b_ref, g_ref, bon_ref, gnw_ref, gnb_ref, o_ref,
                       state_ref):
    c = pl.program_id(1)
    n = RWKV_CHUNK
    n2 = 2 * n

    @pl.when(c == 0)
    def _():
        state_ref[...] = jnp.zeros_like(state_ref)

    tri = (lax.broadcasted_iota(jnp.int32, (n, n), 1) <= lax.broadcasted_iota(jnp.int32, (n, n), 0)).astype(BF16)
    row2 = lax.broadcasted_iota(jnp.int32, (n2, n2), 0)
    col2 = lax.broadcasted_iota(jnp.int32, (n2, n2), 1)
    t2 = row2 & (n - 1)
    s2 = col2 & (n - 1)
    strict = s2 < t2
    incl = s2 <= t2
    eye = (row2 == col2).astype(F32)
    head0 = lax.broadcasted_iota(jnp.int32, (1, LANES), 1) < RWKV_HEAD
    own = jnp.concatenate([jnp.broadcast_to(head0, (n, LANES)), jnp.broadcast_to(~head0, (n, LANES))], axis=0)

    def stack(x):
        return jnp.where(own, jnp.concatenate([x, x], axis=0), 0.0)

    npairs = r_ref.shape[1] // LANES
    nchunks = r_ref.shape[0] // n
    sls = [slice(hp * LANES, (hp + 1) * LANES) for hp in range(npairs)]
    units = [(ci, hp) for ci in range(nchunks) for hp in range(npairs)]
    pre = []
    for ci in range(nchunks):
        rows = slice(ci * n, (ci + 1) * n)
        lw = lw_ref[rows, :]
        gsum = _dot_exact_rhs_left(tri, lw)
        p_inv = jnp.exp(-gsum)
        pre.append(dict(p_end=jnp.exp(gsum[n - 1:n, :]), rt=r_ref[rows, :] * jnp.exp(gsum), kt=k_ref[rows, :] * p_inv,
                        bt=b_ref[rows, :] * p_inv, kap=kk_ref[rows, :] * jnp.exp(gsum - lw), v=v_ref[rows, :]))
    rt = {u: stack(pre[u[0]]["rt"][:, sls[u[1]]]).astype(BF16) for u in units}
    kt = {u: stack(pre[u[0]]["kt"][:, sls[u[1]]]) for u in units}
    bt = {u: stack(pre[u[0]]["bt"][:, sls[u[1]]]) for u in units}
    kap = {u: stack(pre[u[0]]["kap"][:, sls[u[1]]]).astype(BF16) for u in units}
    v2 = {u: stack(pre[u[0]]["v"][:, sls[u[1]]]).astype(BF16) for u in units}
    prod = {u: _dot_nt(jnp.concatenate([kap[u], rt[u]], axis=0),
                       jnp.concatenate([bt[u].astype(BF16), kt[u].astype(BF16)], axis=0)) for u in units}
    a_ab = {u: jnp.where(strict, prod[u][:n2, :n2], 0.0) for u in units}
    a_ak = {u: jnp.where(strict, prod[u][:n2, n2:], 0.0).astype(BF16) for u in units}
    a_r = {u: jnp.concatenate([jnp.where(incl, prod[u][n2:, n2:], 0.0).astype(BF16),
                               jnp.where(incl, -prod[u][n2:, :n2], 0.0).astype(BF16)], axis=1) for u in units}
    x = {u: eye - a_ab[u] for u in units}
    q = {u: _dot(a_ab[u].astype(BF16), a_ab[u].astype(BF16)) for u in units}
    steps = 1
    while True:
        x = {u: x[u] + _dot(x[u].astype(BF16), q[u].astype(BF16)) for u in units}
        steps *= 2
        if steps * 2 >= n:
            break
        q = {u: _dot(q[u].astype(BF16), q[u].astype(BF16)) for u in units}
    inv_n = 1.0 / RWKV_HEAD
    state = [state_ref[hp] for hp in range(npairs)]
    for ci in range(nchunks):
        rows = slice(ci * n, (ci + 1) * n)
        us = [(ci, hp) for hp in range(npairs)]
        s0b = [s.astype(BF16) for s in state]
        rhs = [_dot_nt(kap[u], s0b[u[1]]) + _dot(a_ak[u], v2[u]) for u in us]
        u2 = [_dot(x[u].astype(BF16), r_.astype(BF16)).astype(BF16) for u, r_ in zip(us, rhs)]
        vu = [jnp.concatenate([v2[u], w_], axis=0) for u, w_ in zip(us, u2)]
        y2 = [_dot_nt(rt[u], s0b[u[1]]) + _dot(a_r[u], vu_) for u, vu_ in zip(us, vu)]
        outs = []
        for hp, u in enumerate(us):
            pe = pre[ci]["p_end"][:, sls[hp]]
            kb_end = jnp.concatenate([(kt[u] * pe).astype(BF16), (-(bt[u] * pe)).astype(BF16)], axis=0)
            state[hp] = state[hp] * pe + _dot_tn(vu[hp], kb_end)
            mean = jnp.sum(y2[hp], axis=-1, keepdims=True) * inv_n
            cen = jnp.where(own, y2[hp] - mean, 0.0)
            var = jnp.sum(cen * cen, axis=-1, keepdims=True) * inv_n
            yn2 = cen * lax.rsqrt(var + RWKV_GN_EPS)
            outs.append(yn2[:n] + yn2[n:])
        yn = jnp.concatenate(outs, axis=1) * gnw_ref[...] + gnb_ref[...]
        o_ref[rows, :] = ((yn + bon_ref[rows, :]) * g_ref[rows, :]).astype(o_ref.dtype)
    for hp in range(npairs):
        state_ref[hp] = state[hp]


def _dot_exact_rhs_left(m_bf16, x):
    hi, lo = _split_bf16(x)
    return _dot(m_bf16, hi) + _dot(m_bf16, lo)


def rwkv_chunks(r, k, v, lw, kk, b, g, bon, gn_w, gn_b, batch, seq):
    t, w = r.shape
    rows = RWKV_CHUNKS_PER_STEP * RWKV_CHUNK
    nc = seq // rows
    spec = pl.BlockSpec((rows, w), lambda bi, c: (bi * nc + c, 0))
    vspec = pl.BlockSpec((1, w), lambda bi, c: (0, 0))
    return pl.pallas_call(
        _rwkv_chunk_kernel,
        out_shape=jax.ShapeDtypeStruct((t, w), BF16),
        grid=(batch, nc),
        in_specs=[spec] * 8 + [vspec, vspec],
        out_specs=spec,
        scratch_shapes=[pltpu.VMEM((w // LANES, LANES, LANES), F32)],
        compiler_params=_cparams("parallel", "arbitrary"),
        name="rwkv_chunks",
    )(r, k, v, lw, kk, b, g, bon, gn_w.reshape(1, w), gn_b.reshape(1, w))


def _pad_rows(w, rows):
    return jnp.pad(w, ((0, rows - w.shape[0]), (0, 0)))


def _pad_cols(w, cols):
    return jnp.pad(w, ((0, 0), (0, cols - w.shape[1])))


def _rope_tables(seq):
    half = HEAD_DIM // 2
    inv_freq = ROPE_THETA ** (-jnp.arange(half, dtype=F32) / half)
    ang = jnp.arange(seq, dtype=F32)[:, None] * inv_freq[None, :]
    cos = jnp.cos(ang)
    sin = jnp.sin(ang)
    return jnp.concatenate([cos, cos], axis=1), jnp.concatenate([-sin, sin], axis=1)


def kernel(x, p, norm_mix_pre, norm_mix_post, norm_ffn_pre, norm_ffn_post, norm_ple_pre, norm_ple_post, w_in, w_merge_gate, lru_conv_w, lru_conv_b, lru_w_r, lru_b_r, lru_w_i, lru_b_i, lru_lambda, rwkv_mu, rwkv_w0, rwkv_w_up, rwkv_a0, rwkv_a_up, rwkv_g_up, rwkv_k_k, rwkv_k_a, rwkv_r_k, rwkv_gn_w, rwkv_gn_b, rwkv_v0, rwkv_v_down, rwkv_v_up, w_branch_a, w_branch_b, w_branch_c, w_branch_d, w_out, w_ffn_up, ffn_conv_w, ffn_conv_b, w_ffn_down, w_ple, w_ple_gate):
    batch, seq, d = x.shape
    depth = w_in.shape[0]
    t = batch * seq
    lru_w = lru_conv_w.shape[2]
    rw = rwkv_w0.shape[1]
    dil_w = 3 * len(DIL_CONFIGS) * DIL_HEADS * HEAD_DIM
    sb_w = 3 * (d // 2)
    off_b = 2 * lru_w
    off_c = off_b + dil_w
    off_d = off_c + sb_w
    rwkv_in = w_in.shape[2] - off_d
    rwkv_pad = 3 * rw + 4 * LANES
    cos, sin = _rope_tables(seq)

    xf = x.reshape(t, d)
    h = rmsnorm_bf16(xf, norm_mix_pre[0])
    v_first = None
    for i in range(depth):
        wi = w_in[i]
        w_a = wi[:, :off_b].astype(BF16)
        w_b = wi[:, off_b:off_c].astype(BF16)
        w_c = wi[:, off_c:off_d].astype(BF16)
        w_d = _pad_cols(wi[:, off_d:], rwkv_pad).astype(BF16)
        seg_a = matmul(h, w_a, F32)
        seg_b = matmul_rope(h, w_b, cos, sin, seq, 2 * dil_w // 3, BF16)
        seg_c = matmul(h, w_c, BF16)
        seg_d = matmul(h, w_d, F32, tn=rwkv_pad // 2)
        y_a = rglru(seg_a.reshape(batch, seq, off_b), lru_conv_w[i], lru_conv_b[i], lru_w_r[i].astype(BF16),
                    lru_b_r[i], lru_w_i[i].astype(BF16), lru_b_i[i], lru_lambda[i]).reshape(t, lru_w)
        y_b = dilated_attention(seg_b, batch, seq)
        y_c = stick_breaking(seg_c, batch, seq)
        mu = jnp.pad(rwkv_mu[i], (0, rwkv_pad - rwkv_in))
        w_up = _pad_rows(rwkv_w_up[i], LANES).astype(BF16)
        a_up = jnp.pad(rwkv_a_up[i], ((RWKV_W_LORA, LANES - RWKV_W_LORA - RWKV_A_LORA), (0, 0))).astype(BF16)
        g_up = _pad_rows(rwkv_g_up[i], 2 * LANES).astype(BF16)
        v_res = None
        if i > 0:
            v_res = (rwkv_v0[i - 1], _pad_cols(rwkv_v_down[i - 1], LANES).astype(BF16),
                     _pad_rows(rwkv_v_up[i - 1], LANES).astype(BF16))
        r_, k_, v_, lw_, kk_, b_, g_, bon_ = rwkv_prep(
            seg_d, seq, mu, rwkv_w0[i], w_up, rwkv_a0[i], a_up, g_up, rwkv_k_k[i], rwkv_k_a[i],
            rwkv_r_k[i].reshape(-1), v_first, v_res)
        if i == 0:
            v_first = v_
        y_d = rwkv_chunks(r_, k_, v_, lw_, kk_, b_, g_, bon_, rwkv_gn_w[i], rwkv_gn_b[i], batch, seq)
        merged = merge_branches(
            h, (y_a, y_b, y_c, y_d), w_merge_gate[i].astype(BF16),
            (w_branch_a[i].astype(BF16), w_branch_b[i].astype(BF16), w_branch_c[i].astype(BF16),
             w_branch_d[i].astype(BF16)))
        xf, h = matmul_norm_res(merged, w_out[i].astype(BF16), xf, norm_mix_post[i], norm_ffn_pre[i], tm=512, tk=d)
        act = ffn_up(h, w_ffn_up, i, ffn_conv_w[i], ffn_conv_b[i], seq)
        xf, h = matmul_norm_res(act, w_ffn_down[i].astype(BF16), xf, norm_ffn_post[i], norm_ple_pre[i],
                                tm=512, tk=w_ffn_down.shape[1] // 4)
        g_next = norm_mix_pre[i + 1] if i + 1 < depth else norm_mix_pre[i]
        xf, h = ple_norm_res(p[i].reshape(t, -1).astype(BF16), h, w_ple[i].astype(BF16),
                             w_ple_gate[i].astype(BF16), xf, norm_ple_post[i], g_next)
    return xf.reshape(batch, seq, d)
```

```python
import functools

import jax
import jax.numpy as jnp
from jax import lax
from jax.experimental import pallas as pl
from jax.experimental.pallas import tpu as pltpu

F32 = jnp.float32
BF16 = jnp.bfloat16

LANES = 128
SUBLANES = 8
VMEM_LIMIT_BYTES = 52 * 1024 * 1024

HEAD_DIM = 128
BLOCK = 128
ROPE_THETA = 10000.0
RMS_EPS = 1e-6
NEG_INF = -1e30

LRU_BLOCKS = 8
LRU_CONV = 4
LRU_C = 8.0
DIL_CONFIGS = ((128, 1), (512, 4), (2048, 16))
DIL_HEADS = 4
RWKV_HEAD = 64
RWKV_W_LORA = 64
RWKV_A_LORA = 64
RWKV_G_LORA = 160
RWKV_V_LORA = 32
RWKV_GN_EPS = 64e-5
RWKV_CHUNK = 64
RWKV_CHUNKS_PER_STEP = 2
FFN_CONV = 3
SB_KEYS = 256
EPILOGUE_ROWS = 256


def _cparams(*sem):
    return pltpu.CompilerParams(dimension_semantics=sem, vmem_limit_bytes=VMEM_LIMIT_BYTES)


def _dot(a, b):
    return jnp.dot(a, b, preferred_element_type=F32)


def _dot_nt(a, b):
    return lax.dot_general(a, b, (((1,), (1,)), ((), ())), preferred_element_type=F32)


def _dot_tn(a, b):
    return lax.dot_general(a, b, (((0,), (0,)), ((), ())), preferred_element_type=F32)


def _split_bf16(x):
    hi = x.astype(BF16)
    lo = (x - hi.astype(F32)).astype(BF16)
    return hi, lo


def _dot_exact_rhs(x, m_bf16):
    hi, lo = _split_bf16(x)
    return _dot(hi, m_bf16) + _dot(lo, m_bf16)


def _rms(x, g):
    return x * lax.rsqrt(jnp.mean(x * x, axis=-1, keepdims=True) + RMS_EPS) * g


def _gelu(x):
    return jax.nn.gelu(x, approximate=True)


def _softplus(x):
    return jnp.maximum(x, 0.0) + jnp.log1p(jnp.exp(-jnp.abs(x)))


def _row_tiles(rows, sub=EPILOGUE_ROWS):
    sub = min(sub, rows)
    return [slice(s, s + sub) for s in range(0, rows, sub)]


def _rmsnorm_kernel(x_ref, g_ref, o_ref):
    o_ref[...] = _rms(x_ref[...], g_ref[...]).astype(o_ref.dtype)


def rmsnorm_bf16(x, g, tm=512):
    t, d = x.shape
    return pl.pallas_call(
        _rmsnorm_kernel,
        out_shape=jax.ShapeDtypeStruct((t, d), BF16),
        grid=(t // tm,),
        in_specs=[pl.BlockSpec((tm, d), lambda i: (i, 0)), pl.BlockSpec((1, d), lambda i: (0, 0))],
        out_specs=pl.BlockSpec((tm, d), lambda i: (i, 0)),
        compiler_params=_cparams("parallel"),
        name="rmsnorm",
    )(x, g.reshape(1, d))


def _mm_kernel(a_ref, w_ref, o_ref):
    o_ref[...] = _dot(a_ref[...], w_ref[...]).astype(o_ref.dtype)


def _mm_rope_kernel(a_ref, w_ref, cos_ref, sin_ref, o_ref, *, n_rope_blocks, tn):
    j = pl.program_id(1)
    row_tiles = _row_tiles(a_ref.shape[0])

    @pl.when(j < n_rope_blocks)
    def _():
        for rows in row_tiles:
            acc = _dot(a_ref[rows, :], w_ref[...])
            cos = cos_ref[rows, :]
            sin = sin_ref[rows, :]
            for c in range(tn // HEAD_DIM):
                seg = acc[:, c * HEAD_DIM:(c + 1) * HEAD_DIM]
                rot = pltpu.roll(seg, HEAD_DIM // 2, axis=1)
                o_ref[rows, c * HEAD_DIM:(c + 1) * HEAD_DIM] = (seg * cos + rot * sin).astype(o_ref.dtype)

    @pl.when(j >= n_rope_blocks)
    def _():
        o_ref[...] = _dot(a_ref[...], w_ref[...]).astype(o_ref.dtype)


def matmul(a, w, out_dtype, tm=1024, tn=1024):
    m, k = a.shape
    n = w.shape[1]
    tm, tn = min(tm, m), min(tn, n)
    return pl.pallas_call(
        _mm_kernel,
        out_shape=jax.ShapeDtypeStruct((m, n), out_dtype),
        grid=(m // tm, n // tn),
        in_specs=[pl.BlockSpec((tm, k), lambda i, j: (i, 0)), pl.BlockSpec((k, tn), lambda i, j: (0, j))],
        out_specs=pl.BlockSpec((tm, tn), lambda i, j: (i, j)),
        compiler_params=_cparams("parallel", "arbitrary"),
        name="matmul_plain",
    )(a, w)


def matmul_rope(a, w, cos, sin, seq, n_rope_cols, out_dtype, tm=1024, tn=1536):
    m, k = a.shape
    n = w.shape[1]
    tm, tn = min(tm, m), min(tn, n)
    sblocks = seq // tm
    return pl.pallas_call(
        functools.partial(_mm_rope_kernel, n_rope_blocks=n_rope_cols // tn, tn=tn),
        out_shape=jax.ShapeDtypeStruct((m, n), out_dtype),
        grid=(m // tm, n // tn),
        in_specs=[
            pl.BlockSpec((tm, k), lambda i, j: (i, 0)),
            pl.BlockSpec((k, tn), lambda i, j: (0, j)),
            pl.BlockSpec((tm, HEAD_DIM), lambda i, j: (i % sblocks, 0)),
            pl.BlockSpec((tm, HEAD_DIM), lambda i, j: (i % sblocks, 0)),
        ],
        out_specs=pl.BlockSpec((tm, tn), lambda i, j: (i, j)),
        compiler_params=_cparams("parallel", "arbitrary"),
        name="matmul_rope",
    )(a, w, cos, sin)


def _mm_norm_res_kernel(a_ref, w_ref, x_ref, gpost_ref, gnext_ref, xo_ref, ho_ref, *acc, nk):
    row_tiles = _row_tiles(a_ref.shape[0])

    def finish(rows, val):
        xn = x_ref[rows, :] + _rms(val, gpost_ref[...])
        xo_ref[rows, :] = xn
        ho_ref[rows, :] = _rms(xn, gnext_ref[...]).astype(ho_ref.dtype)

    if nk == 1:
        for rows in row_tiles:
            finish(rows, _dot(a_ref[rows, :], w_ref[...]))
        return
    acc_ref, = acc
    kk = pl.program_id(1)

    @pl.when(kk == 0)
    def _():
        acc_ref[...] = _dot(a_ref[...], w_ref[...])

    @pl.when((kk > 0) & (kk < nk - 1))
    def _():
        acc_ref[...] += _dot(a_ref[...], w_ref[...])

    @pl.when(kk == nk - 1)
    def _():
        for rows in row_tiles:
            finish(rows, acc_ref[rows, :] + _dot(a_ref[rows, :], w_ref[...]))


def matmul_norm_res(a, w, x, g_post, g_next, tm=256, tk=512):
    m, k = a.shape
    d = w.shape[1]
    tm = min(tm, m)
    nk = k // tk
    return pl.pallas_call(
        functools.partial(_mm_norm_res_kernel, nk=nk),
        out_shape=(jax.ShapeDtypeStruct((m, d), F32), jax.ShapeDtypeStruct((m, d), BF16)),
        grid=(m // tm, nk),
        in_specs=[
            pl.BlockSpec((tm, tk), lambda i, kk: (i, kk)),
            pl.BlockSpec((tk, d), lambda i, kk: (kk, 0)),
            pl.BlockSpec((tm, d), lambda i, kk: (i, 0)),
            pl.BlockSpec((1, d), lambda i, kk: (0, 0)),
            pl.BlockSpec((1, d), lambda i, kk: (0, 0)),
        ],
        out_specs=(pl.BlockSpec((tm, d), lambda i, kk: (i, 0)), pl.BlockSpec((tm, d), lambda i, kk: (i, 0))),
        scratch_shapes=[pltpu.VMEM((tm, d), F32)] if nk > 1 else [],
        compiler_params=_cparams("parallel", "arbitrary"),
        name="matmul_norm_res",
    )(a, w, x, g_post.reshape(1, d), g_next.reshape(1, d))


def _ple_kernel(p_ref, h_ref, wp_ref, wg_ref, x_ref, gpost_ref, gnext_ref, xo_ref, ho_ref):
    for rows in _row_tiles(x_ref.shape[0]):
        val = _dot(p_ref[rows, :], wp_ref[...]) * jax.nn.sigmoid(_dot(h_ref[rows, :], wg_ref[...]))
        xn = x_ref[rows, :] + _rms(val, gpost_ref[...])
        xo_ref[rows, :] = xn
        ho_ref[rows, :] = _rms(xn, gnext_ref[...]).astype(ho_ref.dtype)


def ple_norm_res(p, h, w_ple, w_gate, x, g_post, g_next, tm=512):
    m, d = x.shape
    tm = min(tm, m)
    pd = p.shape[1]
    row = lambda i: (i, 0)
    fix = lambda i: (0, 0)
    return pl.pallas_call(
        _ple_kernel,
        out_shape=(jax.ShapeDtypeStruct((m, d), F32), jax.ShapeDtypeStruct((m, d), BF16)),
        grid=(m // tm,),
        in_specs=[
            pl.BlockSpec((tm, pd), row), pl.BlockSpec((tm, d), row),
            pl.BlockSpec((pd, d), fix), pl.BlockSpec((d, d), fix),
            pl.BlockSpec((tm, d), row), pl.BlockSpec((1, d), fix), pl.BlockSpec((1, d), fix),
        ],
        out_specs=(pl.BlockSpec((tm, d), row), pl.BlockSpec((tm, d), row)),
        compiler_params=_cparams("parallel"),
        name="ple_norm_res",
    )(p, h, w_ple, w_gate, x, g_post.reshape(1, d), g_next.reshape(1, d))


def _merge_kernel(h_ref, ya_ref, yb_ref, yc_ref, yd_ref, ga_ref, gb_ref, gc_ref, gd_ref,
                  wa_ref, wb_ref, wc_ref, wd_ref, o_ref):
    h = h_ref[...]
    acc = jax.nn.sigmoid(_dot(h, ga_ref[...])) * _dot(ya_ref[...], wa_ref[...])
    acc += jax.nn.sigmoid(_dot(h, gb_ref[...])) * _dot(yb_ref[...], wb_ref[...])
    acc += jax.nn.sigmoid(_dot(h, gc_ref[...])) * _dot(yc_ref[...], wc_ref[...])
    acc += jax.nn.sigmoid(_dot(h, gd_ref[...])) * _dot(yd_ref[...], wd_ref[...])
    o_ref[...] = acc.astype(o_ref.dtype)


def merge_branches(h, ys, w_gate, ws, tm=1024, tn=256):
    m, dm = h.shape
    tm = min(tm, m)
    d = ws[0].shape[1]
    nb = d // tn
    in_specs = [pl.BlockSpec((tm, dm), lambda i, j: (i, 0))]
    in_specs += [pl.BlockSpec((tm, y.shape[1]), lambda i, j: (i, 0)) for y in ys]
    in_specs += [pl.BlockSpec((dm, tn), functools.partial(lambda i, j, b: (0, b * nb + j), b=b)) for b in range(4)]
    in_specs += [pl.BlockSpec((w.shape[0], tn), lambda i, j: (0, j)) for w in ws]
    return pl.pallas_call(
        _merge_kernel,
        out_shape=jax.ShapeDtypeStruct((m, d), BF16),
        grid=(m // tm, nb),
        in_specs=in_specs,
        out_specs=pl.BlockSpec((tm, tn), lambda i, j: (i, j)),
        compiler_params=_cparams("parallel", "arbitrary"),
        name="merge_branches",
    )(h, *ys, w_gate, w_gate, w_gate, w_gate, *ws)


def _ffn_up_kernel(h_ref, wg32_ref, wu32_ref, cwg_ref, cwu_ref, cbg_ref, cbu_ref, o_ref, bufg_ref, bufu_ref,
                   wg_ref, wu_ref, *, tm, seq_blocks):
    i = pl.program_id(1)

    @pl.when(i == 0)
    def _():
        wg_ref[...] = wg32_ref[...].astype(BF16)
        wu_ref[...] = wu32_ref[...].astype(BF16)

    @pl.when(i % seq_blocks == 0)
    def _():
        bufg_ref[0:SUBLANES, :] = jnp.zeros((SUBLANES, bufg_ref.shape[1]), F32)
        bufu_ref[0:SUBLANES, :] = jnp.zeros((SUBLANES, bufu_ref.shape[1]), F32)

    h = h_ref[...]

    def conv(w_ref, cw_ref, cb_ref, buf_ref):
        buf_ref[SUBLANES:SUBLANES + tm, :] = _dot(h, w_ref[...])
        cw = cw_ref[...]
        out = cb_ref[...] + cw[2:3, :] * buf_ref[SUBLANES:SUBLANES + tm, :]
        out += cw[1:2, :] * buf_ref[SUBLANES - 1:SUBLANES - 1 + tm, :]
        out += cw[0:1, :] * buf_ref[SUBLANES - 2:SUBLANES - 2 + tm, :]
        buf_ref[0:SUBLANES, :] = buf_ref[tm:tm + SUBLANES, :]
        return out

    g = conv(wg_ref, cwg_ref, cbg_ref, bufg_ref)
    u = conv(wu_ref, cwu_ref, cbu_ref, bufu_ref)
    o_ref[...] = (_gelu(g) * u).astype(o_ref.dtype)


def ffn_up(h, w_up_all, layer, conv_w, conv_b, seq, tm=1024, tn=512):
    m, d = h.shape
    tm = min(tm, seq)
    dff = w_up_all.shape[2] // 2
    nb = dff // tn
    cb = conv_b.reshape(1, 2 * dff)
    return pl.pallas_call(
        functools.partial(_ffn_up_kernel, tm=tm, seq_blocks=seq // tm),
        out_shape=jax.ShapeDtypeStruct((m, dff), BF16),
        grid=(nb, m // tm),
        in_specs=[
            pl.BlockSpec((tm, d), lambda j, i: (i, 0)),
            pl.BlockSpec((None, d, tn), lambda j, i: (layer, 0, j)),
            pl.BlockSpec((None, d, tn), lambda j, i: (layer, 0, j + nb)),
            pl.BlockSpec((FFN_CONV, tn), lambda j, i: (0, j)),
            pl.BlockSpec((FFN_CONV, tn), lambda j, i: (0, j + nb)),
            pl.BlockSpec((1, tn), lambda j, i: (0, j)),
            pl.BlockSpec((1, tn), lambda j, i: (0, j + nb)),
        ],
        out_specs=pl.BlockSpec((tm, tn), lambda j, i: (i, j)),
        scratch_shapes=[pltpu.VMEM((tm + SUBLANES, tn), F32), pltpu.VMEM((tm + SUBLANES, tn), F32),
                        pltpu.VMEM((d, tn), BF16), pltpu.VMEM((d, tn), BF16)],
        compiler_params=_cparams("parallel", "arbitrary"),
        name="ffn_up_conv_glu",
    )(h, w_up_all, w_up_all, conv_w, conv_w, cb, cb)


def _lru_kernel(x_ref, gate_ref, cw_ref, cb_ref, wr_ref, br_ref, wi_ref, bi_ref, lam_ref, o_ref,
                xbuf_ref, a0_ref, h0_ref, a1_ref, h1_ref, *, seq, pad):
    xbuf_ref[0:SUBLANES, :] = jnp.zeros((SUBLANES, LANES), F32)
    xbuf_ref[SUBLANES:SUBLANES + seq, :] = x_ref[0]
    cw = cw_ref[...]
    u = cb_ref[...] + cw[3:4, :] * xbuf_ref[SUBLANES:SUBLANES + seq, :]
    for k in range(LRU_CONV - 1):
        off = SUBLANES - (LRU_CONV - 1) + k
        u += cw[k:k + 1, :] * xbuf_ref[off:off + seq, :]
    ub = u.astype(BF16)
    r = jax.nn.sigmoid(_dot(ub, wr_ref[0]) + br_ref[...])
    ig = jax.nn.sigmoid(_dot(ub, wi_ref[0]) + bi_ref[...])
    log_a = -LRU_C * r * _softplus(-lam_ref[...])
    a = jnp.exp(log_a)
    inp = jnp.sqrt(1.0 - a * a) * ig * u

    ones = jnp.ones((pad, LANES), F32)
    zeros = jnp.zeros((pad, LANES), F32)
    a0_ref[0:pad, :] = ones
    a1_ref[0:pad, :] = ones
    h0_ref[0:pad, :] = zeros
    h1_ref[0:pad, :] = zeros
    a0_ref[pad:pad + seq, :] = a
    h0_ref[pad:pad + seq, :] = inp
    bufs = ((a0_ref, h0_ref), (a1_ref, h1_ref))
    d = 1
    level = 0
    while d < seq:
        (a_src, h_src), (a_dst, h_dst) = bufs[level % 2], bufs[(level + 1) % 2]
        a_cur = a_src[pad:pad + seq, :]
        h_dst[pad:pad + seq, :] = h_src[pad:pad + seq, :] + a_cur * h_src[pad - d:pad - d + seq, :]
        a_dst[pad:pad + seq, :] = a_cur * a_src[pad - d:pad - d + seq, :]
        d *= 2
        level += 1
    h = bufs[level % 2][1][pad:pad + seq, :]
    o_ref[0] = (h * _gelu(gate_ref[0])).astype(o_ref.dtype)


def rglru(xg, conv_w, conv_b, w_r, b_r, w_i, b_i, lam):
    b, s, w2 = xg.shape
    w = w2 // 2
    nblk = w // LANES
    pad = s
    vec = lambda v: v.reshape(1, w)
    vspec = pl.BlockSpec((1, LANES), lambda bi, c: (0, c))
    return pl.pallas_call(
        functools.partial(_lru_kernel, seq=s, pad=pad),
        out_shape=jax.ShapeDtypeStruct((b, s, w), BF16),
        grid=(b, nblk),
        in_specs=[
            pl.BlockSpec((1, s, LANES), lambda bi, c: (bi, 0, c)),
            pl.BlockSpec((1, s, LANES), lambda bi, c: (bi, 0, c + nblk)),
            pl.BlockSpec((LRU_CONV, LANES), lambda bi, c: (0, c)),
            vspec,
            pl.BlockSpec((1, LANES, LANES), lambda bi, c: (c, 0, 0)),
            vspec,
            pl.BlockSpec((1, LANES, LANES), lambda bi, c: (c, 0, 0)),
            vspec, vspec,
        ],
        out_specs=pl.BlockSpec((1, s, LANES), lambda bi, c: (bi, 0, c)),
        scratch_shapes=[pltpu.VMEM((s + SUBLANES, LANES), F32)] + [pltpu.VMEM((pad + s, LANES), F32)] * 4,
        compiler_params=_cparams("parallel", "parallel"),
        name="rglru",
    )(xg, xg, conv_w, vec(conv_b), w_r, vec(b_r), w_i, vec(b_i), vec(lam))


DIL_UNITS_PER_PHASE = 4


def _dil_kernel(*refs, seq):
    ngroups = len(DIL_CONFIGS)
    qkv_refs = refs[:3 * ngroups]
    o_ref = refs[3 * ngroups]
    qf_ref, kf_ref, vf_ref, acc_ref, m_ref, l_ref = refs[3 * ngroups + 1:]
    scale = HEAD_DIM ** -0.5
    row = lax.broadcasted_iota(jnp.int32, (BLOCK, 2 * BLOCK), 0)
    col = lax.broadcasted_iota(jnp.int32, (BLOCK, 2 * BLOCK), 1)
    bias_two = jnp.where((col >= row) & (col <= row + BLOCK), 0.0, NEG_INF)
    bias_own = jnp.where(lax.broadcasted_iota(jnp.int32, (BLOCK, BLOCK), 1)
                         <= lax.broadcasted_iota(jnp.int32, (BLOCK, BLOCK), 0), 0.0, NEG_INF)

    def run_units(g, units):
        s = [_dot_nt(q, k) * scale + (bias_own if k.shape[0] == BLOCK else bias_two) for q, k, _, _ in units]
        mx = [jnp.max(x, axis=-1, keepdims=True) for x in s]
        e = [jnp.exp(x - m).astype(BF16) for x, m in zip(s, mx)]
        pv = [_dot(p, jnp.concatenate([v, jnp.ones_like(v)], axis=1)) for p, (_, _, v, _) in zip(e, units)]
        for y, m, (_, _, _, rows) in zip(pv, mx, units):
            acc_ref[g, rows, :] = y[:, :HEAD_DIM]
            l_ref[g, rows, :] = y[:, HEAD_DIM:]
            m_ref[g, rows, :] = jnp.broadcast_to(m, (BLOCK, HEAD_DIM))

    for g, (_, d) in enumerate(DIL_CONFIGS):
        q_ref, k_ref, v_ref = qkv_refs[3 * g:3 * g + 3]
        ln = seq // d
        nblk = ln // BLOCK
        if d > 1:
            qf_ref[...] = q_ref[...].astype(F32)
            kf_ref[...] = k_ref[...].astype(F32)
            vf_ref[...] = v_ref[...].astype(F32)
        units = []
        for r in range(d):
            if d > 1:
                qr = qf_ref[pl.ds(r, ln, stride=d), :].astype(BF16)
                kr = kf_ref[pl.ds(r, ln, stride=d), :].astype(BF16)
                vr = vf_ref[pl.ds(r, ln, stride=d), :].astype(BF16)
            for nb in range(nblk):
                lo = max(nb - 1, 0) * BLOCK
                hi = (nb + 1) * BLOCK
                if d > 1:
                    unit = (qr[nb * BLOCK:hi], kr[lo:hi], vr[lo:hi], pl.ds(nb * BLOCK * d + r, BLOCK, stride=d))
                else:
                    unit = (q_ref[nb * BLOCK:hi, :], k_ref[lo:hi, :], v_ref[lo:hi, :], pl.ds(nb * BLOCK, BLOCK))
                units.append(unit)
                if len(units) == DIL_UNITS_PER_PHASE:
                    run_units(g, units)
                    units = []
        if units:
            run_units(g, units)

    rows_per_step = 2 * BLOCK
    for c in range(seq // rows_per_step):
        rs = slice(c * rows_per_step, (c + 1) * rows_per_step)
        ms = [m_ref[g, rs, :] for g in range(ngroups)]
        top = functools.reduce(jnp.maximum, ms)
        ws = [jnp.exp(m - top) for m in ms]
        num = functools.reduce(jnp.add, [w * acc_ref[g, rs, :] for g, w in enumerate(ws)])
        den = functools.reduce(jnp.add, [w * l_ref[g, rs, :] for g, w in enumerate(ws)])
        o_ref[rs, :] = (num / den).astype(o_ref.dtype)


def dilated_attention(qkv, batch, seq):
    assert all(window // d == BLOCK and seq % (BLOCK * d) == 0 for window, d in DIL_CONFIGS)
    t, width = qkv.shape
    ngroups = len(DIL_CONFIGS)
    nheads = width // (3 * HEAD_DIM)

    def spec(which, g):
        return pl.BlockSpec((seq, HEAD_DIM), lambda b, h: (b, which * nheads + g * DIL_HEADS + h))

    in_specs = [spec(which, g) for g in range(ngroups) for which in range(3)]
    return pl.pallas_call(
        functools.partial(_dil_kernel, seq=seq),
        out_shape=jax.ShapeDtypeStruct((t, DIL_HEADS * HEAD_DIM), BF16),
        grid=(batch, DIL_HEADS),
        in_specs=in_specs,
        out_specs=pl.BlockSpec((seq, HEAD_DIM), lambda b, h: (b, h)),
        scratch_shapes=[pltpu.VMEM((seq, HEAD_DIM), F32)] * 3 + [pltpu.VMEM((ngroups, seq, HEAD_DIM), F32)] * 3,
        compiler_params=_cparams("parallel", "parallel"),
        name="dilated_attention",
    )(*([qkv] * (3 * ngroups)))


def _sb_kernel(q_ref, k_ref, v_ref, o_ref, acc_ref, run_ref):
    n = pl.program_id(1)
    nh = q_ref.shape[1] // HEAD_DIM
    heads = range(nh)
    sls = [slice(h * HEAD_DIM, (h + 1) * HEAD_DIM) for h in heads]
    scale = HEAD_DIM ** -0.5
    kb = SB_KEYS
    assert kb == 2 * BLOCK
    from_s = (lax.broadcasted_iota(jnp.int32, (kb, kb), 0) >= lax.broadcasted_iota(jnp.int32, (kb, kb), 1)).astype(BF16)
    off = (n % 2) * BLOCK
    before = (lax.broadcasted_iota(jnp.int32, (BLOCK, kb), 1) - lax.broadcasted_iota(jnp.int32, (BLOCK, kb), 0)) < off
    q = [q_ref[:, sl] for sl in sls]

    def block(jb, diag):
        start = pl.multiple_of(jb * kb, kb)
        z = [_dot_nt(q[h], k_ref[pl.ds(start, kb), sls[h]]) * scale for h in heads]
        log_beta = [jnp.minimum(zz, 0.0) - jnp.log(1.0 + jnp.exp(-jnp.abs(zz))) for zz in z]
        log_1m = [lb - zz for lb, zz in zip(log_beta, z)]
        if diag:
            log_1m = [jnp.where(before, x, 0.0) for x in log_1m]
        incl = [_dot_exact_rhs(x, from_s) for x in log_1m]
        for h in heads:
            att = jnp.exp(z[h] + incl[h]) if diag else jnp.exp(z[h] + incl[h] + run_ref[h])
            if diag:
                att = jnp.where(before, att, 0.0)
            pv = _dot(att.astype(BF16), v_ref[pl.ds(start, kb), sls[h]])
            total = jnp.broadcast_to(incl[h][:, 0:1], (BLOCK, kb))
            if diag:
                acc_ref[:, sls[h]] = pv
                run_ref[h] = total
            else:
                acc_ref[:, sls[h]] += pv
                run_ref[h] = run_ref[h] + total

    full = n // 2
    block(full, True)

    def body(t, carry):
        block(full - 1 - t, False)
        return carry

    lax.fori_loop(0, full, body, 0)
    o_ref[...] = acc_ref[...].astype(o_ref.dtype)


def stick_breaking(qkv, batch, seq):
    t, width = qkv.shape
    w = width // 3
    nq = seq // BLOCK
    assert seq % SB_KEYS == 0
    return pl.pallas_call(
        _sb_kernel,
        out_shape=jax.ShapeDtypeStruct((t, w), BF16),
        grid=(batch, nq),
        in_specs=[
            pl.BlockSpec((BLOCK, w), lambda b, n: (b * nq + n, 0)),
            pl.BlockSpec((seq, w), lambda b, n: (b, 1)),
            pl.BlockSpec((seq, w), lambda b, n: (b, 2)),
        ],
        out_specs=pl.BlockSpec((BLOCK, w), lambda b, n: (b * nq + n, 0)),
        scratch_shapes=[pltpu.VMEM((BLOCK, w), F32), pltpu.VMEM((w // HEAD_DIM, BLOCK, SB_KEYS), F32)],
        compiler_params=_cparams("parallel", "arbitrary"),
        name="stick_breaking",
    )(qkv, qkv, qkv)


def _head_sum(x, bd):
    cols = []
    for c in range(x.shape[1] // LANES):
        cols.append(_dot_exact_rhs(x[:, c * LANES:(c + 1) * LANES], bd))
    return jnp.concatenate(cols, axis=1)


def _rwkv_prep_kernel(*refs, tm, width, seq_blocks, has_vres):
    if has_vres:
        (h_ref, wd_ref, mu_ref, w0_ref, wup_ref, a0_ref, aup_ref, gup_ref, kk_ref, ka_ref, rk_ref,
         vf_ref, v0_ref, vdn_ref, vup_ref,
         r_o, k_o, v_o, lw_o, kk_o, b_o, g_o, bon_o, buf_ref) = refs
    else:
        (h_ref, wd_ref, mu_ref, w0_ref, wup_ref, a0_ref, aup_ref, gup_ref, kk_ref, ka_ref, rk_ref,
         r_o, k_o, v_o, lw_o, kk_o, b_o, g_o, bon_o, buf_ref) = refs
    i = pl.program_id(0)

    @pl.when(i % seq_blocks == 0)
    def _():
        buf_ref[0:SUBLANES, :] = jnp.zeros((SUBLANES, buf_ref.shape[1]), F32)

    row = lax.broadcasted_iota(jnp.int32, (LANES, LANES), 0) // RWKV_HEAD
    col = lax.broadcasted_iota(jnp.int32, (LANES, LANES), 1) // RWKV_HEAD
    bd = (row == col).astype(BF16)
    w = width
    for rows in _row_tiles(tm):
        lo, n = SUBLANES + rows.start, rows.stop - rows.start
        seg = _dot(h_ref[rows, :], wd_ref[...])
        buf_ref[lo:lo + n, :] = seg
        shifted = buf_ref[lo - 1:lo - 1 + n, :]
        xs = seg + (shifted - seg) * mu_ref[...]
        r = xs[:, 0:w]
        k = xs[:, w:2 * w]
        v = xs[:, 2 * w:3 * w]
        low = xs[:, 3 * w:3 * w + LANES]
        g_low = xs[:, 3 * w + LANES:3 * w + 3 * LANES]
        wpre = w0_ref[...] + _dot(jnp.tanh(low).astype(BF16), wup_ref[...])
        wlog = -_softplus(-wpre) - 0.5
        lw_o[rows, :] = -jnp.exp(wlog)
        a = jax.nn.sigmoid(a0_ref[...] + _dot(low.astype(BF16), aup_ref[...]))
        g_o[rows, :] = _dot(jax.nn.sigmoid(g_low).astype(BF16), gup_ref[...]).astype(g_o.dtype)
        if has_vres:
            mix = jax.nn.sigmoid(v0_ref[...]
                                 + _dot(_dot(v.astype(BF16), vdn_ref[...]).astype(BF16), vup_ref[...]))
            v = v + (vf_ref[rows, :] - v) * mix
        kk = k * kk_ref[...]
        norm = jnp.sqrt(_head_sum(kk * kk, bd))
        kk = kk / jnp.maximum(norm, 1e-12)
        k2 = k * (1.0 + (a - 1.0) * ka_ref[...])
        bonus = _head_sum(r * k2 * rk_ref[...], bd) * v
        r_o[rows, :] = r.astype(r_o.dtype)
        k_o[rows, :] = k2.astype(k_o.dtype)
        v_o[rows, :] = v
        kk_o[rows, :] = kk.astype(kk_o.dtype)
        b_o[rows, :] = (kk * a).astype(b_o.dtype)
        bon_o[rows, :] = bonus.astype(bon_o.dtype)
    buf_ref[0:SUBLANES, :] = buf_ref[tm:tm + SUBLANES, :]


def rwkv_prep(h, w_d, seq, mu, w0, w_up, a0, a_up, g_up, k_k, k_a, r_k, v_first, v_res, tm=256):
    t, d = h.shape
    wpad = w_d.shape[1]
    w = w0.shape[0]
    has_vres = v_res is not None
    row = lambda i: (i, 0)
    fix = lambda i: (0, 0)
    vec = lambda x: x.reshape(1, -1)
    in_specs = [
        pl.BlockSpec((tm, d), row),
        pl.BlockSpec((d, wpad), fix),
        pl.BlockSpec((1, wpad), fix), pl.BlockSpec((1, w), fix), pl.BlockSpec((LANES, w), fix),
        pl.BlockSpec((1, w), fix), pl.BlockSpec((LANES, w), fix), pl.BlockSpec((2 * LANES, w), fix),
        pl.BlockSpec((1, w), fix), pl.BlockSpec((1, w), fix), pl.BlockSpec((1, w), fix),
    ]
    args = [h, w_d, vec(mu), vec(w0), w_up, vec(a0), a_up, g_up, vec(k_k), vec(k_a), vec(r_k)]
    if has_vres:
        v0, v_down, v_up = v_res
        in_specs += [pl.BlockSpec((tm, w), row), pl.BlockSpec((1, w), fix),
                     pl.BlockSpec((w, LANES), fix), pl.BlockSpec((LANES, w), fix)]
        args += [v_first, vec(v0), v_down, v_up]
    outs = tuple(jax.ShapeDtypeStruct((t, w), dt) for dt in (BF16, BF16, F32, F32, BF16, BF16, BF16, BF16))
    return pl.pallas_call(
        functools.partial(_rwkv_prep_kernel, tm=tm, width=w, seq_blocks=seq // tm, has_vres=has_vres),
        out_shape=outs,
        grid=(t // tm,),
        in_specs=in_specs,
        out_specs=(pl.BlockSpec((tm, w), row),) * 8,
        scratch_shapes=[pltpu.VMEM((tm + SUBLANES, wpad), F32)],
        compiler_params=_cparams("arbitrary"),
        name="rwkv_prep",
    )(*args)


def _rwkv_chunk_kernel(r_ref, k_ref, v_ref, lw_ref, kk_ref, b_ref, g_ref, bon_ref, gnw_ref, gnb_ref, o_ref,
                       state_ref):
    c = pl.program_id(1)
    n = RWKV_CHUNK
    n2 = 2 * n

    @pl.when(c == 0)
    def _():
        state_ref[...] = jnp.zeros_like(state_ref)

    tri = (lax.broadcasted_iota(jnp.int32, (n, n), 1) <= lax.broadcasted_iota(jnp.int32, (n, n), 0)).astype(BF16)
    row2 = lax.broadcasted_iota(jnp.int32, (n2, n2), 0)
    col2 = lax.broadcasted_iota(jnp.int32, (n2, n2), 1)
    t2 = row2 & (n - 1)
    s2 = col2 & (n - 1)
    strict = s2 < t2
    incl = s2 <= t2
    eye = (row2 == col2).astype(F32)
    head0 = lax.broadcasted_iota(jnp.int32, (1, LANES), 1) < RWKV_HEAD
    own = jnp.concatenate([jnp.broadcast_to(head0, (n, LANES)), jnp.broadcast_to(~head0, (n, LANES))], axis=0)

    def stack(x):
        return jnp.where(own, jnp.concatenate([x, x], axis=0), 0.0)

    npairs = r_ref.shape[1] // LANES
    nchunks = r_ref.shape[0] // n
    sls = [slice(hp * LANES, (hp + 1) * LANES) for hp in range(npairs)]
    units = [(ci, hp) for ci in range(nchunks) for hp in range(npairs)]
    pre = []
    for ci in range(nchunks):
        rows = slice(ci * n, (ci + 1) * n)
        lw = lw_ref[rows, :]
        gsum = _dot_exact_rhs_left(tri, lw)
        p_inv = jnp.exp(-gsum)
        pre.append(dict(p_end=jnp.exp(gsum[n - 1:n, :]), rt=r_ref[rows, :] * jnp.exp(gsum), kt=k_ref[rows, :] * p_inv,
                        bt=b_ref[rows, :] * p_inv, kap=kk_ref[rows, :] * jnp.exp(gsum - lw), v=v_ref[rows, :]))
    rt = {u: stack(pre[u[0]]["rt"][:, sls[u[1]]]).astype(BF16) for u in units}
    kt = {u: stack(pre[u[0]]["kt"][:, sls[u[1]]]) for u in units}
    bt = {u: stack(pre[u[0]]["bt"][:, sls[u[1]]]) for u in units}
    kap = {u: stack(pre[u[0]]["kap"][:, sls[u[1]]]).astype(BF16) for u in units}
    v2 = {u: stack(pre[u[0]]["v"][:, sls[u[1]]]).astype(BF16) for u in units}
    prod = {u: _dot_nt(jnp.concatenate([kap[u], rt[u]], axis=0),
                       jnp.concatenate([bt[u].astype(BF16), kt[u].astype(BF16)], axis=0)) for u in units}
    a_ab = {u: jnp.where(strict, prod[u][:n2, :n2], 0.0) for u in units}
    a_ak = {u: jnp.where(strict, prod[u][:n2, n2:], 0.0).astype(BF16) for u in units}
    a_r = {u: jnp.concatenate([jnp.where(incl, prod[u][n2:, n2:], 0.0).astype(BF16),
                               jnp.where(incl, -prod[u][n2:, :n2], 0.0).astype(BF16)], axis=1) for u in units}
    x = {u: eye - a_ab[u] for u in units}
    q = {u: _dot(a_ab[u].astype(BF16), a_ab[u].astype(BF16)) for u in units}
    steps = 1
    while True:
        x = {u: x[u] + _dot(x[u].astype(BF16), q[u].astype(BF16)) for u in units}
        steps *= 2
        if steps * 2 >= n:
            break
        q = {u: _dot(q[u].astype(BF16), q[u].astype(BF16)) for u in units}
    inv_n = 1.0 / RWKV_HEAD
    state = [state_ref[hp] for hp in range(npairs)]
    for ci in range(nchunks):
        rows = slice(ci * n, (ci + 1) * n)
        us = [(ci, hp) for hp in range(npairs)]
        s0b = [s.astype(BF16) for s in state]
        rhs = [_dot_nt(kap[u], s0b[u[1]]) + _dot(a_ak[u], v2[u]) for u in us]
        u2 = [_dot(x[u].astype(BF16), r_.astype(BF16)).astype(BF16) for u, r_ in zip(us, rhs)]
        vu = [jnp.concatenate([v2[u], w_], axis=0) for u, w_ in zip(us, u2)]
        y2 = [_dot_nt(rt[u], s0b[u[1]]) + _dot(a_r[u], vu_) for u, vu_ in zip(us, vu)]
        outs = []
        for hp, u in enumerate(us):
            pe = pre[ci]["p_end"][:, sls[hp]]
            kb_end = jnp.concatenate([(kt[u] * pe).astype(BF16), (-(bt[u] * pe)).astype(BF16)], axis=0)
            state[hp] = state[hp] * pe + _dot_tn(vu[hp], kb_end)
            mean = jnp.sum(y2[hp], axis=-1, keepdims=True) * inv_n
            cen = jnp.where(own, y2[hp] - mean, 0.0)
            var = jnp.sum(cen * cen, axis=-1, keepdims=True) * inv_n
            yn2 = cen * lax.rsqrt(var + RWKV_GN_EPS)
            outs.append(yn2[:n] + yn2[n:])
        yn = jnp.concatenate(outs, axis=1) * gnw_ref[...] + gnb_ref[...]
        o_ref[rows, :] = ((yn + bon_ref[rows, :]) * g_ref[rows, :]).astype(o_ref.dtype)
    for hp in range(npairs):
        state_ref[hp] = state[hp]


def _dot_exact_rhs_left(m_bf16, x):
    hi, lo = _split_bf16(x)
    return _dot(m_bf16, hi) + _dot(m_bf16, lo)


def rwkv_chunks(r, k, v, lw, kk, b, g, bon, gn_w, gn_b, batch, seq):
    t, w = r.shape
    rows = RWKV_CHUNKS_PER_STEP * RWKV_CHUNK
    nc = seq // rows
    spec = pl.BlockSpec((rows, w), lambda bi, c: (bi * nc + c, 0))
    vspec = pl.BlockSpec((1, w), lambda bi, c: (0, 0))
    return pl.pallas_call(
        _rwkv_chunk_kernel,
        out_shape=jax.ShapeDtypeStruct((t, w), BF16),
        grid=(batch, nc),
        in_specs=[spec] * 8 + [vspec, vspec],
        out_specs=spec,
        scratch_shapes=[pltpu.VMEM((w // LANES, LANES, LANES), F32)],
        compiler_params=_cparams("parallel", "arbitrary"),
        name="rwkv_chunks",
    )(r, k, v, lw, kk, b, g, bon, gn_w.reshape(1, w), gn_b.reshape(1, w))


def _pad_rows(w, rows):
    return jnp.pad(w, ((0, rows - w.shape[0]), (0, 0)))


def _pad_cols(w, cols):
    return jnp.pad(w, ((0, 0), (0, cols - w.shape[1])))


def _rope_tables(seq):
    half = HEAD_DIM // 2
    inv_freq = ROPE_THETA ** (-jnp.arange(half, dtype=F32) / half)
    ang = jnp.arange(seq, dtype=F32)[:, None] * inv_freq[None, :]
    cos = jnp.cos(ang)
    sin = jnp.sin(ang)
    return jnp.concatenate([cos, cos], axis=1), jnp.concatenate([-sin, sin], axis=1)


def kernel(x, p, norm_mix_pre, norm_mix_post, norm_ffn_pre, norm_ffn_post, norm_ple_pre, norm_ple_post, w_in, w_merge_gate, lru_conv_w, lru_conv_b, lru_w_r, lru_b_r, lru_w_i, lru_b_i, lru_lambda, rwkv_mu, rwkv_w0, rwkv_w_up, rwkv_a0, rwkv_a_up, rwkv_g_up, rwkv_k_k, rwkv_k_a, rwkv_r_k, rwkv_gn_w, rwkv_gn_b, rwkv_v0, rwkv_v_down, rwkv_v_up, w_branch_a, w_branch_b, w_branch_c, w_branch_d, w_out, w_ffn_up, ffn_conv_w, ffn_conv_b, w_ffn_down, w_ple, w_ple_gate):
    batch, seq, d = x.shape
    depth = w_in.shape[0]
    t = batch * seq
    lru_w = lru_conv_w.shape[2]
    rw = rwkv_w0.shape[1]
    dil_w = 3 * len(DIL_CONFIGS) * DIL_HEADS * HEAD_DIM
    sb_w = 3 * (d // 2)
    off_b = 2 * lru_w
    off_c = off_b + dil_w
    off_d = off_c + sb_w
    rwkv_in = w_in.shape[2] - off_d
    rwkv_pad = 3 * rw + 4 * LANES
    cos, sin = _rope_tables(seq)

    xf = x.reshape(t, d)
    h = rmsnorm_bf16(xf, norm_mix_pre[0])
    v_first = None
    for i in range(depth):
        wi = w_in[i]
        w_a = wi[:, :off_b].astype(BF16)
        w_b = wi[:, off_b:off_c].astype(BF16)
        w_c = wi[:, off_c:off_d].astype(BF16)
        w_d = _pad_cols(wi[:, off_d:], rwkv_pad).astype(BF16)
        seg_a = matmul(h, w_a, F32)
        seg_b = matmul_rope(h, w_b, cos, sin, seq, 2 * dil_w // 3, BF16)
        seg_c = matmul(h, w_c, BF16)
        y_a = rglru(seg_a.reshape(batch, seq, off_b), lru_conv_w[i], lru_conv_b[i], lru_w_r[i].astype(BF16),
                    lru_b_r[i], lru_w_i[i].astype(BF16), lru_b_i[i], lru_lambda[i]).reshape(t, lru_w)
        y_b = dilated_attention(seg_b, batch, seq)
        y_c = stick_breaking(seg_c, batch, seq)
        mu = jnp.pad(rwkv_mu[i], (0, rwkv_pad - rwkv_in))
        w_up = _pad_rows(rwkv_w_up[i], LANES).astype(BF16)
        a_up = jnp.pad(rwkv_a_up[i], ((RWKV_W_LORA, LANES - RWKV_W_LORA - RWKV_A_LORA), (0, 0))).astype(BF16)
        g_up = _pad_rows(rwkv_g_up[i], 2 * LANES).astype(BF16)
        v_res = None
        if i > 0:
            v_res = (rwkv_v0[i - 1], _pad_cols(rwkv_v_down[i - 1], LANES).astype(BF16),
                     _pad_rows(rwkv_v_up[i - 1], LANES).astype(BF16))
        r_, k_, v_, lw_, kk_, b_, g_, bon_ = rwkv_prep(
            h, w_d, seq, mu, rwkv_w0[i], w_up, rwkv_a0[i], a_up, g_up, rwkv_k_k[i], rwkv_k_a[i],
            rwkv_r_k[i].reshape(-1), v_first, v_res)
        if i == 0:
            v_first = v_
        y_d = rwkv_chunks(r_, k_, v_, lw_, kk_, b_, g_, bon_, rwkv_gn_w[i], rwkv_gn_b[i], batch, seq)
        merged = merge_branches(
            h, (y_a, y_b, y_c, y_d), w_merge_gate[i].astype(BF16),
            (w_branch_a[i].astype(BF16), w_branch_b[i].astype(BF16), w_branch_c[i].astype(BF16),
             w_branch_d[i].astype(BF16)))
        xf, h = matmul_norm_res(merged, w_out[i].astype(BF16), xf, norm_mix_post[i], norm_ffn_pre[i], tm=512, tk=d)
        act = ffn_up(h, w_ffn_up, i, ffn_conv_w[i], ffn_conv_b[i], seq)
        xf, h = matmul_norm_res(act, w_ffn_down[i].astype(BF16), xf, norm_ffn_post[i], norm_ple_pre[i],
                                tm=512, tk=w_ffn_down.shape[1] // 4)
        g_next = norm_mix_pre[i + 1] if i + 1 < depth else norm_mix_pre[i]
        xf, h = ple_norm_res(p[i].reshape(t, -1).astype(BF16), h, w_ple[i].astype(BF16),
                             w_ple_gate[i].astype(BF16), xf, norm_ple_post[i], g_next)
    return xf.reshape(batch, seq, d)
```

```python
import functools

import jax
import jax.numpy as jnp
from jax import lax
from jax.experimental import pallas as pl
from jax.experimental.pallas import tpu as pltpu

F32 = jnp.float32
BF16 = jnp.bfloat16

LANES = 128
SUBLANES = 8
VMEM_LIMIT_BYTES = 52 * 1024 * 1024

HEAD_DIM = 128
BLOCK = 128
ROPE_THETA = 10000.0
RMS_EPS = 1e-6
NEG_INF = -1e30

LRU_CONV = 4
LRU_C = 8.0
DIL_CONFIGS = ((128, 1), (512, 4), (2048, 16))
DIL_HEADS = 4
RWKV_HEAD = 64
RWKV_W_LORA = 64
RWKV_A_LORA = 64
RWKV_GN_EPS = 64e-5
RWKV_CHUNK = 64
RWKV_CHUNKS_PER_STEP = 2
FFN_CONV = 3
SB_KEYS = 256
EPILOGUE_ROWS = 256


def _cparams(*sem):
    return pltpu.CompilerParams(dimension_semantics=sem, vmem_limit_bytes=VMEM_LIMIT_BYTES)


def _dot(a, b):
    return jnp.dot(a, b, preferred_element_type=F32)


def _dot_nt(a, b):
    return lax.dot_general(a, b, (((1,), (1,)), ((), ())), preferred_element_type=F32)


def _dot_tn(a, b):
    return lax.dot_general(a, b, (((0,), (0,)), ((), ())), preferred_element_type=F32)


def _split_bf16(x):
    hi = x.astype(BF16)
    lo = (x - hi.astype(F32)).astype(BF16)
    return hi, lo


def _dot_exact_rhs(x, m_bf16):
    hi, lo = _split_bf16(x)
    return _dot(hi, m_bf16) + _dot(lo, m_bf16)


def _rms(x, g):
    return x * lax.rsqrt(jnp.mean(x * x, axis=-1, keepdims=True) + RMS_EPS) * g


def _gelu(x):
    return jax.nn.gelu(x, approximate=True)


def _softplus(x):
    return jnp.maximum(x, 0.0) + jnp.log1p(jnp.exp(-jnp.abs(x)))


def _row_tiles(rows, sub=EPILOGUE_ROWS):
    sub = min(sub, rows)
    return [slice(s, s + sub) for s in range(0, rows, sub)]


def _rmsnorm_kernel(x_ref, g_ref, o_ref):
    o_ref[...] = _rms(x_ref[...], g_ref[...]).astype(o_ref.dtype)


def rmsnorm_bf16(x, g, tm=512):
    t, d = x.shape
    return pl.pallas_call(
        _rmsnorm_kernel,
        out_shape=jax.ShapeDtypeStruct((t, d), BF16),
        grid=(t // tm,),
        in_specs=[pl.BlockSpec((tm, d), lambda i: (i, 0)), pl.BlockSpec((1, d), lambda i: (0, 0))],
        out_specs=pl.BlockSpec((tm, d), lambda i: (i, 0)),
        compiler_params=_cparams("parallel"),
        name="rmsnorm",
    )(x, g.reshape(1, d))


def _mm_kernel(a_ref, w_ref, o_ref):
    o_ref[...] = _dot(a_ref[...], w_ref[...]).astype(o_ref.dtype)


def _mm_rope_kernel(a_ref, w_ref, cos_ref, sin_ref, o_ref, *, n_rope_blocks, tn):
    j = pl.program_id(1)
    row_tiles = _row_tiles(a_ref.shape[0])

    @pl.when(j < n_rope_blocks)
    def _():
        for rows in row_tiles:
            acc = _dot(a_ref[rows, :], w_ref[...])
            cos = cos_ref[rows, :]
            sin = sin_ref[rows, :]
            for c in range(tn // HEAD_DIM):
                seg = acc[:, c * HEAD_DIM:(c + 1) * HEAD_DIM]
                rot = pltpu.roll(seg, HEAD_DIM // 2, axis=1)
                o_ref[rows, c * HEAD_DIM:(c + 1) * HEAD_DIM] = (seg * cos + rot * sin).astype(o_ref.dtype)

    @pl.when(j >= n_rope_blocks)
    def _():
        o_ref[...] = _dot(a_ref[...], w_ref[...]).astype(o_ref.dtype)


def matmul(a, w, out_dtype, tm=1024, tn=1024):
    m, k = a.shape
    n = w.shape[1]
    tm, tn = min(tm, m), min(tn, n)
    return pl.pallas_call(
        _mm_kernel,
        out_shape=jax.ShapeDtypeStruct((m, n), out_dtype),
        grid=(m // tm, n // tn),
        in_specs=[pl.BlockSpec((tm, k), lambda i, j: (i, 0)), pl.BlockSpec((k, tn), lambda i, j: (0, j))],
        out_specs=pl.BlockSpec((tm, tn), lambda i, j: (i, j)),
        compiler_params=_cparams("parallel", "arbitrary"),
        name="matmul_plain",
    )(a, w)


def matmul_rope(a, w, cos, sin, seq, n_rope_cols, out_dtype, tm=1024, tn=1536):
    m, k = a.shape
    n = w.shape[1]
    tm, tn = min(tm, m), min(tn, n)
    sblocks = seq // tm
    return pl.pallas_call(
        functools.partial(_mm_rope_kernel, n_rope_blocks=n_rope_cols // tn, tn=tn),
        out_shape=jax.ShapeDtypeStruct((m, n), out_dtype),
        grid=(m // tm, n // tn),
        in_specs=[
            pl.BlockSpec((tm, k), lambda i, j: (i, 0)),
            pl.BlockSpec((k, tn), lambda i, j: (0, j)),
            pl.BlockSpec((tm, HEAD_DIM), lambda i, j: (i % sblocks, 0)),
            pl.BlockSpec((tm, HEAD_DIM), lambda i, j: (i % sblocks, 0)),
        ],
        out_specs=pl.BlockSpec((tm, tn), lambda i, j: (i, j)),
        compiler_params=_cparams("parallel", "arbitrary"),
        name="matmul_rope",
    )(a, w, cos, sin)


def _mm_norm_res_kernel(a_ref, w_ref, x_ref, gpost_ref, gnext_ref, xo_ref, ho_ref, *acc, nk):
    row_tiles = _row_tiles(a_ref.shape[0])

    def finish(rows, val):
        xn = x_ref[rows, :] + _rms(val, gpost_ref[...])
        xo_ref[rows, :] = xn
        ho_ref[rows, :] = _rms(xn, gnext_ref[...]).astype(ho_ref.dtype)

    if nk == 1:
        for rows in row_tiles:
            finish(rows, _dot(a_ref[rows, :], w_ref[...]))
        return
    acc_ref, = acc
    kk = pl.program_id(1)

    @pl.when(kk == 0)
    def _():
        acc_ref[...] = _dot(a_ref[...], w_ref[...])

    @pl.when((kk > 0) & (kk < nk - 1))
    def _():
        acc_ref[...] += _dot(a_ref[...], w_ref[...])

    @pl.when(kk == nk - 1)
    def _():
        for rows in row_tiles:
            finish(rows, acc_ref[rows, :] + _dot(a_ref[rows, :], w_ref[...]))


def matmul_norm_res(a, w, x, g_post, g_next, tm=256, tk=512):
    m, k = a.shape
    d = w.shape[1]
    tm = min(tm, m)
    nk = k // tk
    return pl.pallas_call(
        functools.partial(_mm_norm_res_kernel, nk=nk),
        out_shape=(jax.ShapeDtypeStruct((m, d), F32), jax.ShapeDtypeStruct((m, d), BF16)),
        grid=(m // tm, nk),
        in_specs=[
            pl.BlockSpec((tm, tk), lambda i, kk: (i, kk)),
            pl.BlockSpec((tk, d), lambda i, kk: (kk, 0)),
            pl.BlockSpec((tm, d), lambda i, kk: (i, 0)),
            pl.BlockSpec((1, d), lambda i, kk: (0, 0)),
            pl.BlockSpec((1, d), lambda i, kk: (0, 0)),
        ],
        out_specs=(pl.BlockSpec((tm, d), lambda i, kk: (i, 0)), pl.BlockSpec((tm, d), lambda i, kk: (i, 0))),
        scratch_shapes=[pltpu.VMEM((tm, d), F32)] if nk > 1 else [],
        compiler_params=_cparams("parallel", "arbitrary"),
        name="matmul_norm_res",
    )(a, w, x, g_post.reshape(1, d), g_next.reshape(1, d))


def _ple_kernel(p_ref, h_ref, wp_ref, wg_ref, x_ref, gpost_ref, gnext_ref, xo_ref, ho_ref):
    for rows in _row_tiles(x_ref.shape[0]):
        val = _dot(p_ref[rows, :], wp_ref[...]) * jax.nn.sigmoid(_dot(h_ref[rows, :], wg_ref[...]))
        xn = x_ref[rows, :] + _rms(val, gpost_ref[...])
        xo_ref[rows, :] = xn
        ho_ref[rows, :] = _rms(xn, gnext_ref[...]).astype(ho_ref.dtype)


def ple_norm_res(p, h, w_ple, w_gate, x, g_post, g_next, tm=512):
    m, d = x.shape
    tm = min(tm, m)
    pd = p.shape[1]
    row = lambda i: (i, 0)
    fix = lambda i: (0, 0)
    return pl.pallas_call(
        _ple_kernel,
        out_shape=(jax.ShapeDtypeStruct((m, d), F32), jax.ShapeDtypeStruct((m, d), BF16)),
        grid=(m // tm,),
        in_specs=[
            pl.BlockSpec((tm, pd), row), pl.BlockSpec((tm, d), row),
            pl.BlockSpec((pd, d), fix), pl.BlockSpec((d, d), fix),
            pl.BlockSpec((tm, d), row), pl.BlockSpec((1, d), fix), pl.BlockSpec((1, d), fix),
        ],
        out_specs=(pl.BlockSpec((tm, d), row), pl.BlockSpec((tm, d), row)),
        compiler_params=_cparams("parallel"),
        name="ple_norm_res",
    )(p, h, w_ple, w_gate, x, g_post.reshape(1, d), g_next.reshape(1, d))


def _merge_kernel(h_ref, ya_ref, yb_ref, yc_ref, yd_ref, ga_ref, gb_ref, gc_ref, gd_ref,
                  wa_ref, wb_ref, wc_ref, wd_ref, o_ref):
    h = h_ref[...]
    acc = jax.nn.sigmoid(_dot(h, ga_ref[...])) * _dot(ya_ref[...], wa_ref[...])
    acc += jax.nn.sigmoid(_dot(h, gb_ref[...])) * _dot(yb_ref[...], wb_ref[...])
    acc += jax.nn.sigmoid(_dot(h, gc_ref[...])) * _dot(yc_ref[...], wc_ref[...])
    acc += jax.nn.sigmoid(_dot(h, gd_ref[...])) * _dot(yd_ref[...], wd_ref[...])
    o_ref[...] = acc.astype(o_ref.dtype)


def merge_branches(h, ys, w_gate, ws, tm=1024, tn=256):
    m, dm = h.shape
    tm = min(tm, m)
    d = ws[0].shape[1]
    nb = d // tn
    in_specs = [pl.BlockSpec((tm, dm), lambda i, j: (i, 0))]
    in_specs += [pl.BlockSpec((tm, y.shape[1]), lambda i, j: (i, 0)) for y in ys]
    in_specs += [pl.BlockSpec((dm, tn), functools.partial(lambda i, j, b: (0, b * nb + j), b=b)) for b in range(4)]
    in_specs += [pl.BlockSpec((w.shape[0], tn), lambda i, j: (0, j)) for w in ws]
    return pl.pallas_call(
        _merge_kernel,
        out_shape=jax.ShapeDtypeStruct((m, d), BF16),
        grid=(m // tm, nb),
        in_specs=in_specs,
        out_specs=pl.BlockSpec((tm, tn), lambda i, j: (i, j)),
        compiler_params=_cparams("parallel", "arbitrary"),
        name="merge_branches",
    )(h, *ys, w_gate, w_gate, w_gate, w_gate, *ws)


def _ffn_up_kernel(h_ref, wg32_ref, wu32_ref, cwg_ref, cwu_ref, cbg_ref, cbu_ref, o_ref, bufg_ref, bufu_ref,
                   wg_ref, wu_ref, *, tm, seq_blocks):
    i = pl.program_id(1)

    @pl.when(i == 0)
    def _():
        wg_ref[...] = wg32_ref[...].astype(BF16)
        wu_ref[...] = wu32_ref[...].astype(BF16)

    @pl.when(i % seq_blocks == 0)
    def _():
        bufg_ref[0:SUBLANES, :] = jnp.zeros((SUBLANES, bufg_ref.shape[1]), F32)
        bufu_ref[0:SUBLANES, :] = jnp.zeros((SUBLANES, bufu_ref.shape[1]), F32)

    h = h_ref[...]

    def conv(w_ref, cw_ref, cb_ref, buf_ref):
        buf_ref[SUBLANES:SUBLANES + tm, :] = _dot(h, w_ref[...])
        cw = cw_ref[...]
        out = cb_ref[...] + cw[2:3, :] * buf_ref[SUBLANES:SUBLANES + tm, :]
        out += cw[1:2, :] * buf_ref[SUBLANES - 1:SUBLANES - 1 + tm, :]
        out += cw[0:1, :] * buf_ref[SUBLANES - 2:SUBLANES - 2 + tm, :]
        buf_ref[0:SUBLANES, :] = buf_ref[tm:tm + SUBLANES, :]
        return out

    g = conv(wg_ref, cwg_ref, cbg_ref, bufg_ref)
    u = conv(wu_ref, cwu_ref, cbu_ref, bufu_ref)
    o_ref[...] = (_gelu(g) * u).astype(o_ref.dtype)


def ffn_up(h, w_up_all, layer, conv_w, conv_b, seq, tm=1024, tn=512):
    m, d = h.shape
    tm = min(tm, seq)
    dff = w_up_all.shape[2] // 2
    nb = dff // tn
    cb = conv_b.reshape(1, 2 * dff)
    return pl.pallas_call(
        functools.partial(_ffn_up_kernel, tm=tm, seq_blocks=seq // tm),
        out_shape=jax.ShapeDtypeStruct((m, dff), BF16),
        grid=(nb, m // tm),
        in_specs=[
            pl.BlockSpec((tm, d), lambda j, i: (i, 0)),
            pl.BlockSpec((None, d, tn), lambda j, i: (layer, 0, j)),
            pl.BlockSpec((None, d, tn), lambda j, i: (layer, 0, j + nb)),
            pl.BlockSpec((FFN_CONV, tn), lambda j, i: (0, j)),
            pl.BlockSpec((FFN_CONV, tn), lambda j, i: (0, j + nb)),
            pl.BlockSpec((1, tn), lambda j, i: (0, j)),
            pl.BlockSpec((1, tn), lambda j, i: (0, j + nb)),
        ],
        out_specs=pl.BlockSpec((tm, tn), lambda j, i: (i, j)),
        scratch_shapes=[pltpu.VMEM((tm + SUBLANES, tn), F32), pltpu.VMEM((tm + SUBLANES, tn), F32),
                        pltpu.VMEM((d, tn), BF16), pltpu.VMEM((d, tn), BF16)],
        compiler_params=_cparams("parallel", "arbitrary"),
        name="ffn_up_conv_glu",
    )(h, w_up_all, w_up_all, conv_w, conv_w, cb, cb)


def _lru_kernel(x_ref, gate_ref, cw_ref, cb_ref, wr_ref, br_ref, wi_ref, bi_ref, lam_ref, o_ref,
                xbuf_ref, a0_ref, h0_ref, a1_ref, h1_ref, *, seq, pad):
    xbuf_ref[0:SUBLANES, :] = jnp.zeros((SUBLANES, LANES), F32)
    xbuf_ref[SUBLANES:SUBLANES + seq, :] = x_ref[0]
    cw = cw_ref[...]
    u = cb_ref[...] + cw[3:4, :] * xbuf_ref[SUBLANES:SUBLANES + seq, :]
    for k in range(LRU_CONV - 1):
        off = SUBLANES - (LRU_CONV - 1) + k
        u += cw[k:k + 1, :] * xbuf_ref[off:off + seq, :]
    ub = u.astype(BF16)
    r = jax.nn.sigmoid(_dot(ub, wr_ref[0]) + br_ref[...])
    ig = jax.nn.sigmoid(_dot(ub, wi_ref[0]) + bi_ref[...])
    log_a = -LRU_C * r * _softplus(-lam_ref[...])
    a = jnp.exp(log_a)
    inp = jnp.sqrt(1.0 - a * a) * ig * u

    ones = jnp.ones((pad, LANES), F32)
    zeros = jnp.zeros((pad, LANES), F32)
    a0_ref[0:pad, :] = ones
    a1_ref[0:pad, :] = ones
    h0_ref[0:pad, :] = zeros
    h1_ref[0:pad, :] = zeros
    a0_ref[pad:pad + seq, :] = a
    h0_ref[pad:pad + seq, :] = inp
    bufs = ((a0_ref, h0_ref), (a1_ref, h1_ref))
    d = 1
    level = 0
    while d < seq:
        (a_src, h_src), (a_dst, h_dst) = bufs[level % 2], bufs[(level + 1) % 2]
        a_cur = a_src[pad:pad + seq, :]
        h_dst[pad:pad + seq, :] = h_src[pad:pad + seq, :] + a_cur * h_src[pad - d:pad - d + seq, :]
        a_dst[pad:pad + seq, :] = a_cur * a_src[pad - d:pad - d + seq, :]
        d *= 2
        level += 1
    h = bufs[level % 2][1][pad:pad + seq, :]
    o_ref[0] = (h * _gelu(gate_ref[0])).astype(o_ref.dtype)


def rglru(xg, conv_w, conv_b, w_r, b_r, w_i, b_i, lam):
    b, s, w2 = xg.shape
    w = w2 // 2
    nblk = w // LANES
    pad = s
    vec = lambda v: v.reshape(1, w)
    vspec = pl.BlockSpec((1, LANES), lambda bi, c: (0, c))
    return pl.pallas_call(
        functools.partial(_lru_kernel, seq=s, pad=pad),
        out_shape=jax.ShapeDtypeStruct((b, s, w), BF16),
        grid=(b, nblk),
        in_specs=[
            pl.BlockSpec((1, s, LANES), lambda bi, c: (bi, 0, c)),
            pl.BlockSpec((1, s, LANES), lambda bi, c: (bi, 0, c + nblk)),
            pl.BlockSpec((LRU_CONV, LANES), lambda bi, c: (0, c)),
            vspec,
            pl.BlockSpec((1, LANES, LANES), lambda bi, c: (c, 0, 0)),
            vspec,
            pl.BlockSpec((1, LANES, LANES), lambda bi, c: (c, 0, 0)),
            vspec, vspec,
        ],
        out_specs=pl.BlockSpec((1, s, LANES), lambda bi, c: (bi, 0, c)),
        scratch_shapes=[pltpu.VMEM((s + SUBLANES, LANES), F32)] + [pltpu.VMEM((pad + s, LANES), F32)] * 4,
        compiler_params=_cparams("parallel", "parallel"),
        name="rglru",
    )(xg, xg, conv_w, vec(conv_b), w_r, vec(b_r), w_i, vec(b_i), vec(lam))


DIL_UNITS_PER_PHASE = 4


def _dil_kernel(*refs, seq):
    ngroups = len(DIL_CONFIGS)
    qkv_refs = refs[:3 * ngroups]
    o_ref = refs[3 * ngroups]
    qf_ref, kf_ref, vf_ref, acc_ref, m_ref, l_ref = refs[3 * ngroups + 1:]
    scale = HEAD_DIM ** -0.5
    row = lax.broadcasted_iota(jnp.int32, (BLOCK, 2 * BLOCK), 0)
    col = lax.broadcasted_iota(jnp.int32, (BLOCK, 2 * BLOCK), 1)
    bias_two = jnp.where((col >= row) & (col <= row + BLOCK), 0.0, NEG_INF)
    bias_own = jnp.where(lax.broadcasted_iota(jnp.int32, (BLOCK, BLOCK), 1)
                         <= lax.broadcasted_iota(jnp.int32, (BLOCK, BLOCK), 0), 0.0, NEG_INF)

    def run_units(g, units):
        s = [_dot_nt(q, k) * scale + (bias_own if k.shape[0] == BLOCK else bias_two) for q, k, _, _ in units]
        mx = [jnp.max(x, axis=-1, keepdims=True) for x in s]
        e = [jnp.exp(x - m).astype(BF16) for x, m in zip(s, mx)]
        pv = [_dot(p, jnp.concatenate([v, jnp.ones_like(v)], axis=1)) for p, (_, _, v, _) in zip(e, units)]
        for y, m, (_, _, _, rows) in zip(pv, mx, units):
            acc_ref[g, rows, :] = y[:, :HEAD_DIM]
            l_ref[g, rows, :] = y[:, HEAD_DIM:]
            m_ref[g, rows, :] = jnp.broadcast_to(m, (BLOCK, HEAD_DIM))

    for g, (_, d) in enumerate(DIL_CONFIGS):
        q_ref, k_ref, v_ref = qkv_refs[3 * g:3 * g + 3]
        ln = seq // d
        nblk = ln // BLOCK
        if d > 1:
            qf_ref[...] = q_ref[...].astype(F32)
            kf_ref[...] = k_ref[...].astype(F32)
            vf_ref[...] = v_ref[...].astype(F32)
        units = []
        for r in range(d):
            if d > 1:
                qr = qf_ref[pl.ds(r, ln, stride=d), :].astype(BF16)
                kr = kf_ref[pl.ds(r, ln, stride=d), :].astype(BF16)
                vr = vf_ref[pl.ds(r, ln, stride=d), :].astype(BF16)
            for nb in range(nblk):
                lo = max(nb - 1, 0) * BLOCK
                hi = (nb + 1) * BLOCK
                if d > 1:
                    unit = (qr[nb * BLOCK:hi], kr[lo:hi], vr[lo:hi], pl.ds(nb * BLOCK * d + r, BLOCK, stride=d))
                else:
                    unit = (q_ref[nb * BLOCK:hi, :], k_ref[lo:hi, :], v_ref[lo:hi, :], pl.ds(nb * BLOCK, BLOCK))
                units.append(unit)
                if len(units) == DIL_UNITS_PER_PHASE:
                    run_units(g, units)
                    units = []
        if units:
            run_units(g, units)

    rows_per_step = 2 * BLOCK
    for c in range(seq // rows_per_step):
        rs = slice(c * rows_per_step, (c + 1) * rows_per_step)
        ms = [m_ref[g, rs, :] for g in range(ngroups)]
        top = functools.reduce(jnp.maximum, ms)
        ws = [jnp.exp(m - top) for m in ms]
        num = functools.reduce(jnp.add, [w * acc_ref[g, rs, :] for g, w in enumerate(ws)])
        den = functools.reduce(jnp.add, [w * l_ref[g, rs, :] for g, w in enumerate(ws)])
        o_ref[rs, :] = (num / den).astype(o_ref.dtype)


def dilated_attention(qkv, batch, seq):
    assert all(window // d == BLOCK and seq % (BLOCK * d) == 0 for window, d in DIL_CONFIGS)
    t, width = qkv.shape
    ngroups = len(DIL_CONFIGS)
    nheads = width // (3 * HEAD_DIM)

    def spec(which, g):
        return pl.BlockSpec((seq, HEAD_DIM), lambda b, h: (b, which * nheads + g * DIL_HEADS + h))

    in_specs = [spec(which, g) for g in range(ngroups) for which in range(3)]
    return pl.pallas_call(
        functools.partial(_dil_kernel, seq=seq),
        out_shape=jax.ShapeDtypeStruct((t, DIL_HEADS * HEAD_DIM), BF16),
        grid=(batch, DIL_HEADS),
        in_specs=in_specs,
        out_specs=pl.BlockSpec((seq, HEAD_DIM), lambda b, h: (b, h)),
        scratch_shapes=[pltpu.VMEM((seq, HEAD_DIM), F32)] * 3 + [pltpu.VMEM((ngroups, seq, HEAD_DIM), F32)] * 3,
        compiler_params=_cparams("parallel", "parallel"),
        name="dilated_attention",
    )(*([qkv] * (3 * ngroups)))


def _sb_kernel(q_ref, k_ref, v_ref, o_ref, acc_ref, run_ref):
    n = pl.program_id(1)
    nh = q_ref.shape[1] // HEAD_DIM
    heads = range(nh)
    sls = [slice(h * HEAD_DIM, (h + 1) * HEAD_DIM) for h in heads]
    scale = HEAD_DIM ** -0.5
    kb = SB_KEYS
    assert kb == 2 * BLOCK
    from_s = (lax.broadcasted_iota(jnp.int32, (kb, kb), 0) >= lax.broadcasted_iota(jnp.int32, (kb, kb), 1)).astype(BF16)
    off = (n % 2) * BLOCK
    before = (lax.broadcasted_iota(jnp.int32, (BLOCK, kb), 1) - lax.broadcasted_iota(jnp.int32, (BLOCK, kb), 0)) < off
    q = [q_ref[:, sl] for sl in sls]

    def block(jb, diag):
        start = pl.multiple_of(jb * kb, kb)
        z = [_dot_nt(q[h], k_ref[pl.ds(start, kb), sls[h]]) * scale for h in heads]
        log_beta = [jnp.minimum(zz, 0.0) - jnp.log(1.0 + jnp.exp(-jnp.abs(zz))) for zz in z]
        log_1m = [lb - zz for lb, zz in zip(log_beta, z)]
        if diag:
            log_1m = [jnp.where(before, x, 0.0) for x in log_1m]
        incl = [_dot_exact_rhs(x, from_s) for x in log_1m]
        for h in heads:
            att = jnp.exp(z[h] + incl[h]) if diag else jnp.exp(z[h] + incl[h] + run_ref[h])
            if diag:
                att = jnp.where(before, att, 0.0)
            pv = _dot(att.astype(BF16), v_ref[pl.ds(start, kb), sls[h]])
            total = jnp.broadcast_to(incl[h][:, 0:1], (BLOCK, kb))
            if diag:
                acc_ref[:, sls[h]] = pv
                run_ref[h] = total
            else:
                acc_ref[:, sls[h]] += pv
                run_ref[h] = run_ref[h] + total

    full = n // 2
    block(full, True)

    def body(t, carry):
        block(full - 1 - t, False)
        return carry

    lax.fori_loop(0, full, body, 0)
    o_ref[...] = acc_ref[...].astype(o_ref.dtype)


def stick_breaking(qkv, batch, seq):
    t, width = qkv.shape
    w = width // 3
    nq = seq // BLOCK
    assert seq % SB_KEYS == 0
    return pl.pallas_call(
        _sb_kernel,
        out_shape=jax.ShapeDtypeStruct((t, w), BF16),
        grid=(batch, nq),
        in_specs=[
            pl.BlockSpec((BLOCK, w), lambda b, n: (b * nq + n, 0)),
            pl.BlockSpec((seq, w), lambda b, n: (b, 1)),
            pl.BlockSpec((seq, w), lambda b, n: (b, 2)),
        ],
        out_specs=pl.BlockSpec((BLOCK, w), lambda b, n: (b * nq + n, 0)),
        scratch_shapes=[pltpu.VMEM((BLOCK, w), F32), pltpu.VMEM((w // HEAD_DIM, BLOCK, SB_KEYS), F32)],
        compiler_params=_cparams("parallel", "arbitrary"),
        name="stick_breaking",
    )(qkv, qkv, qkv)


def _head_sum(x, bd):
    cols = []
    for c in range(x.shape[1] // LANES):
        cols.append(_dot_exact_rhs(x[:, c * LANES:(c + 1) * LANES], bd))
    return jnp.concatenate(cols, axis=1)


def _rwkv_prep_kernel(*refs, tm, width, seq_blocks, has_vres):
    if has_vres:
        (h_ref, wd_ref, mu_ref, w0_ref, wup_ref, a0_ref, aup_ref, gup_ref, kk_ref, ka_ref, rk_ref,
         vf_ref, v0_ref, vdn_ref, vup_ref,
         r_o, k_o, v_o, lw_o, kk_o, b_o, g_o, bon_o, buf_ref) = refs
    else:
        (h_ref, wd_ref, mu_ref, w0_ref, wup_ref, a0_ref, aup_ref, gup_ref, kk_ref, ka_ref, rk_ref,
         r_o, k_o, v_o, lw_o, kk_o, b_o, g_o, bon_o, buf_ref) = refs
    i = pl.program_id(0)

    @pl.when(i % seq_blocks == 0)
    def _():
        buf_ref[0:SUBLANES, :] = jnp.zeros((SUBLANES, buf_ref.shape[1]), F32)

    row = lax.broadcasted_iota(jnp.int32, (LANES, LANES), 0) // RWKV_HEAD
    col = lax.broadcasted_iota(jnp.int32, (LANES, LANES), 1) // RWKV_HEAD
    bd = (row == col).astype(BF16)
    w = width
    for rows in _row_tiles(tm):
        lo, n = SUBLANES + rows.start, rows.stop - rows.start
        seg = _dot(h_ref[rows, :], wd_ref[...])
        buf_ref[lo:lo + n, :] = seg
        shifted = buf_ref[lo - 1:lo - 1 + n, :]
        xs = seg + (shifted - seg) * mu_ref[...]
        r = xs[:, 0:w]
        k = xs[:, w:2 * w]
        v = xs[:, 2 * w:3 * w]
        low = xs[:, 3 * w:3 * w + LANES]
        g_low = xs[:, 3 * w + LANES:3 * w + 3 * LANES]
        wpre = w0_ref[...] + _dot(jnp.tanh(low).astype(BF16), wup_ref[...])
        wlog = -_softplus(-wpre) - 0.5
        lw_o[rows, :] = -jnp.exp(wlog)
        a = jax.nn.sigmoid(a0_ref[...] + _dot(low.astype(BF16), aup_ref[...]))
        g_o[rows, :] = _dot(jax.nn.sigmoid(g_low).astype(BF16), gup_ref[...]).astype(g_o.dtype)
        if has_vres:
            mix = jax.nn.sigmoid(v0_ref[...]
                                 + _dot(_dot(v.astype(BF16), vdn_ref[...]).astype(BF16), vup_ref[...]))
            v = v + (vf_ref[rows, :] - v) * mix
        kk = k * kk_ref[...]
        norm = jnp.sqrt(_head_sum(kk * kk, bd))
        kk = kk / jnp.maximum(norm, 1e-12)
        k2 = k * (1.0 + (a - 1.0) * ka_ref[...])
        bonus = _head_sum(r * k2 * rk_ref[...], bd) * v
        r_o[rows, :] = r.astype(r_o.dtype)
        k_o[rows, :] = k2.astype(k_o.dtype)
        v_o[rows, :] = v
        kk_o[rows, :] = kk.astype(kk_o.dtype)
        b_o[rows, :] = (kk * a).astype(b_o.dtype)
        bon_o[rows, :] = bonus.astype(bon_o.dtype)
    buf_ref[0:SUBLANES, :] = buf_ref[tm:tm + SUBLANES, :]


def rwkv_prep(h, w_d, seq, mu, w0, w_up, a0, a_up, g_up, k_k, k_a, r_k, v_first, v_res, tm=256):
    t, d = h.shape
    wpad = w_d.shape[1]
    w = w0.shape[0]
    has_vres = v_res is not None
    row = lambda i: (i, 0)
    fix = lambda i: (0, 0)
    vec = lambda x: x.reshape(1, -1)
    in_specs = [
        pl.BlockSpec((tm, d), row),
        pl.BlockSpec((d, wpad), fix),
        pl.BlockSpec((1, wpad), fix), pl.BlockSpec((1, w), fix), pl.BlockSpec((LANES, w), fix),
        pl.BlockSpec((1, w), fix), pl.BlockSpec((LANES, w), fix), pl.BlockSpec((2 * LANES, w), fix),
        pl.BlockSpec((1, w), fix), pl.BlockSpec((1, w), fix), pl.BlockSpec((1, w), fix),
    ]
    args = [h, w_d, vec(mu), vec(w0), w_up, vec(a0), a_up, g_up, vec(k_k), vec(k_a), vec(r_k)]
    if has_vres:
        v0, v_down, v_up = v_res
        in_specs += [pl.BlockSpec((tm, w), row), pl.BlockSpec((1, w), fix),
                     pl.BlockSpec((w, LANES), fix), pl.BlockSpec((LANES, w), fix)]
        args += [v_first, vec(v0), v_down, v_up]
    outs = tuple(jax.ShapeDtypeStruct((t, w), dt) for dt in (BF16, BF16, F32, F32, BF16, BF16, BF16, BF16))
    return pl.pallas_call(
        functools.partial(_rwkv_prep_kernel, tm=tm, width=w, seq_blocks=seq // tm, has_vres=has_vres),
        out_shape=outs,
        grid=(t // tm,),
        in_specs=in_specs,
        out_specs=(pl.BlockSpec((tm, w), row),) * 8,
        scratch_shapes=[pltpu.VMEM((tm + SUBLANES, wpad), F32)],
        compiler_params=_cparams("arbitrary"),
        name="rwkv_prep",
    )(*args)


def _rwkv_chunk_kernel(r_ref, k_ref, v_ref, lw_ref, kk_ref, b_ref, g_ref, bon_ref, gnw_ref, gnb_ref, o_ref,
                       state_ref):
    c = pl.program_id(1)
    n = RWKV_CHUNK
    n2 = 2 * n

    @pl.when(c == 0)
    def _():
        state_ref[...] = jnp.zeros_like(state_ref)

    tri = (lax.broadcasted_iota(jnp.int32, (n, n), 1) <= lax.broadcasted_iota(jnp.int32, (n, n), 0)).astype(BF16)
    row2 = lax.broadcasted_iota(jnp.int32, (n2, n2), 0)
    col2 = lax.broadcasted_iota(jnp.int32, (n2, n2), 1)
    t2 = row2 & (n - 1)
    s2 = col2 & (n - 1)
    strict = s2 < t2
    incl = s2 <= t2
    eye = (row2 == col2).astype(F32)
    head0 = lax.broadcasted_iota(jnp.int32, (1, LANES), 1) < RWKV_HEAD
    own = jnp.concatenate([jnp.broadcast_to(head0, (n, LANES)), jnp.broadcast_to(~head0, (n, LANES))], axis=0)

    def stack(x):
        return jnp.where(own, jnp.concatenate([x, x], axis=0), 0.0)

    npairs = r_ref.shape[1] // LANES
    nchunks = r_ref.shape[0] // n
    sls = [slice(hp * LANES, (hp + 1) * LANES) for hp in range(npairs)]
    units = [(ci, hp) for ci in range(nchunks) for hp in range(npairs)]
    pre = []
    for ci in range(nchunks):
        rows = slice(ci * n, (ci + 1) * n)
        lw = lw_ref[rows, :]
        gsum = _dot_exact_rhs_left(tri, lw)
        p_inv = jnp.exp(-gsum)
        pre.append(dict(p_end=jnp.exp(gsum[n - 1:n, :]), rt=r_ref[rows, :] * jnp.exp(gsum), kt=k_ref[rows, :] * p_inv,
                        bt=b_ref[rows, :] * p_inv, kap=kk_ref[rows, :] * jnp.exp(gsum - lw), v=v_ref[rows, :]))
    rt = {u: stack(pre[u[0]]["rt"][:, sls[u[1]]]).astype(BF16) for u in units}
    kt = {u: stack(pre[u[0]]["kt"][:, sls[u[1]]]) for u in units}
    bt = {u: stack(pre[u[0]]["bt"][:, sls[u[1]]]) for u in units}
    kap = {u: stack(pre[u[0]]["kap"][:, sls[u[1]]]).astype(BF16) for u in units}
    v2 = {u: stack(pre[u[0]]["v"][:, sls[u[1]]]).astype(BF16) for u in units}
    prod = {u: _dot_nt(jnp.concatenate([kap[u], rt[u]], axis=0),
                       jnp.concatenate([bt[u].astype(BF16), kt[u].astype(BF16)], axis=0)) for u in units}
    a_ab = {u: jnp.where(strict, prod[u][:n2, :n2], 0.0) for u in units}
    a_ak = {u: jnp.where(strict, prod[u][:n2, n2:], 0.0).astype(BF16) for u in units}
    a_r = {u: jnp.concatenate([jnp.where(incl, prod[u][n2:, n2:], 0.0).astype(BF16),
                               jnp.where(incl, -prod[u][n2:, :n2], 0.0).astype(BF16)], axis=1) for u in units}
    x = {u: eye - a_ab[u] for u in units}
    q = {u: _dot(a_ab[u].astype(BF16), a_ab[u].astype(BF16)) for u in units}
    steps = 1
    while True:
        x = {u: x[u] + _dot(x[u].astype(BF16), q[u].astype(BF16)) for u in units}
        steps *= 2
        if steps * 2 >= n:
            break
        q = {u: _dot(q[u].astype(BF16), q[u].astype(BF16)) for u in units}
    inv_n = 1.0 / RWKV_HEAD
    state = [state_ref[hp] for hp in range(npairs)]
    for ci in range(nchunks):
        rows = slice(ci * n, (ci + 1) * n)
        us = [(ci, hp) for hp in range(npairs)]
        s0b = [s.astype(BF16) for s in state]
        rhs = [_dot_nt(kap[u], s0b[u[1]]) + _dot(a_ak[u], v2[u]) for u in us]
        u2 = [_dot(x[u].astype(BF16), r_.astype(BF16)).astype(BF16) for u, r_ in zip(us, rhs)]
        vu = [jnp.concatenate([v2[u], w_], axis=0) for u, w_ in zip(us, u2)]
        y2 = [_dot_nt(rt[u], s0b[u[1]]) + _dot(a_r[u], vu_) for u, vu_ in zip(us, vu)]
        outs = []
        for hp, u in enumerate(us):
            pe = pre[ci]["p_end"][:, sls[hp]]
            kb_end = jnp.concatenate([(kt[u] * pe).astype(BF16), (-(bt[u] * pe)).astype(BF16)], axis=0)
            state[hp] = state[hp] * pe + _dot_tn(vu[hp], kb_end)
            mean = jnp.sum(y2[hp], axis=-1, keepdims=True) * inv_n
            cen = jnp.where(own, y2[hp] - mean, 0.0)
            var = jnp.sum(cen * cen, axis=-1, keepdims=True) * inv_n
            yn2 = cen * lax.rsqrt(var + RWKV_GN_EPS)
            outs.append(yn2[:n] + yn2[n:])
        yn = jnp.concatenate(outs, axis=1) * gnw_ref[...] + gnb_ref[...]
        o_ref[rows, :] = ((yn + bon_ref[rows, :]) * g_ref[rows, :]).astype(o_ref.dtype)
    for hp in range(npairs):
        state_ref[hp] = state[hp]


def _dot_exact_rhs_left(m_bf16, x):
    hi, lo = _split_bf16(x)
    return _dot(m_bf16, hi) + _dot(m_bf16, lo)


def rwkv_chunks(r, k, v, lw, kk, b, g, bon, gn_w, gn_b, batch, seq):
    t, w = r.shape
    rows = RWKV_CHUNKS_PER_STEP * RWKV_CHUNK
    nc = seq // rows
    spec = pl.BlockSpec((rows, w), lambda bi, c: (bi * nc + c, 0))
    vspec = pl.BlockSpec((1, w), lambda bi, c: (0, 0))
    return pl.pallas_call(
        _rwkv_chunk_kernel,
        out_shape=jax.ShapeDtypeStruct((t, w), BF16),
        grid=(batch, nc),
        in_specs=[spec] * 8 + [vspec, vspec],
        out_specs=spec,
        scratch_shapes=[pltpu.VMEM((w // LANES, LANES, LANES), F32)],
        compiler_params=_cparams("parallel", "arbitrary"),
        name="rwkv_chunks",
    )(r, k, v, lw, kk, b, g, bon, gn_w.reshape(1, w), gn_b.reshape(1, w))


def _pad_rows(w, rows):
    return jnp.pad(w, ((0, rows - w.shape[0]), (0, 0)))


def _pad_cols(w, cols):
    return jnp.pad(w, ((0, 0), (0, cols - w.shape[1])))


def _rope_tables(seq):
    half = HEAD_DIM // 2
    inv_freq = ROPE_THETA ** (-jnp.arange(half, dtype=F32) / half)
    ang = jnp.arange(seq, dtype=F32)[:, None] * inv_freq[None, :]
    cos = jnp.cos(ang)
    sin = jnp.sin(ang)
    return jnp.concatenate([cos, cos], axis=1), jnp.concatenate([-sin, sin], axis=1)


def kernel(x, p, norm_mix_pre, norm_mix_post, norm_ffn_pre, norm_ffn_post, norm_ple_pre, norm_ple_post, w_in, w_merge_gate, lru_conv_w, lru_conv_b, lru_w_r, lru_b_r, lru_w_i, lru_b_i, lru_lambda, rwkv_mu, rwkv_w0, rwkv_w_up, rwkv_a0, rwkv_a_up, rwkv_g_up, rwkv_k_k, rwkv_k_a, rwkv_r_k, rwkv_gn_w, rwkv_gn_b, rwkv_v0, rwkv_v_down, rwkv_v_up, w_branch_a, w_branch_b, w_branch_c, w_branch_d, w_out, w_ffn_up, ffn_conv_w, ffn_conv_b, w_ffn_down, w_ple, w_ple_gate):
    batch, seq, d = x.shape
    depth = w_in.shape[0]
    t = batch * seq
    lru_w = lru_conv_w.shape[2]
    rw = rwkv_w0.shape[1]
    dil_w = 3 * len(DIL_CONFIGS) * DIL_HEADS * HEAD_DIM
    sb_w = 3 * (d // 2)
    off_b = 2 * lru_w
    off_c = off_b + dil_w
    off_d = off_c + sb_w
    rwkv_in = w_in.shape[2] - off_d
    rwkv_pad = 3 * rw + 4 * LANES
    cos, sin = _rope_tables(seq)

    xf = x.reshape(t, d)
    h = rmsnorm_bf16(xf, norm_mix_pre[0])
    v_first = None
    for i in range(depth):
        wi = w_in[i]
        w_a = wi[:, :off_b].astype(BF16)
        w_b = wi[:, off_b:off_c].astype(BF16)
        w_c = wi[:, off_c:off_d].astype(BF16)
        w_d = _pad_cols(wi[:, off_d:], rwkv_pad).astype(BF16)
        seg_a = matmul(h, w_a, F32)
        seg_b = matmul_rope(h, w_b, cos, sin, seq, 2 * dil_w // 3, BF16)
        seg_c = matmul(h, w_c, BF16)
        y_a = rglru(seg_a.reshape(batch, seq, off_b), lru_conv_w[i], lru_conv_b[i], lru_w_r[i].astype(BF16),
                    lru_b_r[i], lru_w_i[i].astype(BF16), lru_b_i[i], lru_lambda[i]).reshape(t, lru_w)
        y_b = dilated_attention(seg_b, batch, seq)
        y_c = stick_breaking(seg_c, batch, seq)
        mu = jnp.pad(rwkv_mu[i], (0, rwkv_pad - rwkv_in))
        w_up = _pad_rows(rwkv_w_up[i], LANES).astype(BF16)
        a_up = jnp.pad(rwkv_a_up[i], ((RWKV_W_LORA, LANES - RWKV_W_LORA - RWKV_A_LORA), (0, 0))).astype(BF16)
        g_up = _pad_rows(rwkv_g_up[i], 2 * LANES).astype(BF16)
        v_res = None
        if i > 0:
            v_res = (rwkv_v0[i - 1], _pad_cols(rwkv_v_down[i - 1], LANES).astype(BF16),
                     _pad_rows(rwkv_v_up[i - 1], LANES).astype(BF16))
        r_, k_, v_, lw_, kk_, b_, g_, bon_ = rwkv_prep(
            h, w_d, seq, mu, rwkv_w0[i], w_up, rwkv_a0[i], a_up, g_up, rwkv_k_k[i], rwkv_k_a[i],
            rwkv_r_k[i].reshape(-1), v_first, v_res)
        if i == 0:
            v_first = v_
        y_d = rwkv_chunks(r_, k_, v_, lw_, kk_, b_, g_, bon_, rwkv_gn_w[i], rwkv_gn_b[i], batch, seq)
        merged = merge_branches(
            h, (y_a, y_b, y_c, y_d), w_merge_gate[i].astype(BF16),
            (w_branch_a[i].astype(BF16), w_branch_b[i].astype(BF16), w_branch_c[i].astype(BF16),
             w_branch_d[i].astype(BF16)))
        xf, h = matmul_norm_res(merged, w_out[i].astype(BF16), xf, norm_mix_post[i], norm_ffn_pre[i], tm=512, tk=d)
        act = ffn_up(h, w_ffn_up, i, ffn_conv_w[i], ffn_conv_b[i], seq)
        xf, h = matmul_norm_res(act, w_ffn_down[i].astype(BF16), xf, norm_ffn_post[i], norm_ple_pre[i],
                                tm=512, tk=w_ffn_down.shape[1] // 4)
        g_next = norm_mix_pre[i + 1] if i + 1 < depth else norm_mix_pre[i]
        xf, h = ple_norm_res(p[i].reshape(t, -1).astype(BF16), h, w_ple[i].astype(BF16),
                             w_ple_gate[i].astype(BF16), xf, norm_ple_post[i], g_next)
    return xf.reshape(batch, seq, d)
```
